```python
import math
import jax
import jax.numpy as jnp
from jax import lax
import numpy as np

D_MODEL = 1024
BATCH = 8
SEQ = 8192
DEPTH = 2
DEC_BATCH = 32
DEC_SEQ = 16
PAST_LEN = 4096

CHUNK = 64
A_HEADS = 4
A_HEAD_DIM = 64
A_V_DIM = 2 * A_HEAD_DIM
A_QK_WIDTH = A_HEADS * 2 * A_HEAD_DIM
A_WIDTH = A_HEADS * A_V_DIM
ROPE_THETA = 10000.0
Q_BLOCK = 128
NEG_INF = -1e30
B_WIDTH = 512
B_BLOCKS = 8
B_BLOCK_DIM = B_WIDTH // B_BLOCKS
CONV_W = 4
LRU_C = 8.0
C_HEADS = 4
C_DK = 128
C_DV = 128
C_WIDTH = C_HEADS * C_DV
F_TINY = 1e-30
N_BRANCH = 3
MIX_WIDTH = A_WIDTH + B_WIDTH + C_WIDTH
IN_SIZES = (A_QK_WIDTH, A_QK_WIDTH, A_WIDTH, B_WIDTH, B_WIDTH,
            C_HEADS * C_DK, C_HEADS * C_DK, C_WIDTH, C_WIDTH, N_BRANCH * D_MODEL)
IN_SPLITS = tuple(int(s) for s in np.cumsum(IN_SIZES)[:-1])
N_IN = int(sum(IN_SIZES))
VALUE_SLOTS = (2, 3, 7)
N_EXPERTS = 16
N_GROUPS = 4
EXPERTS_PER_GROUP = N_EXPERTS // N_GROUPS
TOP_K = 2
D_EXPERT = 512
MOE_BLOCK = 128
DN_ALPHA = (2 * DEPTH) ** 0.25
DN_BETA = (8 * DEPTH) ** -0.25
NORM_EPS = 1e-5

kernel_name = 'hybrid_streaming_encoder_step'


def layer_norm(x, g, b):
    xf = x.astype(jnp.float32)
    mu = jnp.mean(xf, axis=-1, keepdims=True)
    xc = xf - mu
    var = jnp.mean(xc * xc, axis=-1, keepdims=True)
    return (xc * lax.rsqrt(var + NORM_EPS) * g.astype(jnp.float32) + b.astype(jnp.float32)).astype(x.dtype)


def rms_norm(x, g):
    xf = x.astype(jnp.float32)
    ms = jnp.mean(xf * xf, axis=-1, keepdims=True)
    return (xf * lax.rsqrt(ms + NORM_EPS) * g.astype(jnp.float32)).astype(x.dtype)


def rotary(x, pos):
    half = x.shape[-1] // 2
    inv_freq = ROPE_THETA ** (-jnp.arange(half, dtype=jnp.float32) / half)
    ang = pos.astype(jnp.float32)[:, None] * inv_freq[None, :]
    cos = jnp.cos(ang)[None, :, None, :]
    sin = jnp.sin(ang)[None, :, None, :]
    xf = x.astype(jnp.float32)
    x1, x2 = xf[..., :half], xf[..., half:]
    return jnp.concatenate([x1 * cos - x2 * sin, x2 * cos + x1 * sin], axis=-1).astype(x.dtype)


def diff_attend(q, k, v, lam, mask):
    s = jnp.einsum('bqhcd,bkhcd->bhcqk', q, k).astype(jnp.float32) * (A_HEAD_DIM ** -0.5)
    if mask is not None:
        s = jnp.where(mask, s, NEG_INF)
    p = jax.nn.softmax(s, axis=-1)
    w = p[:, :, 0] - lam * p[:, :, 1]
    return jnp.einsum('bhqk,bkhd->bqhd', w.astype(v.dtype), v)


def diff_attention_prompt(q, k, v, lam):
    Bn, L = q.shape[0], q.shape[1]
    nb = L // Q_BLOCK
    q_blocks = jnp.moveaxis(q.reshape(Bn, nb, Q_BLOCK, A_HEADS, 2, A_HEAD_DIM), 1, 0)
    key_chunk = jnp.arange(L) // CHUNK

    def one_block(args):
        qi, bi = args
        q_chunk = (bi * Q_BLOCK + jnp.arange(Q_BLOCK)) // CHUNK
        mask = key_chunk[None, :] <= q_chunk[:, None]
        return diff_attend(qi, k, v, lam, mask)

    o = lax.map(one_block, (q_blocks, jnp.arange(nb)))
    return jnp.moveaxis(o, 0, 1).reshape(Bn, L, A_HEADS, A_V_DIM)


def _linear_combine(left, right):
    return (left[0] * right[0], right[0] * left[1] + right[1])


def rglru_branch(xb, gb, conv_buf, h0, pos, conv_w, conv_b, wa, ba, wx, bx, lam):
    Bn, L, _ = xb.shape
    xpad = jnp.concatenate([conv_buf, xb], axis=1)
    xc = lax.conv_general_dilated(xpad, conv_w[:, None, :], window_strides=(1,), padding='VALID',
                                  dimension_numbers=('NWC', 'WIO', 'NWC'),
                                  feature_group_count=B_WIDTH) + conv_b
    xcb = xc.reshape(Bn, L, B_BLOCKS, B_BLOCK_DIM)
    r = jax.nn.sigmoid((jnp.einsum('blnd,nde->blne', xcb, wa).reshape(Bn, L, B_WIDTH) + ba).astype(jnp.float32))
    i = jax.nn.sigmoid((jnp.einsum('blnd,nde->blne', xcb, wx).reshape(Bn, L, B_WIDTH) + bx).astype(jnp.float32))
    log_a = -LRU_C * r * jax.nn.softplus(-lam.astype(jnp.float32))
    a = jnp.exp(log_a)
    mult = jnp.sqrt(-jnp.expm1(2.0 * log_a))
    mult = jnp.where((pos == 0)[None, :, None], 1.0, mult)
    b = xc.astype(jnp.float32) * i * mult
    b = b.at[:, 0].add(a[:, 0] * h0.astype(jnp.float32))
    _, h = lax.associative_scan(_linear_combine, (a, b), axis=1)
    y = (jax.nn.gelu(gb.astype(jnp.float32)) * h).astype(xb.dtype)
    return y, xpad[:, L:], h[:, -1].astype(h0.dtype)


def gla_chunk_scan(q, k, v, log_f, S0, chunk):
    Bn, L, H, _ = q.shape
    n = L // chunk

    def to_chunks(t):
        return t.reshape(Bn, n, chunk, H, t.shape[-1]).transpose(1, 0, 3, 2, 4)

    causal = jnp.tril(jnp.ones((chunk, chunk), dtype=bool))[:, :, None]

    def step(S, inp):
        qi, ki, vi, gi = inp
        b = jnp.cumsum(gi, axis=2)
        diff = b[:, :, :, None, :] - b[:, :, None, :, :]
        decay = jnp.where(causal, jnp.exp(jnp.minimum(diff, 0.0)), 0.0)
        scores = jnp.einsum('bhtd,bhsd,bhtsd->bhts', qi, ki, decay)
        o = jnp.einsum('bhts,bhsv->bhtv', scores, vi) + jnp.einsum('bhtd,bhdv->bhtv', qi * jnp.exp(b), S)
        b_last = b[:, :, -1:, :]
        S_new = (jnp.exp(b_last[:, :, 0, :])[..., None] * S
                 + jnp.einsum('bhsd,bhsv->bhdv', ki * jnp.exp(b_last - b), vi))
        return S_new, o

    S, o = lax.scan(step, S0, (to_chunks(q), to_chunks(k), to_chunks(v), to_chunks(log_f)))
    return o.transpose(1, 0, 3, 2, 4).reshape(Bn, L, H, v.shape[-1]), S


def hgrn2_branch(qc, fc, ic, gc, S0, lb, norm_g):
    Bn, L, _ = qc.shape
    shp = (Bn, L, C_HEADS, C_DK)
    z = fc.astype(jnp.float32)
    lb = lb.astype(jnp.float32)
    f = lb + (1.0 - lb) * jax.nn.sigmoid(z)
    log_f = jnp.log(jnp.maximum(f, F_TINY)).reshape(shp)
    k = ((1.0 - lb) * jax.nn.sigmoid(-z)).reshape(shp)
    q = jax.nn.silu(qc.astype(jnp.float32)).reshape(shp)
    v = ic.astype(jnp.float32).reshape(Bn, L, C_HEADS, C_DV)
    o, S = gla_chunk_scan(q, k, v, log_f, S0.astype(jnp.float32), min(CHUNK, L))
    gate = jax.nn.silu(gc.astype(jnp.float32)).reshape(Bn, L, C_HEADS, C_DV)
    y = rms_norm(o, norm_g) * gate
    return y.reshape(Bn, L, C_WIDTH).astype(qc.dtype), S.astype(S0.dtype)


def moe_ffn(x, router_w, router_b, w1, w3, w2):
    Bn, L, D = x.shape
    xt = x.reshape(-1, D)
    T = xt.shape[0]
    logits = (xt @ router_w + router_b).astype(jnp.float32)
    aff = jax.nn.softmax(logits, axis=-1)
    grp = aff.reshape(T, N_GROUPS, EXPERTS_PER_GROUP)
    g_sel = jnp.argmax(jnp.max(grp, axis=-1), axis=-1)
    in_grp = jnp.take_along_axis(grp, g_sel[:, None, None], axis=1)[:, 0]
    top_v, top_i = lax.top_k(in_grp, TOP_K)
    expert = (g_sel[:, None] * EXPERTS_PER_GROUP + top_i).astype(jnp.int32)
    gate = top_v / jnp.sum(top_v, axis=-1, keepdims=True)
    n_slots = T * TOP_K
    flat_e = expert.reshape(-1)
    order = jnp.argsort(flat_e)
    sorted_e = flat_e[order]
    counts = jnp.bincount(flat_e, length=N_EXPERTS)
    padded = (counts + MOE_BLOCK - 1) // MOE_BLOCK * MOE_BLOCK
    pad_end = jnp.cumsum(padded)
    pad_start = pad_end - padded
    start = jnp.cumsum(counts) - counts
    dest = pad_start[sorted_e] + jnp.arange(n_slots) - start[sorted_e]
    n_blocks = -(-n_slots // MOE_BLOCK) + N_EXPERTS
    tok = jnp.zeros((n_blocks * MOE_BLOCK,), jnp.int32).at[dest].set((order // TOP_K).astype(jnp.int32))
    blk_e = jnp.minimum(jnp.searchsorted(pad_end, jnp.arange(n_blocks) * MOE_BLOCK, side='right'),
                        N_EXPERTS - 1)
    xb = xt[tok].reshape(n_blocks, MOE_BLOCK, D)

    def expert_block(args):
        xi, e = args
        h = jax.nn.silu(xi @ w1[e]) * (xi @ w3[e])
        return h @ w2[e]

    yb = lax.map(expert_block, (xb, blk_e)).reshape(-1, D)
    y_slots = jnp.zeros((n_slots, D), x.dtype).at[order].set(yb[dest])
    y = jnp.einsum('tkd,tk->td', y_slots.reshape(T, TOP_K, D), gate.astype(x.dtype))
    return y.reshape(Bn, L, D)


def trunk_layer(x, pos, li, W, lb, past):
    Bn, L, _ = x.shape
    dt = x.dtype
    w_cols = jnp.split(W['w_in'][li], IN_SPLITS, axis=1)
    b_cols = jnp.split(W['b_in'][li], IN_SPLITS, axis=0)
    qa, ka, va, xb, gb, qc, fc, ic, gc, mg = [x @ w + b for w, b in zip(w_cols, b_cols)]

    q = rotary(qa.reshape(Bn, L, 2 * A_HEADS, A_HEAD_DIM), pos).reshape(Bn, L, A_HEADS, 2, A_HEAD_DIM)
    k = rotary(ka.reshape(Bn, L, 2 * A_HEADS, A_HEAD_DIM), pos).reshape(Bn, L, A_HEADS, 2, A_HEAD_DIM)
    v = va.reshape(Bn, L, A_HEADS, A_V_DIM)
    lam_init = 0.8 - 0.6 * math.exp(-0.3 * li)
    dl = W['diff_lambda'][li].astype(jnp.float32)
    lam = jnp.exp(jnp.sum(dl[0] * dl[1])) - jnp.exp(jnp.sum(dl[2] * dl[3])) + lam_init
    if past is None:
        o_a = diff_attention_prompt(q, k, v, lam)
        conv_buf = jnp.zeros((Bn, CONV_W - 1, B_WIDTH), dt)
        h0 = jnp.zeros((Bn, B_WIDTH), dt)
        S0 = jnp.zeros((Bn, C_HEADS, C_DK, C_DV), dt)
    else:
        k_past, v_past, conv_buf, h0, S0 = past
        o_a = diff_attend(q, jnp.concatenate([k_past, k], axis=1), jnp.concatenate([v_past, v], axis=1), lam, None)
    y_a = (rms_norm(o_a, W['diff_subln_g'][li]) * (1.0 - lam_init)).reshape(Bn, L, A_WIDTH).astype(dt)

    y_b, conv_new, h_new = rglru_branch(xb, gb, conv_buf, h0, pos, W['conv_w'][li], W['conv_b'][li],
                                        W['lru_wa'][li], W['lru_ba'][li], W['lru_wx'][li], W['lru_bx'][li],
                                        W['lru_lambda'][li])
    y_c, S_new = hgrn2_branch(qc, fc, ic, gc, S0, lb, W['hgrn_norm_g'][li])

    wo = W['w_out'][li]
    g = jax.nn.sigmoid(mg.astype(jnp.float32)).astype(dt).reshape(Bn, L, N_BRANCH, D_MODEL)
    mixed = (g[:, :, 0] * (y_a @ wo[:A_WIDTH])
             + g[:, :, 1] * (y_b @ wo[A_WIDTH:A_WIDTH + B_WIDTH])
             + g[:, :, 2] * (y_c @ wo[A_WIDTH + B_WIDTH:]))
    x = layer_norm(DN_ALPHA * x + mixed, W['ln1_g'][li], W['ln1_b'][li])
    ffn = moe_ffn(x, W['router_w'], W['router_b'], W['moe_w1'][li], W['moe_w3'][li], W['moe_w2'][li])
    x = layer_norm(DN_ALPHA * x + ffn, W['ln2_g'][li], W['ln2_b'][li])
    return x, (k, v, conv_new, h_new, S_new)


def setup_inputs(seed: int = 0) -> dict:
    key = jax.random.key(seed)
    ks = jax.random.split(key, 40)
    f32 = jnp.float32

    def nrm(k, shape, s):
        return s * jax.random.normal(k, shape, f32)

    col_scale = np.ones((N_IN,), np.float32)
    offs = (0,) + IN_SPLITS
    for idx in VALUE_SLOTS:
        col_scale[offs[idx]:offs[idx] + IN_SIZES[idx]] = DN_BETA
    a8 = jax.random.uniform(ks[19], (DEPTH, B_WIDTH), f32, 0.9, 0.999)
    a_base = a8 ** (1.0 / LRU_C)
    return {
        'x_prompt': nrm(ks[0], (BATCH, SEQ, D_MODEL), 1.0),
        'x_sample': nrm(ks[1], (DEC_BATCH, DEC_SEQ, D_MODEL), 1.0),
        'cache_k': nrm(ks[2], (DEPTH, DEC_BATCH, PAST_LEN, A_HEADS, 2, A_HEAD_DIM), 1.0),
        'cache_v': nrm(ks[3], (DEPTH, DEC_BATCH, PAST_LEN, A_HEADS, A_V_DIM), DN_BETA),
        'state_conv': nrm(ks[4], (DEPTH, DEC_BATCH, CONV_W - 1, B_WIDTH), DN_BETA),
        'state_lru': nrm(ks[5], (DEPTH, DEC_BATCH, B_WIDTH), 0.5),
        'state_hgrn': nrm(ks[6], (DEPTH, DEC_BATCH, C_HEADS, C_DK, C_DV), 0.5),
        'ln_in_g': 1.0 + nrm(ks[7], (D_MODEL,), 0.02),
        'ln_in_b': nrm(ks[8], (D_MODEL,), 0.02),
        'w_in': nrm(ks[9], (DEPTH, D_MODEL, N_IN), D_MODEL ** -0.5) * jnp.asarray(col_scale),
        'b_in': nrm(ks[10], (DEPTH, N_IN), 0.02),
        'diff_lambda': nrm(ks[11], (DEPTH, 4, A_HEAD_DIM), 0.1),
        'diff_subln_g': 1.0 + nrm(ks[12], (DEPTH, A_V_DIM), 0.02),
        'conv_w': nrm(ks[13], (DEPTH, CONV_W, B_WIDTH), CONV_W ** -0.5),
        'conv_b': nrm(ks[14], (DEPTH, B_WIDTH), 0.02),
        'lru_wa': nrm(ks[15], (DEPTH, B_BLOCKS, B_BLOCK_DIM, B_BLOCK_DIM), B_BLOCK_DIM ** -0.5),
        'lru_ba': nrm(ks[16], (DEPTH, B_WIDTH), 0.02),
        'lru_wx': nrm(ks[17], (DEPTH, B_BLOCKS, B_BLOCK_DIM, B_BLOCK_DIM), B_BLOCK_DIM ** -0.5),
        'lru_bx': nrm(ks[18], (DEPTH, B_WIDTH), 0.02),
        'lru_lambda': jnp.log(a_base) - jnp.log1p(-a_base),
        'hgrn_lb': nrm(ks[20], (DEPTH, C_HEADS * C_DK), 0.1),
        'hgrn_norm_g': 1.0 + nrm(ks[21], (DEPTH, C_DV), 0.02),
        'w_out': nrm(ks[22], (DEPTH, MIX_WIDTH, D_MODEL), MIX_WIDTH ** -0.5 * DN_BETA),
        'ln1_g': 1.0 + nrm(ks[23], (DEPTH, D_MODEL), 0.02),
        'ln1_b': nrm(ks[24], (DEPTH, D_MODEL), 0.02),
        'router_w': nrm(ks[25], (D_MODEL, N_EXPERTS), D_MODEL ** -0.5),
        'router_b': nrm(ks[26], (N_EXPERTS,), 0.01),
        'moe_w1': nrm(ks[27], (DEPTH, N_EXPERTS, D_MODEL, D_EXPERT), D_MODEL ** -0.5 * DN_BETA),
        'moe_w3': nrm(ks[28], (DEPTH, N_EXPERTS, D_MODEL, D_EXPERT), D_MODEL ** -0.5 * DN_BETA),
        'moe_w2': nrm(ks[29], (DEPTH, N_EXPERTS, D_EXPERT, D_MODEL), D_EXPERT ** -0.5 * DN_BETA),
        'ln2_g': 1.0 + nrm(ks[30], (DEPTH, D_MODEL), 0.02),
        'ln2_b': nrm(ks[31], (DEPTH, D_MODEL), 0.02),
    }


def reference(x_prompt, x_sample, cache_k, cache_v, state_conv, state_lru, state_hgrn,
              ln_in_g, ln_in_b, w_in, b_in, diff_lambda, diff_subln_g, conv_w, conv_b,
              lru_wa, lru_ba, lru_wx, lru_bx, lru_lambda, hgrn_lb, hgrn_norm_g, w_out,
              ln1_g, ln1_b, router_w, router_b, moe_w1, moe_w3, moe_w2, ln2_g, ln2_b):
    W = dict(w_in=w_in, b_in=b_in, diff_lambda=diff_lambda, diff_subln_g=diff_subln_g,
             conv_w=conv_w, conv_b=conv_b, lru_wa=lru_wa, lru_ba=lru_ba, lru_wx=lru_wx,
             lru_bx=lru_bx, lru_lambda=lru_lambda, hgrn_norm_g=hgrn_norm_g, w_out=w_out,
             ln1_g=ln1_g, ln1_b=ln1_b, router_w=router_w, router_b=router_b,
             moe_w1=moe_w1, moe_w3=moe_w3, moe_w2=moe_w2, ln2_g=ln2_g, ln2_b=ln2_b)
    p_lb = jax.nn.softmax(hgrn_lb.astype(jnp.float32), axis=0)
    lower_bounds = jnp.cumsum(p_lb, axis=0) - p_lb[0]
    pos_prompt = jnp.arange(x_prompt.shape[1])
    pos_sample = cache_k.shape[2] + jnp.arange(x_sample.shape[1])
    xp = layer_norm(x_prompt, ln_in_g, ln_in_b)
    xs = layer_norm(x_sample, ln_in_g, ln_in_b)
    prompt_states = []
    sample_states = []
    for li in range(DEPTH):
        xp, st_p = trunk_layer(xp, pos_prompt, li, W, lower_bounds[li], None)
        xs, st_s = trunk_layer(xs, pos_sample, li, W, lower_bounds[li],
                               (cache_k[li], cache_v[li], state_conv[li], state_lru[li], state_hgrn[li]))
        prompt_states.append(st_p)
        sample_states.append(st_s)
    k_prompt, v_prompt, conv_prompt, lru_prompt, hgrn_prompt = [jnp.stack(t) for t in zip(*prompt_states)]
    k_sample, v_sample, conv_sample, lru_sample, hgrn_sample = [jnp.stack(t) for t in zip(*sample_states)]
    return (xp, xs, k_prompt, v_prompt, conv_prompt, lru_prompt, hgrn_prompt,
            k_sample, v_sample, conv_sample, lru_sample, hgrn_sample)
```

```python
import functools
import math

import numpy as np
import jax
import jax.numpy as jnp
from jax import lax
from jax.experimental import pallas as pl
from jax.experimental.pallas import tpu as pltpu

F32 = jnp.float32
BF16 = jnp.bfloat16

D_MODEL = 1024
CHUNK = 64
A_HEADS = 4
A_HEAD_DIM = 64
A_V_DIM = 2 * A_HEAD_DIM
A_QK_WIDTH = A_HEADS * 2 * A_HEAD_DIM
A_WIDTH = A_HEADS * A_V_DIM
ROPE_THETA = 10000.0
NEG_INF = -1e30
B_WIDTH = 512
B_BLOCKS = 8
CONV_W = 4
LRU_C = 8.0
C_HEADS = 4
C_DK = 128
C_DV = 128
C_WIDTH = C_HEADS * C_DV
F_TINY = 1e-30
N_BRANCH = 3
N_EXPERTS = 16
N_GROUPS = 4
EXPERTS_PER_GROUP = N_EXPERTS // N_GROUPS
D_EXPERT = 512
NORM_EPS = 1e-5

LANES = 128
PAIRS_PER_GROUP = 6
N_CLASSES = N_GROUPS * PAIRS_PER_GROUP
MOE_ROWS = 256
AUX_W = LANES
VMEM_LIMIT = 56 * 1024 * 1024

OFF_A = 0
OFF_B = OFF_A + 2 * A_QK_WIDTH + A_WIDTH
OFF_C = OFF_B + 2 * B_WIDTH
OFF_G = OFF_C + 4 * C_WIDTH
N_IN = OFF_G + N_BRANCH * D_MODEL


def _cparams(*sem):
    return pltpu.CompilerParams(dimension_semantics=sem, vmem_limit_bytes=VMEM_LIMIT)


def _dot(a, b):
    return jnp.dot(a, b, preferred_element_type=F32)


def _dot_nt(a, b):
    return lax.dot_general(a, b, (((1,), (1,)), ((), ())), preferred_element_type=F32)


def _dot_tn(a, b):
    return lax.dot_general(a, b, (((0,), (0,)), ((), ())), preferred_element_type=F32)


def _sigmoid(x):
    return 1.0 / (1.0 + jnp.exp(-x))


def _silu(x):
    return x * _sigmoid(x)


def _layer_norm(x, g, b):
    mu = jnp.mean(x, axis=-1, keepdims=True)
    xc = x - mu
    var = jnp.mean(xc * xc, axis=-1, keepdims=True)
    return xc * lax.rsqrt(var + NORM_EPS) * g + b


def _ln_kernel(x_ref, g_ref, b_ref, o_ref):
    o_ref[...] = _layer_norm(x_ref[...], g_ref[...], b_ref[...])


def _input_ln(x2d, g, b):
    T = x2d.shape[0]
    tm = min(T, 512)
    return pl.pallas_call(
        _ln_kernel,
        grid=(T // tm,),
        in_specs=[pl.BlockSpec((tm, D_MODEL), lambda i: (i, 0)),
                  pl.BlockSpec((1, D_MODEL), lambda i: (0, 0)),
                  pl.BlockSpec((1, D_MODEL), lambda i: (0, 0))],
        out_specs=pl.BlockSpec((tm, D_MODEL), lambda i: (i, 0)),
        out_shape=jax.ShapeDtypeStruct((T, D_MODEL), F32),
        compiler_params=_cparams("parallel"),
        name="input_ln",
    )(x2d, g.reshape(1, -1), b.reshape(1, -1))


def _attn_proj_kernel(x_ref, w_ref, b_ref, cos_ref, sin_ref, q_ref, k_ref, kb_ref, v_ref, vb_ref):
    x = x_ref[0].astype(BF16)
    y = _dot(x, w_ref[...]) + b_ref[...]
    cos = cos_ref[...]
    sin = sin_ref[...]
    lane = lax.broadcasted_iota(jnp.int32, cos.shape, 1)
    first_half = (lane % A_HEAD_DIM) < (A_HEAD_DIM // 2)
    for s in range(2 * A_QK_WIDTH // LANES):
        blk = y[:, s * LANES:(s + 1) * LANES]
        swapped = jnp.where(first_half,
                            pltpu.roll(blk, LANES - A_HEAD_DIM // 2, 1),
                            pltpu.roll(blk, A_HEAD_DIM // 2, 1))
        rot = blk * cos + swapped * sin
        if s < A_QK_WIDTH // LANES:
            q_ref[0, :, s * LANES:(s + 1) * LANES] = (rot * (A_HEAD_DIM ** -0.5)).astype(BF16)
        else:
            o = s * LANES - A_QK_WIDTH
            k_ref[0, :, o:o + LANES] = rot
            kb_ref[0, :, o:o + LANES] = rot.astype(BF16)
    v = y[:, 2 * A_QK_WIDTH:]
    v_ref[0] = v
    vb_ref[0] = v.astype(BF16)


def _rope_tables(pos0, L):
    half = A_HEAD_DIM // 2
    inv_freq = ROPE_THETA ** (-jnp.arange(half, dtype=F32) / half)
    pos = (pos0 + jnp.arange(L)).astype(F32)
    ang = pos[:, None] * inv_freq[None, :]
    cos = jnp.cos(ang)
    sin = jnp.sin(ang)
    reps = LANES // A_HEAD_DIM
    cos_t = jnp.tile(jnp.concatenate([cos, cos], axis=1), (1, reps))
    sin_t = jnp.tile(jnp.concatenate([-sin, sin], axis=1), (1, reps))
    return cos_t, sin_t


def _attn_proj(x, w, b, pos0):
    B, L, _ = x.shape
    tl = min(L, 512)
    cos_t, sin_t = _rope_tables(pos0, L)
    wa = A_QK_WIDTH
    n = 2 * A_QK_WIDTH + A_WIDTH
    row = lambda bi, j: (bi, j, 0)
    return pl.pallas_call(
        _attn_proj_kernel,
        grid=(B, L // tl),
        in_specs=[pl.BlockSpec((1, tl, D_MODEL), row),
                  pl.BlockSpec((D_MODEL, n), lambda bi, j: (0, 0)),
                  pl.BlockSpec((1, n), lambda bi, j: (0, 0)),
                  pl.BlockSpec((tl, LANES), lambda bi, j: (j, 0)),
                  pl.BlockSpec((tl, LANES), lambda bi, j: (j, 0))],
        out_specs=[pl.BlockSpec((1, tl, wa), row)] * 5,
        out_shape=[jax.ShapeDtypeStruct((B, L, wa), BF16),
                   jax.ShapeDtypeStruct((B, L, wa), F32),
                   jax.ShapeDtypeStruct((B, L, wa), BF16),
                   jax.ShapeDtypeStruct((B, L, wa), F32),
                   jax.ShapeDtypeStruct((B, L, wa), BF16)],
        compiler_params=_cparams("parallel", "parallel"),
        name="attn_proj",
    )(x, w, b, cos_t, sin_t)


def _softmax_update(c, qz, kb, vb, mask, m_s, l_s, acc_s):
    s = _dot_nt(qz, kb)
    if mask is not None:
        s = jnp.where(mask, s, NEG_INF)
    m_old = m_s[c]
    m_new = jnp.maximum(m_old, jnp.max(s, axis=-1, keepdims=True))
    alpha = jnp.exp(m_old - m_new)
    p = jnp.exp(s - m_new)
    l_s[c] = alpha * l_s[c] + jnp.sum(p, axis=-1, keepdims=True)
    acc_s[c] = alpha * acc_s[c] + _dot(p.astype(BF16), vb)
    m_s[c] = m_new


def _diff_finish(c0, lam, g, out_scale, l_s, acc_s):
    o = acc_s[c0] / l_s[c0] - lam * (acc_s[c0 + 1] / l_s[c0 + 1])
    ms = jnp.mean(o * o, axis=-1, keepdims=True)
    return o * lax.rsqrt(ms + NORM_EPS) * g * out_scale


def _split_maps(q):
    lane = lax.broadcasted_iota(jnp.int32, q.shape, 1)
    zero = jnp.zeros_like(q)
    return jnp.where(lane < A_HEAD_DIM, q, zero), jnp.where(lane >= A_HEAD_DIM, q, zero)


def _flash_kernel(lam_ref, g_ref, q_ref, k_ref, v_ref, o_ref, m_s, l_s, acc_s, *, tq, out_scale):
    i = pl.program_id(2)
    qz = _split_maps(q_ref[0])
    m_s[...] = jnp.full(m_s.shape, NEG_INF, F32)
    l_s[...] = jnp.zeros(l_s.shape, F32)
    acc_s[...] = jnp.zeros(acc_s.shape, F32)

    def body(j, carry):
        off = pl.multiple_of(j * tq, tq)
        kb = k_ref[0, pl.ds(off, tq), :]
        vb = v_ref[0, pl.ds(off, tq), :]
        for c in range(2):
            _softmax_update(c, qz[c], kb, vb, None, m_s, l_s, acc_s)
        return carry

    lax.fori_loop(0, i, body, 0)
    off = pl.multiple_of(i * tq, tq)
    kb = k_ref[0, pl.ds(off, tq), :]
    vb = v_ref[0, pl.ds(off, tq), :]
    row = lax.broadcasted_iota(jnp.int32, (tq, tq), 0) // CHUNK
    col = lax.broadcasted_iota(jnp.int32, (tq, tq), 1) // CHUNK
    mask = col <= row
    for c in range(2):
        _softmax_update(c, qz[c], kb, vb, mask, m_s, l_s, acc_s)
    o_ref[0] = _diff_finish(0, lam_ref[...], g_ref[...], out_scale, l_s, acc_s).astype(o_ref.dtype)


def _flash_prompt(q, kb, vb, lam_row, g_row, out_scale):
    B, L, _ = q.shape
    tq = min(L, 256)
    hb = lambda b, h, i: (b, 0, h)
    return pl.pallas_call(
        functools.partial(_flash_kernel, tq=tq, out_scale=out_scale),
        grid=(B, A_HEADS, L // tq),
        in_specs=[pl.BlockSpec((1, LANES), lambda b, h, i: (0, 0)),
                  pl.BlockSpec((1, A_V_DIM), lambda b, h, i: (0, 0)),
                  pl.BlockSpec((1, tq, LANES), lambda b, h, i: (b, i, h)),
                  pl.BlockSpec((1, L, LANES), hb),
                  pl.BlockSpec((1, L, A_V_DIM), hb)],
        out_specs=pl.BlockSpec((1, tq, A_V_DIM), lambda b, h, i: (b, i, h)),
        out_shape=jax.ShapeDtypeStruct((B, L, A_WIDTH), BF16),
        scratch_shapes=[pltpu.VMEM((2, tq, 1), F32), pltpu.VMEM((2, tq, 1), F32),
                        pltpu.VMEM((2, tq, A_V_DIM), F32)],
        compiler_params=_cparams("parallel", "parallel", "arbitrary"),
        name="flash_prompt",
    )(lam_row, g_row, q, kb, vb)


def _decode_kernel(lam_ref, g_ref, q_ref, kn_ref, vn_ref, kp_ref, vp_ref, o_ref, m_s, l_s, acc_s, *, out_scale):
    j = pl.program_id(1)

    @pl.when(j == 0)
    def _():
        m_s[...] = jnp.full(m_s.shape, NEG_INF, F32)
        l_s[...] = jnp.zeros(l_s.shape, F32)
        acc_s[...] = jnp.zeros(acc_s.shape, F32)

    def attend(k_full, v_full):
        for h in range(A_HEADS):
            hs = slice(h * LANES, (h + 1) * LANES)
            qz = _split_maps(q_ref[0, :, hs])
            kb = k_full[:, hs].astype(BF16)
            vb = v_full[:, hs].astype(BF16)
            for c in range(2):
                _softmax_update(2 * h + c, qz[c], kb, vb, None, m_s, l_s, acc_s)

    attend(kp_ref[0], vp_ref[0])

    @pl.when(j == pl.num_programs(1) - 1)
    def _():
        attend(kn_ref[0], vn_ref[0])
        for h in range(A_HEADS):
            y = _diff_finish(2 * h, lam_ref[...], g_ref[...], out_scale, l_s, acc_s)
            o_ref[0, :, h * A_V_DIM:(h + 1) * A_V_DIM] = y.astype(o_ref.dtype)


def _decode_attention(q, k_new, v_new, k_past, v_past, lam_row, g_row, out_scale):
    B, L, _ = q.shape
    P = k_past.shape[1]
    tk = min(P, 512)
    new = lambda b, j: (b, 0, 0)
    past = lambda b, j: (b, j, 0)
    return pl.pallas_call(
        functools.partial(_decode_kernel, out_scale=out_scale),
        grid=(B, P // tk),
        in_specs=[pl.BlockSpec((1, LANES), lambda b, j: (0, 0)),
                  pl.BlockSpec((1, A_V_DIM), lambda b, j: (0, 0)),
                  pl.BlockSpec((1, L, A_QK_WIDTH), new),
                  pl.BlockSpec((1, L, A_QK_WIDTH), new),
                  pl.BlockSpec((1, L, A_WIDTH), new),
                  pl.BlockSpec((1, tk, A_QK_WIDTH), past),
                  pl.BlockSpec((1, tk, A_WIDTH), past)],
        out_specs=pl.BlockSpec((1, L, A_WIDTH), new),
        out_shape=jax.ShapeDtypeStruct((B, L, A_WIDTH), BF16),
        scratch_shapes=[pltpu.VMEM((2 * A_HEADS, L, 1), F32), pltpu.VMEM((2 * A_HEADS, L, 1), F32),
                        pltpu.VMEM((2 * A_HEADS, L, A_V_DIM), F32)],
        compiler_params=_cparams("parallel", "arbitrary"),
        name="decode_attention",
    )(lam_row, g_row, q, k_new, v_new, k_past, v_past)


def _gelu_tanh(x):
    return 0.5 * x * (1.0 + jnp.tanh(math.sqrt(2.0 / math.pi) * (x + 0.044715 * (x * x * x))))


def _rglru_kernel(x_ref, w_ref, b_ref, cw_ref, cb_ref, gw_ref, gbias_ref, sp_ref, cbuf_ref, h0_ref,
                  y_ref, conv_ref, h_ref, xs, hc, *, tl, pos0):
    j = pl.program_id(1)
    tail = CONV_W - 1
    base = 8

    @pl.when(j == 0)
    def _():
        xs[0:base, :] = jnp.zeros((base, B_WIDTH), F32)
        xs[base - tail:base, :] = cbuf_ref[0]
        hc[...] = h0_ref[0]

    x = x_ref[0].astype(BF16)
    y = _dot(x, w_ref[...]) + b_ref[...]
    xb = y[:, :B_WIDTH]
    gb = y[:, B_WIDTH:]
    xs[base:base + tl, :] = xb
    xc = cb_ref[...] + cw_ref[0:1, :] * xs[base - tail:base - tail + tl, :]
    for t in range(1, CONV_W):
        xc = xc + cw_ref[t:t + 1, :] * xs[base - tail + t:base - tail + t + tl, :]
    new_tail = xs[base + tl - tail:base + tl, :]
    xs[base - tail:base, :] = new_tail
    conv_ref[0] = new_tail

    gates = _dot(xc.astype(BF16), gw_ref[...]) + gbias_ref[...]
    r = _sigmoid(gates[:, :B_WIDTH])
    ig = _sigmoid(gates[:, B_WIDTH:])
    log_a = -LRU_C * r * sp_ref[...]
    a = jnp.exp(log_a)
    t = jnp.tanh(log_a)
    mult = jnp.sqrt(-2.0 * t / (1.0 - t))
    row = lax.broadcasted_iota(jnp.int32, (tl, B_WIDTH), 0)
    mult = jnp.where(row + (pos0 + j * tl) == 0, 1.0, mult)
    bb = xc * ig * mult
    d = 1
    while d < tl:
        keep = row >= d
        a_sh = pltpu.roll(a, d, 0)
        b_sh = pltpu.roll(bb, d, 0)
        bb = jnp.where(keep, a * b_sh + bb, bb)
        a = jnp.where(keep, a * a_sh, a)
        d *= 2
    h = a * hc[...] + bb
    h_last = h[tl - 1:tl, :]
    hc[...] = h_last
    h_ref[0] = h_last
    y_ref[0] = (_gelu_tanh(gb) * h).astype(y_ref.dtype)


def _rglru(x, w, b, conv_w, conv_b, gate_w, gate_b, sp, conv_buf, h0, pos0):
    B, L, _ = x.shape
    tl = min(L, 256)
    tail = CONV_W - 1
    const = lambda bi, j: (0, 0)
    per_b = lambda bi, j: (bi, 0, 0)
    return pl.pallas_call(
        functools.partial(_rglru_kernel, tl=tl, pos0=pos0),
        grid=(B, L // tl),
        in_specs=[pl.BlockSpec((1, tl, D_MODEL), lambda bi, j: (bi, j, 0)),
                  pl.BlockSpec((D_MODEL, 2 * B_WIDTH), const),
                  pl.BlockSpec((1, 2 * B_WIDTH), const),
                  pl.BlockSpec((CONV_W, B_WIDTH), const),
                  pl.BlockSpec((1, B_WIDTH), const),
                  pl.BlockSpec((B_WIDTH, 2 * B_WIDTH), const),
                  pl.BlockSpec((1, 2 * B_WIDTH), const),
                  pl.BlockSpec((1, B_WIDTH), const),
                  pl.BlockSpec((1, tail, B_WIDTH), per_b),
                  pl.BlockSpec((1, 1, B_WIDTH), per_b)],
        out_specs=[pl.BlockSpec((1, tl, B_WIDTH), lambda bi, j: (bi, j, 0)),
                   pl.BlockSpec((1, tail, B_WIDTH), per_b),
                   pl.BlockSpec((1, 1, B_WIDTH), per_b)],
        out_shape=[jax.ShapeDtypeStruct((B, L, B_WIDTH), BF16),
                   jax.ShapeDtypeStruct((B, tail, B_WIDTH), F32),
                   jax.ShapeDtypeStruct((B, 1, B_WIDTH), F32)],
        scratch_shapes=[pltpu.VMEM((8 + tl, B_WIDTH), F32), pltpu.VMEM((1, B_WIDTH), F32)],
        compiler_params=_cparams("parallel", "arbitrary"),
        name="rglru",
    )(x, w, b, conv_w, conv_b, gate_w, gate_b, sp, conv_buf, h0)


def _hgrn_levels(C):
    return [C >> i for i in range(int(math.log2(C)) + 1)]


def _hgrn_sum_matrices(C):
    t = np.arange(C)
    mats = []
    for m in _hgrn_levels(C):
        start = (t // m) * m
        mats.append((t[None, :] >= start[:, None]) & (t[None, :] <= t[:, None]))
    for m in _hgrn_levels(C):
        end = (t // m) * m + m - 1
        mats.append((t[None, :] > t[:, None]) & (t[None, :] <= end[:, None]))
    return np.concatenate(mats, axis=0).astype(np.float32)


def _hgrn_kernel(x_ref, w_ref, b_ref, lb_ref, ng_ref, dm_ref, s0_ref, y_ref, s_ref,
                 q_s, k_s, g_s, v_s, gate_s, o_s, st_s, *, tl, C):
    j = pl.program_id(1)
    levels = _hgrn_levels(C)
    nlev = len(levels)

    @pl.when(j == 0)
    def _():
        for h in range(C_HEADS):
            st_s[h] = s0_ref[0, h].T

    x = x_ref[0].astype(BF16)
    y = _dot(x, w_ref[...]) + b_ref[...]
    lb = lb_ref[...]
    z = y[:, C_WIDTH:2 * C_WIDTH]
    f = lb + (1.0 - lb) * _sigmoid(z)
    q_s[...] = _silu(y[:, :C_WIDTH])
    k_s[...] = (1.0 - lb) * _sigmoid(-z)
    g_s[...] = jnp.log(jnp.maximum(f, F_TINY))
    v_s[...] = y[:, 2 * C_WIDTH:3 * C_WIDTH]
    gate_s[...] = _silu(y[:, 3 * C_WIDTH:])

    ti = lax.broadcasted_iota(jnp.int32, (C, C), 0)
    si = lax.broadcasted_iota(jnp.int32, (C, C), 1)
    masks = [ti == si]
    for m in levels[1:]:
        masks.append((ti // (2 * m) == si // (2 * m)) & ((ti // m) % 2 == 1) & ((si // m) % 2 == 0))

    def chunk(ci, carry):
        r0 = pl.multiple_of(ci * C, C)
        rows = pl.ds(r0, C)
        for h in range(C_HEADS):
            hs = slice(h * C_DK, (h + 1) * C_DK)
            g = g_s[rows, hs]
            g_hi = g.astype(BF16)
            r1 = g - g_hi.astype(F32)
            g_mid = r1.astype(BF16)
            g_lo = (r1 - g_mid.astype(F32)).astype(BF16)
            dm = dm_ref[...]
            sums = _dot(dm, g_hi) + _dot(dm, g_mid) + _dot(dm, g_lo)
            e = jnp.exp(jnp.minimum(sums, 0.0))
            q = q_s[rows, hs]
            k = k_s[rows, hs]
            vb = v_s[rows, hs].astype(BF16)
            scores = jnp.where(masks[0], _dot_nt(q.astype(BF16), k.astype(BF16)), 0.0)
            for li in range(1, nlev):
                ql = (q * e[li * C:(li + 1) * C]).astype(BF16)
                kl = (k * e[(nlev + li) * C:(nlev + li + 1) * C]).astype(BF16)
                scores = scores + jnp.where(masks[li], _dot_nt(ql, kl), 0.0)
            e_cum = e[0:C]
            qc = (q * e_cum).astype(BF16)
            kc = (k * e[nlev * C:(nlev + 1) * C]).astype(BF16)
            st = st_s[h]
            o = _dot(scores.astype(BF16), vb) + _dot_nt(qc, st.astype(BF16))
            st_s[h] = st * e_cum[C - 1:C, :] + _dot_tn(vb, kc)
            ms = jnp.mean(o * o, axis=-1, keepdims=True)
            o_s[rows, hs] = o * lax.rsqrt(ms + NORM_EPS) * ng_ref[...] * gate_s[rows, hs]
        return carry

    lax.fori_loop(0, tl // C, chunk, 0)
    y_ref[0] = o_s[...].astype(y_ref.dtype)
    for h in range(C_HEADS):
        s_ref[0, h] = st_s[h].T


def _hgrn(x, w, b, lb, norm_g, s0):
    B, L, _ = x.shape
    C = min(CHUNK, L)
    tl = min(L, 256)
    dm = jnp.asarray(_hgrn_sum_matrices(C), BF16)
    const = lambda bi, j: (0, 0)
    per_b = lambda bi, j: (bi, 0, 0, 0)
    wide = pltpu.VMEM((tl, C_WIDTH), F32)
    return pl.pallas_call(
        functools.partial(_hgrn_kernel, tl=tl, C=C),
        grid=(B, L // tl),
        in_specs=[pl.BlockSpec((1, tl, D_MODEL), lambda bi, j: (bi, j, 0)),
                  pl.BlockSpec((D_MODEL, 4 * C_WIDTH), const),
                  pl.BlockSpec((1, 4 * C_WIDTH), const),
                  pl.BlockSpec((1, C_WIDTH), const),
                  pl.BlockSpec((1, C_DV), const),
                  pl.BlockSpec(dm.shape, const),
                  pl.BlockSpec((1, C_HEADS, C_DK, C_DV), per_b)],
        out_specs=[pl.BlockSpec((1, tl, C_WIDTH), lambda bi, j: (bi, j, 0)),
                   pl.BlockSpec((1, C_HEADS, C_DK, C_DV), per_b)],
        out_shape=[jax.ShapeDtypeStruct((B, L, C_WIDTH), BF16),
                   jax.ShapeDtypeStruct((B, C_HEADS, C_DK, C_DV), F32)],
        scratch_shapes=[wide, wide, wide, wide, wide, wide, pltpu.VMEM((C_HEADS, C_DV, C_DK), F32)],
        compiler_params=_cparams("parallel", "arbitrary"),
        name="hgrn",
    )(x, w, b, lb, norm_g, dm, s0)


def _merge_kernel(x_ref, ya_ref, yb_ref, yc_ref, wg_ref, bg_ref, wo_ref, g1_ref, b1_ref, rw_ref, rb_ref,
                  o_ref, *, alpha):
    x = x_ref[...]
    xb = x.astype(BF16)
    mixed = None
    for n, y_ref in enumerate((ya_ref, yb_ref, yc_ref)):
        cols = slice(n * D_MODEL, (n + 1) * D_MODEL)
        gate = _sigmoid(_dot(xb, wg_ref[:, cols]) + bg_ref[:, cols])
        rows = slice(n * A_WIDTH, (n + 1) * A_WIDTH)
        term = gate * _dot(y_ref[...], wo_ref[rows, :])
        mixed = term if mixed is None else mixed + term
    x1 = _layer_norm(alpha * x + mixed, g1_ref[...], b1_ref[...])
    o_ref[:, :D_MODEL] = x1

    lg = _dot(x1.astype(BF16), rw_ref[...]) + rb_ref[...]
    col = [lg[:, e:e + 1] for e in range(N_EXPERTS)]
    gmax = [functools.reduce(jnp.maximum, col[g * EXPERTS_PER_GROUP:(g + 1) * EXPERTS_PER_GROUP])
            for g in range(N_GROUPS)]
    best = functools.reduce(jnp.maximum, gmax)
    gs = jnp.where(gmax[0] == best, 0, jnp.where(gmax[1] == best, 1, jnp.where(gmax[2] == best, 2, 3)))
    v = []
    for e in range(EXPERTS_PER_GROUP):
        v.append(jnp.where(gs == 0, col[e],
                           jnp.where(gs == 1, col[EXPERTS_PER_GROUP + e],
                                     jnp.where(gs == 2, col[2 * EXPERTS_PER_GROUP + e],
                                               col[3 * EXPERTS_PER_GROUP + e]))))
    m1 = best
    i1 = jnp.where(v[0] == m1, 0, jnp.where(v[1] == m1, 1, jnp.where(v[2] == m1, 2, 3)))
    v2 = [jnp.where(i1 == e, -jnp.inf, v[e]) for e in range(EXPERTS_PER_GROUP)]
    m2 = functools.reduce(jnp.maximum, v2)
    i2 = jnp.where(v2[0] == m2, 0, jnp.where(v2[1] == m2, 1, jnp.where(v2[2] == m2, 2, 3)))
    e2 = jnp.exp(m2 - m1)
    p1 = 1.0 / (1.0 + e2)
    p2 = e2 / (1.0 + e2)
    ea = jnp.minimum(i1, i2)
    eb = jnp.maximum(i1, i2)
    ga = jnp.where(i1 < i2, p1, p2)
    gb = jnp.where(i1 < i2, p2, p1)
    pair = jnp.where(ea == 0, 0, jnp.where(ea == 1, 3, 5)) + (eb - ea - 1)
    cls = (gs * PAIRS_PER_GROUP + pair).astype(F32)
    lane = lax.broadcasted_iota(jnp.int32, (x.shape[0], AUX_W), 1)
    o_ref[:, D_MODEL:] = jnp.where(lane == 0, ga, jnp.where(lane == 1, gb, jnp.where(lane == 2, cls, 0.0)))


def _merge(x2d, ya, yb, yc, wg, bg, wo, g1, b1, rw, rb, alpha):
    T = x2d.shape[0]
    tm = min(T, 256)
    row = lambda i: (i, 0)
    const = lambda i: (0, 0)
    return pl.pallas_call(
        functools.partial(_merge_kernel, alpha=alpha),
        grid=(T // tm,),
        in_specs=[pl.BlockSpec((tm, D_MODEL), row),
                  pl.BlockSpec((tm, A_WIDTH), row),
                  pl.BlockSpec((tm, B_WIDTH), row),
                  pl.BlockSpec((tm, C_WIDTH), row),
                  pl.BlockSpec((D_MODEL, N_BRANCH * D_MODEL), const),
                  pl.BlockSpec((1, N_BRANCH * D_MODEL), const),
                  pl.BlockSpec((N_BRANCH * A_WIDTH, D_MODEL), const),
                  pl.BlockSpec((1, D_MODEL), const),
                  pl.BlockSpec((1, D_MODEL), const),
                  pl.BlockSpec((D_MODEL, N_EXPERTS), const),
                  pl.BlockSpec((1, N_EXPERTS), const)],
        out_specs=pl.BlockSpec((tm, D_MODEL + AUX_W), row),
        out_shape=jax.ShapeDtypeStruct((T, D_MODEL + AUX_W), F32),
        compiler_params=_cparams("parallel"),
        name="merge",
    )(x2d, ya, yb, yc, wg, bg, wo, g1, b1, rw, rb)


def _moe_kernel(ea_ref, eb_ref, nv_ref, tok_ref, x_hbm, w1a, w3a, w2a, w1b, w3b, w2b, g2_ref, b2_ref,
                out_hbm, xbuf, obuf, gsem, ssem, *, alpha):
    i = pl.program_id(0)
    nv = nv_ref[i]

    @pl.when(i == 0)
    def _():
        xbuf[...] = jnp.zeros(xbuf.shape, F32)

    def row_in(r):
        return pltpu.make_async_copy(x_hbm.at[pl.ds(tok_ref[0, 0, r], 1), :], xbuf.at[pl.ds(r, 1), :], gsem)

    def row_out(r):
        return pltpu.make_async_copy(obuf.at[pl.ds(r, 1), :], out_hbm.at[pl.ds(tok_ref[0, 0, r], 1), :], ssem)

    @pl.when(nv > 0)
    def _():
        def start_in(r, c):
            row_in(r).start()
            return c

        def wait_in(r, c):
            row_in(r).wait()
            return c

        lax.fori_loop(0, nv, start_in, 0)
        lax.fori_loop(0, nv, wait_in, 0)

        xw = xbuf[...]
        x = xw[:, :D_MODEL]
        xb = x.astype(BF16)
        y = None
        for n, (w1, w3, w2) in enumerate(((w1a, w3a, w2a), (w1b, w3b, w2b))):
            hmid = _silu(_dot(xb, w1[0])) * _dot(xb, w3[0])
            ye = xw[:, D_MODEL + n:D_MODEL + n + 1] * _dot(hmid.astype(BF16), w2[0])
            y = ye if y is None else y + ye
        obuf[...] = _layer_norm(alpha * x + y, g2_ref[...], b2_ref[...])

        def start_out(r, c):
            row_out(r).start()
            return c

        def wait_out(r, c):
            row_out(r).wait()
            return c

        lax.fori_loop(0, nv, start_out, 0)
        lax.fori_loop(0, nv, wait_out, 0)


def _route(cls, T):
    n_blocks = T // MOE_ROWS + N_CLASSES
    onehot = (cls[:, None] == jnp.arange(N_CLASSES, dtype=jnp.int32)[None, :]).astype(jnp.int32)
    counts = jnp.sum(onehot, axis=0)
    rank = jnp.sum((jnp.cumsum(onehot, axis=0) - onehot) * onehot, axis=1)
    padded = (counts + MOE_ROWS - 1) // MOE_ROWS * MOE_ROWS
    pad_end = jnp.cumsum(padded)
    pad_start = pad_end - padded
    dest = pad_start[cls] + rank
    tok = jnp.zeros((n_blocks * MOE_ROWS,), jnp.int32).at[dest].set(jnp.arange(T, dtype=jnp.int32))
    blk_start = jnp.arange(n_blocks, dtype=jnp.int32) * MOE_ROWS
    blk_cls = jnp.minimum(jnp.searchsorted(pad_end, blk_start, side='right'), N_CLASSES - 1).astype(jnp.int32)
    nv = jnp.clip(pad_start[blk_cls] + counts[blk_cls] - blk_start, 0, MOE_ROWS).astype(jnp.int32)
    grp = blk_cls // PAIRS_PER_GROUP
    pair = blk_cls % PAIRS_PER_GROUP
    pa = jnp.asarray([0, 0, 0, 1, 1, 2], jnp.int32)[pair]
    pb = jnp.asarray([1, 2, 3, 2, 3, 3], jnp.int32)[pair]
    ea = grp * EXPERTS_PER_GROUP + pa
    eb = grp * EXPERTS_PER_GROUP + pb
    return ea, eb, nv, tok.reshape(n_blocks, 1, MOE_ROWS)


def _moe(x1w, w1, w3, w2, g2, b2, alpha):
    T = x1w.shape[0]
    cls = x1w[:, D_MODEL + 2].astype(jnp.int32)
    ea, eb, nv, tok = _route(cls, T)
    n_blocks = tok.shape[0]
    wa = lambda i, ea, eb, nv: (ea[i], 0, 0)
    wb = lambda i, ea, eb, nv: (eb[i], 0, 0)
    const = lambda i, ea, eb, nv: (0, 0)
    up = (1, D_MODEL, D_EXPERT)
    down = (1, D_EXPERT, D_MODEL)
    grid_spec = pltpu.PrefetchScalarGridSpec(
        num_scalar_prefetch=3,
        grid=(n_blocks,),
        in_specs=[pl.BlockSpec((1, 1, MOE_ROWS), lambda i, ea, eb, nv: (i, 0, 0), memory_space=pltpu.SMEM),
                  pl.BlockSpec(memory_space=pl.ANY),
                  pl.BlockSpec(up, wa), pl.BlockSpec(up, wa), pl.BlockSpec(down, wa),
                  pl.BlockSpec(up, wb), pl.BlockSpec(up, wb), pl.BlockSpec(down, wb),
                  pl.BlockSpec((1, D_MODEL), const), pl.BlockSpec((1, D_MODEL), const)],
        out_specs=pl.BlockSpec(memory_space=pl.ANY),
        scratch_shapes=[pltpu.VMEM((MOE_ROWS, D_MODEL + AUX_W), F32),
                        pltpu.VMEM((MOE_ROWS, D_MODEL), F32),
                        pltpu.SemaphoreType.DMA(()),
                        pltpu.SemaphoreType.DMA(())],
    )
    return pl.pallas_call(
        functools.partial(_moe_kernel, alpha=alpha),
        grid_spec=grid_spec,
        out_shape=jax.ShapeDtypeStruct((T, D_MODEL), F32),
        compiler_params=_cparams("arbitrary"),
        name="moe",
    )(ea, eb, nv, tok, x1w, w1, w3, w2, w1, w3, w2, g2, b2)


def _block_diag(w):
    n, d, _ = w.shape
    eye = jnp.eye(n, dtype=w.dtype)
    return jnp.einsum('nde,nm->ndme', w, eye).reshape(n * d, n * d)


def _layer(x, pos0, li, P, past):
    B, L, _ = x.shape
    T = B * L
    depth = P['depth']
    alpha = (2 * depth) ** 0.25
    w_in, b_in = P['w_in'][li], P['b_in'][li]

    def cols(lo, hi):
        return w_in[:, lo:hi], b_in[lo:hi].reshape(1, -1)

    lam_init = 0.8 - 0.6 * math.exp(-0.3 * li)
    dl = P['diff_lambda'][li]
    lam = jnp.exp(jnp.sum(dl[0] * dl[1])) - jnp.exp(jnp.sum(dl[2] * dl[3])) + lam_init
    lam_row = jnp.full((1, LANES), lam, F32)
    subln_g = P['diff_subln_g'][li].reshape(1, -1)

    wa, ba = cols(OFF_A, OFF_B)
    q, k, kb, v, vb = _attn_proj(x, wa, ba, pos0)
    if past is None:
        y_a = _flash_prompt(q, kb, vb, lam_row, subln_g, 1.0 - lam_init)
        conv_buf = jnp.zeros((B, CONV_W - 1, B_WIDTH), F32)
        h0 = jnp.zeros((B, 1, B_WIDTH), F32)
        s0 = jnp.zeros((B, C_HEADS, C_DK, C_DV), F32)
    else:
        k_past, v_past, conv_buf, h0, s0 = past
        P_len = k_past.shape[1]
        y_a = _decode_attention(q, kb, vb, k_past.reshape(B, P_len, A_QK_WIDTH),
                                v_past.reshape(B, P_len, A_WIDTH), lam_row, subln_g, 1.0 - lam_init)
        h0 = h0.reshape(B, 1, B_WIDTH)

    wb, bb = cols(OFF_B, OFF_C)
    y_b, conv_new, h_new = _rglru(x, wb, bb, P['conv_w'][li], P['conv_b'][li].reshape(1, -1),
                                  P['lru_gate_w'][li], P['lru_gate_b'][li], P['lru_sp'][li].reshape(1, -1),
                                  conv_buf, h0, pos0)
    wc, bc = cols(OFF_C, OFF_G)
    y_c, s_new = _hgrn(x, wc, bc, P['lower_bounds'][li].reshape(1, -1), P['hgrn_norm_g'][li].reshape(1, -1), s0)

    wg, bg = cols(OFF_G, N_IN)
    x1w = _merge(x.reshape(T, D_MODEL), y_a.reshape(T, A_WIDTH), y_b.reshape(T, B_WIDTH),
                 y_c.reshape(T, C_WIDTH), wg, bg, P['w_out'][li], P['ln1_g'][li].reshape(1, -1),
                 P['ln1_b'][li].reshape(1, -1), P['router_w'], P['router_b'].reshape(1, -1), alpha)
    x2 = _moe(x1w, P['moe_w1'][li], P['moe_w3'][li], P['moe_w2'][li], P['ln2_g'][li].reshape(1, -1),
              P['ln2_b'][li].reshape(1, -1), alpha)
    states = (k.reshape(B, L, A_HEADS, 2, A_HEAD_DIM), v.reshape(B, L, A_HEADS, A_V_DIM),
              conv_new, h_new.reshape(B, B_WIDTH), s_new)
    return x2.reshape(B, L, D_MODEL), states


def kernel(x_prompt, x_sample, cache_k, cache_v, state_conv, state_lru, state_hgrn, ln_in_g, ln_in_b, w_in, b_in, diff_lambda, diff_subln_g, conv_w, conv_b, lru_wa, lru_ba, lru_wx, lru_bx, lru_lambda, hgrn_lb, hgrn_norm_g, w_out, ln1_g, ln1_b, router_w, router_b, moe_w1, moe_w3, moe_w2, ln2_g, ln2_b):
    depth = w_in.shape[0]
    p_lb = jax.nn.softmax(hgrn_lb.astype(F32), axis=0)
    P = dict(
        depth=depth,
        w_in=w_in.astype(BF16), b_in=b_in,
        diff_lambda=diff_lambda.astype(F32), diff_subln_g=diff_subln_g,
        conv_w=conv_w, conv_b=conv_b,
        lru_gate_w=jnp.stack([jnp.concatenate([_block_diag(lru_wa[l]), _block_diag(lru_wx[l])], axis=1)
                              for l in range(depth)]).astype(BF16),
        lru_gate_b=jnp.concatenate([lru_ba, lru_bx], axis=1).reshape(depth, 1, -1),
        lru_sp=jax.nn.softplus(-lru_lambda.astype(F32)),
        lower_bounds=jnp.cumsum(p_lb, axis=0) - p_lb[0],
        hgrn_norm_g=hgrn_norm_g,
        w_out=w_out.astype(BF16), ln1_g=ln1_g, ln1_b=ln1_b,
        router_w=router_w.astype(BF16), router_b=router_b,
        moe_w1=moe_w1.astype(BF16), moe_w3=moe_w3.astype(BF16), moe_w2=moe_w2.astype(BF16),
        ln2_g=ln2_g, ln2_b=ln2_b,
    )
    Bp, Lp, _ = x_prompt.shape
    Bs, Ls, _ = x_sample.shape
    past_len = cache_k.shape[2]
    xp = _input_ln(x_prompt.reshape(Bp * Lp, D_MODEL), ln_in_g, ln_in_b).reshape(Bp, Lp, D_MODEL)
    xs = _input_ln(x_sample.reshape(Bs * Ls, D_MODEL), ln_in_g, ln_in_b).reshape(Bs, Ls, D_MODEL)
    prompt_states, sample_states = [], []
    for li in range(depth):
        xp, st_p = _layer(xp, 0, li, P, None)
        xs, st_s = _layer(xs, past_len, li, P,
                          (cache_k[li], cache_v[li], state_conv[li], state_lru[li], state_hgrn[li]))
        prompt_states.append(st_p)
        sample_states.append(st_s)
    k_p, v_p, conv_p, lru_p, hgrn_p = [jnp.stack(t) for t in zip(*prompt_states)]
    k_s, v_s, conv_s, lru_s, hgrn_s = [jnp.stack(t) for t in zip(*sample_states)]
    return (xp, xs, k_p, v_p, conv_p, lru_p, hgrn_p, k_s, v_s, conv_s, lru_s, hgrn_s)
```

```python
import functools
import math

import numpy as np
import jax
import jax.numpy as jnp
from jax import lax
from jax.experimental import pallas as pl
from jax.experimental.pallas import tpu as pltpu

F32 = jnp.float32
BF16 = jnp.bfloat16

D_MODEL = 1024
CHUNK = 64
A_HEADS = 4
A_HEAD_DIM = 64
A_V_DIM = 2 * A_HEAD_DIM
A_QK_WIDTH = A_HEADS * 2 * A_HEAD_DIM
A_WIDTH = A_HEADS * A_V_DIM
ROPE_THETA = 10000.0
NEG_INF = -1e30
Q_SCALE = A_HEAD_DIM ** -0.5 * math.log2(math.e)
B_WIDTH = 512
B_BLOCKS = 8
CONV_W = 4
LRU_C = 8.0
C_HEADS = 4
C_DK = 128
C_DV = 128
C_WIDTH = C_HEADS * C_DV
F_TINY = 1e-30
N_BRANCH = 3
N_EXPERTS = 16
N_GROUPS = 4
EXPERTS_PER_GROUP = N_EXPERTS // N_GROUPS
D_EXPERT = 512
NORM_EPS = 1e-5

LANES = 128
PAIRS_PER_GROUP = 6
N_CLASSES = N_GROUPS * PAIRS_PER_GROUP
MOE_ROWS = 256
AUX_W = LANES
VMEM_LIMIT = 56 * 1024 * 1024

OFF_A = 0
OFF_B = OFF_A + 2 * A_QK_WIDTH + A_WIDTH
OFF_C = OFF_B + 2 * B_WIDTH
OFF_G = OFF_C + 4 * C_WIDTH
N_IN = OFF_G + N_BRANCH * D_MODEL


def _cparams(*sem):
    return pltpu.CompilerParams(dimension_semantics=sem, vmem_limit_bytes=VMEM_LIMIT)


def _dot(a, b):
    return jnp.dot(a, b, preferred_element_type=F32)


def _dot_nt(a, b):
    return lax.dot_general(a, b, (((1,), (1,)), ((), ())), preferred_element_type=F32)


def _dot_tn(a, b):
    return lax.dot_general(a, b, (((0,), (0,)), ((), ())), preferred_element_type=F32)


def _sigmoid(x):
    return 1.0 / (1.0 + jnp.exp(-x))


def _silu(x):
    return x * _sigmoid(x)


def _layer_norm(x, g, b):
    mu = jnp.mean(x, axis=-1, keepdims=True)
    xc = x - mu
    var = jnp.mean(xc * xc, axis=-1, keepdims=True)
    return xc * lax.rsqrt(var + NORM_EPS) * g + b


def _ln_kernel(x_ref, g_ref, b_ref, o_ref):
    o_ref[...] = _layer_norm(x_ref[...], g_ref[...], b_ref[...])


def _input_ln(x2d, g, b):
    T = x2d.shape[0]
    tm = min(T, 512)
    return pl.pallas_call(
        _ln_kernel,
        grid=(T // tm,),
        in_specs=[pl.BlockSpec((tm, D_MODEL), lambda i: (i, 0)),
                  pl.BlockSpec((1, D_MODEL), lambda i: (0, 0)),
                  pl.BlockSpec((1, D_MODEL), lambda i: (0, 0))],
        out_specs=pl.BlockSpec((tm, D_MODEL), lambda i: (i, 0)),
        out_shape=jax.ShapeDtypeStruct((T, D_MODEL), F32),
        compiler_params=_cparams("parallel"),
        name="input_ln",
    )(x2d, g.reshape(1, -1), b.reshape(1, -1))


def _attn_proj_kernel(x_ref, w_ref, b_ref, cos_ref, sin_ref, q_ref, k_ref, kb_ref, v_ref, vb_ref, vt_ref):
    x = x_ref[0].astype(BF16)
    y = _dot(x, w_ref[...]) + b_ref[...]
    cos = cos_ref[...]
    sin = sin_ref[...]
    lane = lax.broadcasted_iota(jnp.int32, cos.shape, 1)
    first_half = (lane % A_HEAD_DIM) < (A_HEAD_DIM // 2)
    for s in range(2 * A_QK_WIDTH // LANES):
        blk = y[:, s * LANES:(s + 1) * LANES]
        swapped = jnp.where(first_half,
                            pltpu.roll(blk, LANES - A_HEAD_DIM // 2, 1),
                            pltpu.roll(blk, A_HEAD_DIM // 2, 1))
        rot = blk * cos + swapped * sin
        if s < A_QK_WIDTH // LANES:
            q_ref[0, :, s * LANES:(s + 1) * LANES] = (rot * Q_SCALE).astype(BF16)
        else:
            o = s * LANES - A_QK_WIDTH
            k_ref[0, :, o:o + LANES] = rot
            kb_ref[0, :, o:o + LANES] = rot.astype(BF16)
    v = y[:, 2 * A_QK_WIDTH:]
    v_ref[0] = v
    vb_ref[0] = v.astype(BF16)
    vt_ref[0] = v.T.astype(BF16)


def _rope_tables(pos0, L):
    half = A_HEAD_DIM // 2
    inv_freq = ROPE_THETA ** (-jnp.arange(half, dtype=F32) / half)
    pos = (pos0 + jnp.arange(L)).astype(F32)
    ang = pos[:, None] * inv_freq[None, :]
    cos = jnp.cos(ang)
    sin = jnp.sin(ang)
    reps = LANES // A_HEAD_DIM
    cos_t = jnp.tile(jnp.concatenate([cos, cos], axis=1), (1, reps))
    sin_t = jnp.tile(jnp.concatenate([-sin, sin], axis=1), (1, reps))
    return cos_t, sin_t


def _attn_proj(x, w, b, pos0):
    B, L, _ = x.shape
    tl = min(L, 512)
    cos_t, sin_t = _rope_tables(pos0, L)
    wa = A_QK_WIDTH
    n = 2 * A_QK_WIDTH + A_WIDTH
    row = lambda bi, j: (bi, j, 0)
    return pl.pallas_call(
        _attn_proj_kernel,
        grid=(B, L // tl),
        in_specs=[pl.BlockSpec((1, tl, D_MODEL), row),
                  pl.BlockSpec((D_MODEL, n), lambda bi, j: (0, 0)),
                  pl.BlockSpec((1, n), lambda bi, j: (0, 0)),
                  pl.BlockSpec((tl, LANES), lambda bi, j: (j, 0)),
                  pl.BlockSpec((tl, LANES), lambda bi, j: (j, 0))],
        out_specs=[pl.BlockSpec((1, tl, wa), row)] * 5 + [pl.BlockSpec((1, wa, tl), lambda bi, j: (bi, 0, j))],
        out_shape=[jax.ShapeDtypeStruct((B, L, wa), BF16),
                   jax.ShapeDtypeStruct((B, L, wa), F32),
                   jax.ShapeDtypeStruct((B, L, wa), BF16),
                   jax.ShapeDtypeStruct((B, L, wa), F32),
                   jax.ShapeDtypeStruct((B, L, wa), BF16),
                   jax.ShapeDtypeStruct((B, wa, L), BF16)],
        compiler_params=_cparams("parallel", "parallel"),
        name="attn_proj",
    )(x, w, b, cos_t, sin_t)


def _softmax_update(c, qz, kb, vb, mask, m_s, l_s, acc_s):
    s = _dot_nt(qz, kb)
    if mask is not None:
        s = jnp.where(mask, s, NEG_INF)
    m_old = m_s[c]
    m_new = jnp.maximum(m_old, jnp.max(s, axis=-1, keepdims=True))
    alpha = jnp.exp2(m_old - m_new)
    p = jnp.exp2(s - m_new)
    l_s[c] = alpha * l_s[c] + jnp.sum(p, axis=-1, keepdims=True)
    acc_s[c] = alpha * acc_s[c] + _dot(p.astype(BF16), vb)
    m_s[c] = m_new


def _diff_finish(c0, lam, g, out_scale, l_s, acc_s):
    o = acc_s[c0] / l_s[c0] - lam * (acc_s[c0 + 1] / l_s[c0 + 1])
    ms = jnp.mean(o * o, axis=-1, keepdims=True)
    return o * lax.rsqrt(ms + NORM_EPS) * g * out_scale


def _split_maps(q):
    lane = lax.broadcasted_iota(jnp.int32, q.shape, 1)
    zero = jnp.zeros_like(q)
    return jnp.where(lane < A_HEAD_DIM, q, zero), jnp.where(lane >= A_HEAD_DIM, q, zero)


def _flash_kernel(lam_ref, g_ref, q_ref, k_ref, vt_ref, o_ref, m_s, l_s, acc_s, *, tq, out_scale):
    i = pl.program_id(2)
    qz = _split_maps(q_ref[0])
    qq = jnp.concatenate(qz, axis=0)
    m_s[...] = jnp.full(m_s.shape, NEG_INF, F32)
    l_s[...] = jnp.zeros(l_s.shape, F32)
    acc_s[...] = jnp.zeros(acc_s.shape, F32)

    def update(off, mask):
        kb = k_ref[0, pl.ds(off, tq), :]
        vt = vt_ref[0, :, pl.ds(off, tq)]
        s = _dot_nt(kb, qq)
        if mask is not None:
            s = jnp.where(mask, s, NEG_INF)
        m_old = m_s[...]
        m_new = jnp.maximum(m_old, jnp.max(s, axis=0, keepdims=True))
        alpha = jnp.exp2(m_old - m_new)
        p = jnp.exp2(s - m_new)
        l_s[...] = alpha * l_s[...] + jnp.sum(p, axis=0, keepdims=True)
        acc_s[...] = alpha * acc_s[...] + _dot(vt, p.astype(BF16))
        m_s[...] = m_new

    def body(j, carry):
        update(pl.multiple_of(j * tq, tq), None)
        return carry

    lax.fori_loop(0, i, body, 0)
    key = lax.broadcasted_iota(jnp.int32, (tq, 2 * tq), 0) // CHUNK
    qry = (lax.broadcasted_iota(jnp.int32, (tq, 2 * tq), 1) % tq) // CHUNK
    update(pl.multiple_of(i * tq, tq), key <= qry)

    inv_l = 1.0 / l_s[...]
    o = acc_s[...] * inv_l
    o = o[:, :tq] - lam_ref[0:1, 0:1] * o[:, tq:]
    ms = jnp.mean(o * o, axis=0, keepdims=True)
    y = o * lax.rsqrt(ms + NORM_EPS) * (g_ref[...] * out_scale)
    o_ref[0] = y.T.astype(o_ref.dtype)


def _flash_prompt(q, kb, vt, lam_row, g_col, out_scale):
    B, L, _ = q.shape
    tq = min(L, 512)
    return pl.pallas_call(
        functools.partial(_flash_kernel, tq=tq, out_scale=out_scale),
        grid=(B, A_HEADS, L // tq),
        in_specs=[pl.BlockSpec((1, LANES), lambda b, h, i: (0, 0)),
                  pl.BlockSpec((A_V_DIM, 1), lambda b, h, i: (0, 0)),
                  pl.BlockSpec((1, tq, LANES), lambda b, h, i: (b, i, h)),
                  pl.BlockSpec((1, L, LANES), lambda b, h, i: (b, 0, h)),
                  pl.BlockSpec((1, A_V_DIM, L), lambda b, h, i: (b, h, 0))],
        out_specs=pl.BlockSpec((1, tq, A_V_DIM), lambda b, h, i: (b, i, h)),
        out_shape=jax.ShapeDtypeStruct((B, L, A_WIDTH), BF16),
        scratch_shapes=[pltpu.VMEM((1, 2 * tq), F32), pltpu.VMEM((1, 2 * tq), F32),
                        pltpu.VMEM((A_V_DIM, 2 * tq), F32)],
        compiler_params=_cparams("parallel", "parallel", "arbitrary"),
        name="flash_prompt",
    )(lam_row, g_col, q, kb, vt)


def _decode_kernel(lam_ref, g_ref, q_ref, kn_ref, vn_ref, kp_ref, vp_ref, o_ref, m_s, l_s, acc_s, *, out_scale):
    j = pl.program_id(1)

    @pl.when(j == 0)
    def _():
        m_s[...] = jnp.full(m_s.shape, NEG_INF, F32)
        l_s[...] = jnp.zeros(l_s.shape, F32)
        acc_s[...] = jnp.zeros(acc_s.shape, F32)

    def attend(k_full, v_full):
        for h in range(A_HEADS):
            hs = slice(h * LANES, (h + 1) * LANES)
            qz = _split_maps(q_ref[0, :, hs])
            kb = k_full[:, hs].astype(BF16)
            vb = v_full[:, hs].astype(BF16)
            for c in range(2):
                _softmax_update(2 * h + c, qz[c], kb, vb, None, m_s, l_s, acc_s)

    attend(kp_ref[0], vp_ref[0])

    @pl.when(j == pl.num_programs(1) - 1)
    def _():
        attend(kn_ref[0], vn_ref[0])
        for h in range(A_HEADS):
            y = _diff_finish(2 * h, lam_ref[...], g_ref[...], out_scale, l_s, acc_s)
            o_ref[0, :, h * A_V_DIM:(h + 1) * A_V_DIM] = y.astype(o_ref.dtype)


def _decode_attention(q, k_new, v_new, k_past, v_past, lam_row, g_row, out_scale):
    B, L, _ = q.shape
    P = k_past.shape[1]
    tk = min(P, 512)
    new = lambda b, j: (b, 0, 0)
    past = lambda b, j: (b, j, 0)
    return pl.pallas_call(
        functools.partial(_decode_kernel, out_scale=out_scale),
        grid=(B, P // tk),
        in_specs=[pl.BlockSpec((1, LANES), lambda b, j: (0, 0)),
                  pl.BlockSpec((1, A_V_DIM), lambda b, j: (0, 0)),
                  pl.BlockSpec((1, L, A_QK_WIDTH), new),
                  pl.BlockSpec((1, L, A_QK_WIDTH), new),
                  pl.BlockSpec((1, L, A_WIDTH), new),
                  pl.BlockSpec((1, tk, A_QK_WIDTH), past),
                  pl.BlockSpec((1, tk, A_WIDTH), past)],
        out_specs=pl.BlockSpec((1, L, A_WIDTH), new),
        out_shape=jax.ShapeDtypeStruct((B, L, A_WIDTH), BF16),
        scratch_shapes=[pltpu.VMEM((2 * A_HEADS, L, 1), F32), pltpu.VMEM((2 * A_HEADS, L, 1), F32),
                        pltpu.VMEM((2 * A_HEADS, L, A_V_DIM), F32)],
        compiler_params=_cparams("parallel", "arbitrary"),
        name="decode_attention",
    )(lam_row, g_row, q, k_new, v_new, k_past, v_past)


def _gelu_tanh(x):
    return 0.5 * x * (1.0 + jnp.tanh(math.sqrt(2.0 / math.pi) * (x + 0.044715 * (x * x * x))))


def _rglru_kernel(x_ref, w_ref, b_ref, cw_ref, cb_ref, gw_ref, gbias_ref, sp_ref, cbuf_ref, h0_ref,
                  y_ref, conv_ref, h_ref, xs, hc, *, tl, pos0):
    j = pl.program_id(1)
    tail = CONV_W - 1
    base = 8

    @pl.when(j == 0)
    def _():
        xs[0:base, :] = jnp.zeros((base, B_WIDTH), F32)
        xs[base - tail:base, :] = cbuf_ref[0]
        hc[...] = h0_ref[0]

    x = x_ref[0].astype(BF16)
    y = _dot(x, w_ref[...]) + b_ref[...]
    xb = y[:, :B_WIDTH]
    gb = y[:, B_WIDTH:]
    xs[base:base + tl, :] = xb
    xc = cb_ref[...] + cw_ref[0:1, :] * xs[base - tail:base - tail + tl, :]
    for t in range(1, CONV_W):
        xc = xc + cw_ref[t:t + 1, :] * xs[base - tail + t:base - tail + t + tl, :]
    new_tail = xs[base + tl - tail:base + tl, :]
    xs[base - tail:base, :] = new_tail
    conv_ref[0] = new_tail

    gates = _dot(xc.astype(BF16), gw_ref[...]) + gbias_ref[...]
    r = _sigmoid(gates[:, :B_WIDTH])
    ig = _sigmoid(gates[:, B_WIDTH:])
    log_a = -LRU_C * r * sp_ref[...]
    a = jnp.exp(log_a)
    t = jnp.tanh(log_a)
    mult = jnp.sqrt(-2.0 * t / (1.0 - t))
    row = lax.broadcasted_iota(jnp.int32, (tl, B_WIDTH), 0)
    mult = jnp.where(row + (pos0 + j * tl) == 0, 1.0, mult)
    bb = xc * ig * mult
    d = 1
    while d < tl:
        keep = row >= d
        a_sh = pltpu.roll(a, d, 0)
        b_sh = pltpu.roll(bb, d, 0)
        bb = jnp.where(keep, a * b_sh + bb, bb)
        a = jnp.where(keep, a * a_sh, a)
        d *= 2
    h = a * hc[...] + bb
    h_last = h[tl - 1:tl, :]
    hc[...] = h_last
    h_ref[0] = h_last
    y_ref[0] = (_gelu_tanh(gb) * h).astype(y_ref.dtype)


def _rglru(x, w, b, conv_w, conv_b, gate_w, gate_b, sp, conv_buf, h0, pos0):
    B, L, _ = x.shape
    tl = min(L, 256)
    tail = CONV_W - 1
    const = lambda bi, j: (0, 0)
    per_b = lambda bi, j: (bi, 0, 0)
    return pl.pallas_call(
        functools.partial(_rglru_kernel, tl=tl, pos0=pos0),
        grid=(B, L // tl),
        in_specs=[pl.BlockSpec((1, tl, D_MODEL), lambda bi, j: (bi, j, 0)),
                  pl.BlockSpec((D_MODEL, 2 * B_WIDTH), const),
                  pl.BlockSpec((1, 2 * B_WIDTH), const),
                  pl.BlockSpec((CONV_W, B_WIDTH), const),
                  pl.BlockSpec((1, B_WIDTH), const),
                  pl.BlockSpec((B_WIDTH, 2 * B_WIDTH), const),
                  pl.BlockSpec((1, 2 * B_WIDTH), const),
                  pl.BlockSpec((1, B_WIDTH), const),
                  pl.BlockSpec((1, tail, B_WIDTH), per_b),
                  pl.BlockSpec((1, 1, B_WIDTH), per_b)],
        out_specs=[pl.BlockSpec((1, tl, B_WIDTH), lambda bi, j: (bi, j, 0)),
                   pl.BlockSpec((1, tail, B_WIDTH), per_b),
                   pl.BlockSpec((1, 1, B_WIDTH), per_b)],
        out_shape=[jax.ShapeDtypeStruct((B, L, B_WIDTH), BF16),
                   jax.ShapeDtypeStruct((B, tail, B_WIDTH), F32),
                   jax.ShapeDtypeStruct((B, 1, B_WIDTH), F32)],
        scratch_shapes=[pltpu.VMEM((8 + tl, B_WIDTH), F32), pltpu.VMEM((1, B_WIDTH), F32)],
        compiler_params=_cparams("parallel", "arbitrary"),
        name="rglru",
    )(x, w, b, conv_w, conv_b, gate_w, gate_b, sp, conv_buf, h0)


def _hgrn_levels(C):
    return [C >> i for i in range(int(math.log2(C)) + 1)]


HGRN_SPLIT = 3


def _hgrn_sum_matrices(C):
    t = np.arange(C)

    def prefix(m):
        return (t[None, :] >= ((t // m) * m)[:, None]) & (t[None, :] <= t[:, None])

    def suffix(m):
        return (t[None, :] > t[:, None]) & (t[None, :] <= ((t // m) * m + m - 1)[:, None])

    mats = [prefix(C), suffix(C)]
    for m in _hgrn_levels(C)[1:-1]:
        upper = ((t // m) % 2 == 1)[:, None]
        mats.append(np.where(upper, prefix(m), suffix(m)))
    one = np.concatenate(mats, axis=0).astype(np.float32)
    return np.concatenate([one] * HGRN_SPLIT, axis=1)


def _hgrn_kernel(x_ref, w_ref, b_ref, lb_ref, ng_ref, dm_ref, s0_ref, y_ref, s_ref,
                 q_s, k_s, g_s, v_s, gate_s, o_s, st_s, *, tl, C):
    j = pl.program_id(1)
    levels = _hgrn_levels(C)
    nlev = len(levels)

    @pl.when(j == 0)
    def _():
        for h in range(C_HEADS):
            st_s[h] = s0_ref[0, h].T

    x = x_ref[0].astype(BF16)
    y = _dot(x, w_ref[...]) + b_ref[...]
    lb = lb_ref[...]
    z = y[:, C_WIDTH:2 * C_WIDTH]
    f = lb + (1.0 - lb) * _sigmoid(z)
    q_s[...] = _silu(y[:, :C_WIDTH])
    k_s[...] = (1.0 - lb) * _sigmoid(-z)
    g_s[...] = jnp.log(jnp.maximum(f, F_TINY))
    v_s[...] = y[:, 2 * C_WIDTH:3 * C_WIDTH]
    gate_s[...] = _silu(y[:, 3 * C_WIDTH:])

    ti = lax.broadcasted_iota(jnp.int32, (C, C), 0)
    si = lax.broadcasted_iota(jnp.int32, (C, C), 1)
    masks = [ti == si]
    for m in levels[1:]:
        masks.append((ti // (2 * m) == si // (2 * m)) & ((ti // m) % 2 == 1) & ((si // m) % 2 == 0))
    odd_row = lax.broadcasted_iota(jnp.int32, (C, C_WIDTH), 0) % 2 == 1

    def chunk(ci, carry):
        r0 = pl.multiple_of(ci * C, C)
        rows = pl.ds(r0, C)
        g = g_s[rows, :]
        pieces = []
        rest = g
        for _ in range(HGRN_SPLIT):
            piece = rest.astype(BF16)
            pieces.append(piece)
            rest = rest - piece.astype(F32)
        sums = _dot(dm_ref[...], jnp.concatenate(pieces, axis=0))
        e = jnp.exp(jnp.minimum(sums, 0.0))
        e_one = jnp.exp(jnp.where(odd_row, g, 0.0))
        for h in range(C_HEADS):
            hs = slice(h * C_DK, (h + 1) * C_DK)
            q = q_s[rows, hs]
            k = k_s[rows, hs]
            vb = v_s[rows, hs].astype(BF16)
            scores = jnp.where(masks[0], _dot_nt(q.astype(BF16), k.astype(BF16)), 0.0)
            for li in range(1, nlev):
                el = e[(li + 1) * C:(li + 2) * C, hs] if li < nlev - 1 else e_one[:, hs]
                scores = scores + jnp.where(masks[li], _dot_nt((q * el).astype(BF16), (k * el).astype(BF16)), 0.0)
            e_cum = e[0:C, hs]
            qc = (q * e_cum).astype(BF16)
            kc = (k * e[C:2 * C, hs]).astype(BF16)
            st = st_s[h]
            o = _dot(scores.astype(BF16), vb) + _dot_nt(qc, st.astype(BF16))
            st_s[h] = st * e_cum[C - 1:C, :] + _dot_tn(vb, kc)
            ms = jnp.mean(o * o, axis=-1, keepdims=True)
            o_s[rows, hs] = o * lax.rsqrt(ms + NORM_EPS) * ng_ref[...] * gate_s[rows, hs]
        return carry

    lax.fori_loop(0, tl // C, chunk, 0)
    y_ref[0] = o_s[...].astype(y_ref.dtype)
    for h in range(C_HEADS):
        s_ref[0, h] = st_s[h].T


def _hgrn(x, w, b, lb, norm_g, s0):
    B, L, _ = x.shape
    C = min(CHUNK, L)
    tl = min(L, 256)
    dm = jnp.asarray(_hgrn_sum_matrices(C), BF16)
    const = lambda bi, j: (0, 0)
    per_b = lambda bi, j: (bi, 0, 0, 0)
    wide = pltpu.VMEM((tl, C_WIDTH), F32)
    return pl.pallas_call(
        functools.partial(_hgrn_kernel, tl=tl, C=C),
        grid=(B, L // tl),
        in_specs=[pl.BlockSpec((1, tl, D_MODEL), lambda bi, j: (bi, j, 0)),
                  pl.BlockSpec((D_MODEL, 4 * C_WIDTH), const),
                  pl.BlockSpec((1, 4 * C_WIDTH), const),
                  pl.BlockSpec((1, C_WIDTH), const),
                  pl.BlockSpec((1, C_DV), const),
                  pl.BlockSpec(dm.shape, const),
                  pl.BlockSpec((1, C_HEADS, C_DK, C_DV), per_b)],
        out_specs=[pl.BlockSpec((1, tl, C_WIDTH), lambda bi, j: (bi, j, 0)),
                   pl.BlockSpec((1, C_HEADS, C_DK, C_DV), per_b)],
        out_shape=[jax.ShapeDtypeStruct((B, L, C_WIDTH), BF16),
                   jax.ShapeDtypeStruct((B, C_HEADS, C_DK, C_DV), F32)],
        scratch_shapes=[wide, wide, wide, wide, wide, wide, pltpu.VMEM((C_HEADS, C_DV, C_DK), F32)],
        compiler_params=_cparams("parallel", "arbitrary"),
        name="hgrn",
    )(x, w, b, lb, norm_g, dm, s0)


def _merge_kernel(x_ref, ya_ref, yb_ref, yc_ref, wg_ref, bg_ref, wo_ref, g1_ref, b1_ref, rw_ref, rb_ref,
                  o_ref, cls_ref, *, alpha):
    x = x_ref[...]
    xb = x.astype(BF16)
    mixed = None
    for n, y_ref in enumerate((ya_ref, yb_ref, yc_ref)):
        cols = slice(n * D_MODEL, (n + 1) * D_MODEL)
        gate = _sigmoid(_dot(xb, wg_ref[:, cols]) + bg_ref[:, cols])
        rows = slice(n * A_WIDTH, (n + 1) * A_WIDTH)
        term = gate * _dot(y_ref[...], wo_ref[rows, :])
        mixed = term if mixed is None else mixed + term
    x1 = _layer_norm(alpha * x + mixed, g1_ref[...], b1_ref[...])
    o_ref[:, :D_MODEL] = x1

    lg = _dot(x1.astype(BF16), rw_ref[...]) + rb_ref[...]
    col = [lg[:, e:e + 1] for e in range(N_EXPERTS)]
    gmax = [functools.reduce(jnp.maximum, col[g * EXPERTS_PER_GROUP:(g + 1) * EXPERTS_PER_GROUP])
            for g in range(N_GROUPS)]
    best = functools.reduce(jnp.maximum, gmax)
    gs = jnp.where(gmax[0] == best, 0, jnp.where(gmax[1] == best, 1, jnp.where(gmax[2] == best, 2, 3)))
    v = []
    for e in range(EXPERTS_PER_GROUP):
        v.append(jnp.where(gs == 0, col[e],
                           jnp.where(gs == 1, col[EXPERTS_PER_GROUP + e],
                                     jnp.where(gs == 2, col[2 * EXPERTS_PER_GROUP + e],
                                               col[3 * EXPERTS_PER_GROUP + e]))))
    m1 = best
    i1 = jnp.where(v[0] == m1, 0, jnp.where(v[1] == m1, 1, jnp.where(v[2] == m1, 2, 3)))
    v2 = [jnp.where(i1 == e, -jnp.inf, v[e]) for e in range(EXPERTS_PER_GROUP)]
    m2 = functools.reduce(jnp.maximum, v2)
    i2 = jnp.where(v2[0] == m2, 0, jnp.where(v2[1] == m2, 1, jnp.where(v2[2] == m2, 2, 3)))
    e2 = jnp.exp(m2 - m1)
    p1 = 1.0 / (1.0 + e2)
    p2 = e2 / (1.0 + e2)
    ea = jnp.minimum(i1, i2)
    eb = jnp.maximum(i1, i2)
    ga = jnp.where(i1 < i2, p1, p2)
    gb = jnp.where(i1 < i2, p2, p1)
    pair = jnp.where(ea == 0, 0, jnp.where(ea == 1, 3, 5)) + (eb - ea - 1)
    cls = (gs * PAIRS_PER_GROUP + pair).astype(F32)
    lane = lax.broadcasted_iota(jnp.int32, (x.shape[0], AUX_W), 1)
    aux = jnp.where(lane == 0, ga, jnp.where(lane == 1, gb, jnp.where(lane == 2, cls, 0.0)))
    o_ref[:, D_MODEL:] = aux
    pick = (lax.broadcasted_iota(jnp.int32, (8, AUX_W), 0) == 0) & (lax.broadcasted_iota(jnp.int32, (8, AUX_W), 1) == 2)
    cls_ref[...] = _dot_nt(jnp.where(pick, 1.0, 0.0).astype(BF16), aux.astype(BF16))


def _merge(x2d, ya, yb, yc, wg, bg, wo, g1, b1, rw, rb, alpha):
    T = x2d.shape[0]
    tm = min(T, 256)
    row = lambda i: (i, 0)
    const = lambda i: (0, 0)
    return pl.pallas_call(
        functools.partial(_merge_kernel, alpha=alpha),
        grid=(T // tm,),
        in_specs=[pl.BlockSpec((tm, D_MODEL), row),
                  pl.BlockSpec((tm, A_WIDTH), row),
                  pl.BlockSpec((tm, B_WIDTH), row),
                  pl.BlockSpec((tm, C_WIDTH), row),
                  pl.BlockSpec((D_MODEL, N_BRANCH * D_MODEL), const),
                  pl.BlockSpec((1, N_BRANCH * D_MODEL), const),
                  pl.BlockSpec((N_BRANCH * A_WIDTH, D_MODEL), const),
                  pl.BlockSpec((1, D_MODEL), const),
                  pl.BlockSpec((1, D_MODEL), const),
                  pl.BlockSpec((D_MODEL, N_EXPERTS), const),
                  pl.BlockSpec((1, N_EXPERTS), const)],
        out_specs=[pl.BlockSpec((tm, D_MODEL + AUX_W), row), pl.BlockSpec((8, tm), lambda i: (0, i))],
        out_shape=[jax.ShapeDtypeStruct((T, D_MODEL + AUX_W), F32), jax.ShapeDtypeStruct((8, T), F32)],
        compiler_params=_cparams("parallel"),
        name="merge",
    )(x2d, ya, yb, yc, wg, bg, wo, g1, b1, rw, rb)


def _dispatch_kernel(dest_ref, x_ref, out_hbm, zbuf, sem, *, tm, n_tok_steps):
    i = pl.program_id(0)

    def scatter_rows(src):
        def start(r, c):
            pltpu.make_async_copy(src.at[pl.ds(r, 1), :], out_hbm.at[pl.ds(dest_ref[0, 0, r], 1), :], sem).start()
            return c

        lax.fori_loop(0, tm, start, 0, unroll=8)
        pltpu.make_async_copy(src, out_hbm.at[pl.ds(0, tm), :], sem).wait()

    @pl.when(i < n_tok_steps)
    def _():
        scatter_rows(x_ref)

    @pl.when(i >= n_tok_steps)
    def _():
        @pl.when(i == n_tok_steps)
        def _():
            zbuf[...] = jnp.zeros(zbuf.shape, F32)

        scatter_rows(zbuf)


def _dispatch(x1w, dest_all, n_rows):
    T, W = x1w.shape
    tm = min(T, 512)
    n_tok_steps = T // tm
    steps = dest_all.shape[0] // tm
    return pl.pallas_call(
        functools.partial(_dispatch_kernel, tm=tm, n_tok_steps=n_tok_steps),
        grid=(steps,),
        in_specs=[pl.BlockSpec((1, 1, tm), lambda i: (i, 0, 0), memory_space=pltpu.SMEM),
                  pl.BlockSpec((tm, W), lambda i: (jnp.minimum(i, n_tok_steps - 1), 0))],
        out_specs=pl.BlockSpec(memory_space=pl.ANY),
        out_shape=jax.ShapeDtypeStruct((n_rows, W), F32),
        scratch_shapes=[pltpu.VMEM((tm, W), F32), pltpu.SemaphoreType.DMA(())],
        compiler_params=_cparams("arbitrary"),
        name="moe_dispatch",
    )(dest_all.reshape(steps, 1, tm), x1w)


def _experts_kernel(ea_ref, eb_ref, nv_ref, x_ref, w1a, w3a, w2a, w1b, w3b, w2b, g2_ref, b2_ref, o_ref, *, alpha):
    i = pl.program_id(0)

    @pl.when(nv_ref[i] > 0)
    def _():
        xw = x_ref[...]
        x = xw[:, :D_MODEL]
        xb = x.astype(BF16)
        y = None
        for n, (w1, w3, w2) in enumerate(((w1a, w3a, w2a), (w1b, w3b, w2b))):
            hmid = _silu(_dot(xb, w1[0])) * _dot(xb, w3[0])
            ye = xw[:, D_MODEL + n:D_MODEL + n + 1] * _dot(hmid.astype(BF16), w2[0])
            y = ye if y is None else y + ye
        o_ref[...] = _layer_norm(alpha * x + y, g2_ref[...], b2_ref[...])

    @pl.when(nv_ref[i] == 0)
    def _():
        o_ref[...] = jnp.zeros(o_ref.shape, F32)


def _experts(xs, ea, eb, nv, w1, w3, w2, g2, b2, alpha):
    n_rows, W = xs.shape
    wa = lambda i, ea, eb, nv: (ea[i], 0, 0)
    wb = lambda i, ea, eb, nv: (eb[i], 0, 0)
    const = lambda i, ea, eb, nv: (0, 0)
    row = lambda i, ea, eb, nv: (i, 0)
    up = (1, D_MODEL, D_EXPERT)
    down = (1, D_EXPERT, D_MODEL)
    grid_spec = pltpu.PrefetchScalarGridSpec(
        num_scalar_prefetch=3,
        grid=(n_rows // MOE_ROWS,),
        in_specs=[pl.BlockSpec((MOE_ROWS, W), row),
                  pl.BlockSpec(up, wa), pl.BlockSpec(up, wa), pl.BlockSpec(down, wa),
                  pl.BlockSpec(up, wb), pl.BlockSpec(up, wb), pl.BlockSpec(down, wb),
                  pl.BlockSpec((1, D_MODEL), const), pl.BlockSpec((1, D_MODEL), const)],
        out_specs=pl.BlockSpec((MOE_ROWS, D_MODEL), row),
    )
    return pl.pallas_call(
        functools.partial(_experts_kernel, alpha=alpha),
        grid_spec=grid_spec,
        out_shape=jax.ShapeDtypeStruct((n_rows, D_MODEL), F32),
        compiler_params=_cparams("arbitrary"),
        name="moe_experts",
    )(ea, eb, nv, xs, w1, w3, w2, w1, w3, w2, g2, b2)


def _combine_kernel(dest_ref, y_hbm, o_ref, sem, *, tm):
    def start(r, c):
        pltpu.make_async_copy(y_hbm.at[pl.ds(dest_ref[0, 0, r], 1), :], o_ref.at[pl.ds(r, 1), :], sem).start()
        return c

    lax.fori_loop(0, tm, start, 0, unroll=8)
    pltpu.make_async_copy(y_hbm.at[pl.ds(0, tm), :], o_ref, sem).wait()


def _combine(ys, dest, T):
    tm = min(T, 512)
    return pl.pallas_call(
        functools.partial(_combine_kernel, tm=tm),
        grid=(T // tm,),
        in_specs=[pl.BlockSpec((1, 1, tm), lambda i: (i, 0, 0), memory_space=pltpu.SMEM),
                  pl.BlockSpec(memory_space=pl.ANY)],
        out_specs=pl.BlockSpec((tm, D_MODEL), lambda i: (i, 0)),
        out_shape=jax.ShapeDtypeStruct((T, D_MODEL), F32),
        scratch_shapes=[pltpu.SemaphoreType.DMA(())],
        compiler_params=_cparams("arbitrary"),
        name="moe_combine",
    )(dest.reshape(T // tm, 1, tm), ys)


def _route(cls, T):
    n_blocks = T // MOE_ROWS + N_CLASSES
    n_rows = n_blocks * MOE_ROWS
    rb = math.gcd(T, MOE_ROWS)
    onehot = (cls[:, None] == jnp.arange(N_CLASSES, dtype=jnp.int32)[None, :]).astype(F32).reshape(T // rb, rb, N_CLASSES)
    strict = jnp.asarray(np.tril(np.ones((rb, rb), np.float32), -1))
    rank_in = jnp.einsum('ij,bjc->bic', strict, onehot)
    blk_counts = jnp.sum(onehot, axis=1)
    blk_off = jnp.cumsum(blk_counts, axis=0) - blk_counts
    rank = jnp.sum(onehot * (rank_in + blk_off[:, None, :]), axis=-1).reshape(T).astype(jnp.int32)
    counts = jnp.sum(blk_counts, axis=0).astype(jnp.int32)
    padded = (counts + MOE_ROWS - 1) // MOE_ROWS * MOE_ROWS
    pad_end = jnp.cumsum(padded)
    pad_start = pad_end - padded
    dest = (pad_start[cls] + rank).astype(jnp.int32)
    seg_start = jnp.concatenate([pad_start + counts, pad_end[-1:]])
    seg_len = jnp.concatenate([padded - counts, n_rows - pad_end[-1:]])
    seg_end = jnp.cumsum(seg_len)
    kk = jnp.arange(n_rows - T, dtype=jnp.int32)
    seg = jnp.searchsorted(seg_end, kk, side='right')
    pad_rows = (seg_start[seg] + kk - (seg_end[seg] - seg_len[seg])).astype(jnp.int32)
    blk_start = jnp.arange(n_blocks, dtype=jnp.int32) * MOE_ROWS
    blk_cls = jnp.minimum(jnp.searchsorted(pad_end, blk_start, side='right'), N_CLASSES - 1).astype(jnp.int32)
    nv = jnp.clip(pad_start[blk_cls] + counts[blk_cls] - blk_start, 0, MOE_ROWS).astype(jnp.int32)
    grp = blk_cls // PAIRS_PER_GROUP
    pair = blk_cls % PAIRS_PER_GROUP
    pa = jnp.asarray([0, 0, 0, 1, 1, 2], jnp.int32)[pair]
    pb = jnp.asarray([1, 2, 3, 2, 3, 3], jnp.int32)[pair]
    ea = grp * EXPERTS_PER_GROUP + pa
    eb = grp * EXPERTS_PER_GROUP + pb
    return ea, eb, nv, dest, pad_rows, n_rows


def _moe(x1w, cls_rows, w1, w3, w2, g2, b2, alpha):
    T = x1w.shape[0]
    cls = cls_rows[0].astype(jnp.int32)
    ea, eb, nv, dest, pad_rows, n_rows = _route(cls, T)
    xs = _dispatch(x1w, jnp.concatenate([dest, pad_rows]), n_rows)
    ys = _experts(xs, ea, eb, nv, w1, w3, w2, g2, b2, alpha)
    return _combine(ys, dest, T)


def _block_diag(w):
    n, d, _ = w.shape
    eye = jnp.eye(n, dtype=w.dtype)
    return jnp.einsum('nde,nm->ndme', w, eye).reshape(n * d, n * d)


def _layer(x, pos0, li, P, past):
    B, L, _ = x.shape
    T = B * L
    depth = P['depth']
    alpha = (2 * depth) ** 0.25
    w_in, b_in = P['w_in'][li], P['b_in'][li]

    def cols(lo, hi):
        return w_in[:, lo:hi], b_in[lo:hi].reshape(1, -1)

    lam_init = 0.8 - 0.6 * math.exp(-0.3 * li)
    dl = P['diff_lambda'][li]
    lam = jnp.exp(jnp.sum(dl[0] * dl[1])) - jnp.exp(jnp.sum(dl[2] * dl[3])) + lam_init
    lam_row = jnp.full((1, LANES), lam, F32)
    subln_g = P['diff_subln_g'][li].reshape(1, -1)

    wa, ba = cols(OFF_A, OFF_B)
    q, k, kb, v, vb, vt = _attn_proj(x, wa, ba, pos0)
    if past is None:
        y_a = _flash_prompt(q, kb, vt, lam_row, subln_g.reshape(-1, 1), 1.0 - lam_init)
        conv_buf = jnp.zeros((B, CONV_W - 1, B_WIDTH), F32)
        h0 = jnp.zeros((B, 1, B_WIDTH), F32)
        s0 = jnp.zeros((B, C_HEADS, C_DK, C_DV), F32)
    else:
        k_past, v_past, conv_buf, h0, s0 = past
        P_len = k_past.shape[1]
        y_a = _decode_attention(q, kb, vb, k_past.reshape(B, P_len, A_QK_WIDTH),
                                v_past.reshape(B, P_len, A_WIDTH), lam_row, subln_g, 1.0 - lam_init)
        h0 = h0.reshape(B, 1, B_WIDTH)

    wb, bb = cols(OFF_B, OFF_C)
    y_b, conv_new, h_new = _rglru(x, wb, bb, P['conv_w'][li], P['conv_b'][li].reshape(1, -1),
                                  P['lru_gate_w'][li], P['lru_gate_b'][li], P['lru_sp'][li].reshape(1, -1),
                                  conv_buf, h0, pos0)
    wc, bc = cols(OFF_C, OFF_G)
    y_c, s_new = _hgrn(x, wc, bc, P['lower_bounds'][li].reshape(1, -1), P['hgrn_norm_g'][li].reshape(1, -1), s0)

    wg, bg = cols(OFF_G, N_IN)
    x1w, cls_rows = _merge(x.reshape(T, D_MODEL), y_a.reshape(T, A_WIDTH), y_b.reshape(T, B_WIDTH),
                 y_c.reshape(T, C_WIDTH), wg, bg, P['w_out'][li], P['ln1_g'][li].reshape(1, -1),
                 P['ln1_b'][li].reshape(1, -1), P['router_w'], P['router_b'].reshape(1, -1), alpha)
    x2 = _moe(x1w, cls_rows, P['moe_w1'][li], P['moe_w3'][li], P['moe_w2'][li], P['ln2_g'][li].reshape(1, -1),
              P['ln2_b'][li].reshape(1, -1), alpha)
    states = (k.reshape(B, L, A_HEADS, 2, A_HEAD_DIM), v.reshape(B, L, A_HEADS, A_V_DIM),
              conv_new, h_new.reshape(B, B_WIDTH), s_new)
    return x2.reshape(B, L, D_MODEL), states


def kernel(x_prompt, x_sample, cache_k, cache_v, state_conv, state_lru, state_hgrn, ln_in_g, ln_in_b, w_in, b_in, diff_lambda, diff_subln_g, conv_w, conv_b, lru_wa, lru_ba, lru_wx, lru_bx, lru_lambda, hgrn_lb, hgrn_norm_g, w_out, ln1_g, ln1_b, router_w, router_b, moe_w1, moe_w3, moe_w2, ln2_g, ln2_b):
    depth = w_in.shape[0]
    p_lb = jax.nn.softmax(hgrn_lb.astype(F32), axis=0)
    P = dict(
        depth=depth,
        w_in=w_in.astype(BF16), b_in=b_in,
        diff_lambda=diff_lambda.astype(F32), diff_subln_g=diff_subln_g,
        conv_w=conv_w, conv_b=conv_b,
        lru_gate_w=jnp.stack([jnp.concatenate([_block_diag(lru_wa[l]), _block_diag(lru_wx[l])], axis=1)
                              for l in range(depth)]).astype(BF16),
        lru_gate_b=jnp.concatenate([lru_ba, lru_bx], axis=1).reshape(depth, 1, -1),
        lru_sp=jax.nn.softplus(-lru_lambda.astype(F32)),
        lower_bounds=jnp.cumsum(p_lb, axis=0) - p_lb[0],
        hgrn_norm_g=hgrn_norm_g,
        w_out=w_out.astype(BF16), ln1_g=ln1_g, ln1_b=ln1_b,
        router_w=router_w.astype(BF16), router_b=router_b,
        moe_w1=moe_w1.astype(BF16), moe_w3=moe_w3.astype(BF16), moe_w2=moe_w2.astype(BF16),
        ln2_g=ln2_g, ln2_b=ln2_b,
    )
    Bp, Lp, _ = x_prompt.shape
    Bs, Ls, _ = x_sample.shape
    past_len = cache_k.shape[2]
    xp = _input_ln(x_prompt.reshape(Bp * Lp, D_MODEL), ln_in_g, ln_in_b).reshape(Bp, Lp, D_MODEL)
    xs = _input_ln(x_sample.reshape(Bs * Ls, D_MODEL), ln_in_g, ln_in_b).reshape(Bs, Ls, D_MODEL)
    prompt_states, sample_states = [], []
    for li in range(depth):
        xp, st_p = _layer(xp, 0, li, P, None)
        xs, st_s = _layer(xs, past_len, li, P,
                          (cache_k[li], cache_v[li], state_conv[li], state_lru[li], state_hgrn[li]))
        prompt_states.append(st_p)
        sample_states.append(st_s)
    k_p, v_p, conv_p, lru_p, hgrn_p = [jnp.stack(t) for t in zip(*prompt_states)]
    k_s, v_s, conv_s, lru_s, hgrn_s = [jnp.stack(t) for t in zip(*sample_states)]
    return (xp, xs, k_p, v_p, conv_p, lru_p, hgrn_p, k_s, v_s, conv_s, lru_s, hgrn_s)
```

```python
import functools
import math

import numpy as np
import jax
import jax.numpy as jnp
from jax import lax
from jax.experimental import pallas as pl
from jax.experimental.pallas import tpu as pltpu

F32 = jnp.float32
BF16 = jnp.bfloat16

D_MODEL = 1024
CHUNK = 64
A_HEADS = 4
A_HEAD_DIM = 64
A_V_DIM = 2 * A_HEAD_DIM
A_QK_WIDTH = A_HEADS * 2 * A_HEAD_DIM
A_WIDTH = A_HEADS * A_V_DIM
ROPE_THETA = 10000.0
NEG_INF = -1e30
Q_SCALE = A_HEAD_DIM ** -0.5 * math.log2(math.e)
B_WIDTH = 512
B_BLOCKS = 8
CONV_W = 4
LRU_C = 8.0
C_HEADS = 4
C_DK = 128
C_DV = 128
C_WIDTH = C_HEADS * C_DV
F_TINY = 1e-30
N_BRANCH = 3
N_EXPERTS = 16
N_GROUPS = 4
EXPERTS_PER_GROUP = N_EXPERTS // N_GROUPS
D_EXPERT = 512
NORM_EPS = 1e-5

LANES = 128
PAIRS_PER_GROUP = 6
N_CLASSES = N_GROUPS * PAIRS_PER_GROUP
MOE_ROWS = 256
AUX_W = LANES
VMEM_LIMIT = 56 * 1024 * 1024

OFF_A = 0
OFF_B = OFF_A + 2 * A_QK_WIDTH + A_WIDTH
OFF_C = OFF_B + 2 * B_WIDTH
OFF_G = OFF_C + 4 * C_WIDTH
N_IN = OFF_G + N_BRANCH * D_MODEL


def _cparams(*sem):
    return pltpu.CompilerParams(dimension_semantics=sem, vmem_limit_bytes=VMEM_LIMIT)


def _dot(a, b):
    return jnp.dot(a, b, preferred_element_type=F32)


def _dot_nt(a, b):
    return lax.dot_general(a, b, (((1,), (1,)), ((), ())), preferred_element_type=F32)


def _dot_tn(a, b):
    return lax.dot_general(a, b, (((0,), (0,)), ((), ())), preferred_element_type=F32)


def _sigmoid(x):
    return 1.0 / (1.0 + jnp.exp(-x))


def _silu(x):
    return x * _sigmoid(x)


def _layer_norm(x, g, b):
    mu = jnp.mean(x, axis=-1, keepdims=True)
    xc = x - mu
    var = jnp.mean(xc * xc, axis=-1, keepdims=True)
    return xc * lax.rsqrt(var + NORM_EPS) * g + b


def _ln_kernel(x_ref, g_ref, b_ref, o_ref):
    o_ref[...] = _layer_norm(x_ref[...], g_ref[...], b_ref[...])


def _input_ln(x2d, g, b):
    T = x2d.shape[0]
    tm = min(T, 512)
    return pl.pallas_call(
        _ln_kernel,
        grid=(T // tm,),
        in_specs=[pl.BlockSpec((tm, D_MODEL), lambda i: (i, 0)),
                  pl.BlockSpec((1, D_MODEL), lambda i: (0, 0)),
                  pl.BlockSpec((1, D_MODEL), lambda i: (0, 0))],
        out_specs=pl.BlockSpec((tm, D_MODEL), lambda i: (i, 0)),
        out_shape=jax.ShapeDtypeStruct((T, D_MODEL), F32),
        compiler_params=_cparams("parallel"),
        name="input_ln",
    )(x2d, g.reshape(1, -1), b.reshape(1, -1))


def _attn_proj_kernel(x_ref, w_ref, b_ref, cos_ref, sin_ref, q_ref, k_ref, kb_ref, v_ref, vb_ref, vt_ref):
    x = x_ref[0].astype(BF16)
    y = _dot(x, w_ref[...]) + b_ref[...]
    cos = cos_ref[...]
    sin = sin_ref[...]
    lane = lax.broadcasted_iota(jnp.int32, cos.shape, 1)
    first_half = (lane % A_HEAD_DIM) < (A_HEAD_DIM // 2)
    for s in range(2 * A_QK_WIDTH // LANES):
        blk = y[:, s * LANES:(s + 1) * LANES]
        swapped = jnp.where(first_half,
                            pltpu.roll(blk, LANES - A_HEAD_DIM // 2, 1),
                            pltpu.roll(blk, A_HEAD_DIM // 2, 1))
        rot = blk * cos + swapped * sin
        if s < A_QK_WIDTH // LANES:
            q_ref[0, :, s * LANES:(s + 1) * LANES] = (rot * Q_SCALE).astype(BF16)
        else:
            o = s * LANES - A_QK_WIDTH
            k_ref[0, :, o:o + LANES] = rot
            kb_ref[0, :, o:o + LANES] = rot.astype(BF16)
    v = y[:, 2 * A_QK_WIDTH:]
    v_ref[0] = v
    vb_ref[0] = v.astype(BF16)
    vt_ref[0] = v.T.astype(BF16)


def _rope_tables(pos0, L):
    half = A_HEAD_DIM // 2
    inv_freq = ROPE_THETA ** (-jnp.arange(half, dtype=F32) / half)
    pos = (pos0 + jnp.arange(L)).astype(F32)
    ang = pos[:, None] * inv_freq[None, :]
    cos = jnp.cos(ang)
    sin = jnp.sin(ang)
    reps = LANES // A_HEAD_DIM
    cos_t = jnp.tile(jnp.concatenate([cos, cos], axis=1), (1, reps))
    sin_t = jnp.tile(jnp.concatenate([-sin, sin], axis=1), (1, reps))
    return cos_t, sin_t


def _attn_proj(x, w, b, pos0):
    B, L, _ = x.shape
    tl = min(L, 512)
    cos_t, sin_t = _rope_tables(pos0, L)
    wa = A_QK_WIDTH
    n = 2 * A_QK_WIDTH + A_WIDTH
    row = lambda bi, j: (bi, j, 0)
    return pl.pallas_call(
        _attn_proj_kernel,
        grid=(B, L // tl),
        in_specs=[pl.BlockSpec((1, tl, D_MODEL), row),
                  pl.BlockSpec((D_MODEL, n), lambda bi, j: (0, 0)),
                  pl.BlockSpec((1, n), lambda bi, j: (0, 0)),
                  pl.BlockSpec((tl, LANES), lambda bi, j: (j, 0)),
                  pl.BlockSpec((tl, LANES), lambda bi, j: (j, 0))],
        out_specs=[pl.BlockSpec((1, tl, wa), row)] * 5 + [pl.BlockSpec((1, wa, tl), lambda bi, j: (bi, 0, j))],
        out_shape=[jax.ShapeDtypeStruct((B, L, wa), BF16),
                   jax.ShapeDtypeStruct((B, L, wa), F32),
                   jax.ShapeDtypeStruct((B, L, wa), BF16),
                   jax.ShapeDtypeStruct((B, L, wa), F32),
                   jax.ShapeDtypeStruct((B, L, wa), BF16),
                   jax.ShapeDtypeStruct((B, wa, L), BF16)],
        compiler_params=_cparams("parallel", "parallel"),
        name="attn_proj",
    )(x, w, b, cos_t, sin_t)


def _split_maps(q):
    lane = lax.broadcasted_iota(jnp.int32, q.shape, 1)
    zero = jnp.zeros_like(q)
    return jnp.where(lane < A_HEAD_DIM, q, zero), jnp.where(lane >= A_HEAD_DIM, q, zero)


def _flash_kernel(lam_ref, g_ref, q_ref, k_ref, vt_ref, o_ref, m_s, l_s, acc_s, sa_s, sb_s, *, tq, out_scale):
    i = pl.program_id(2)
    qz = _split_maps(q_ref[0])
    qq = jnp.concatenate(qz, axis=0)
    m_s[...] = jnp.full(m_s.shape, NEG_INF, F32)
    l_s[...] = jnp.zeros(l_s.shape, F32)
    acc_s[...] = jnp.zeros(acc_s.shape, F32)

    def scores_into(dst, blk):
        off = pl.multiple_of(blk * tq, tq)
        dst[...] = _dot_nt(k_ref[0, pl.ds(off, tq), :], qq)

    def update(src, blk, diagonal):
        off = pl.multiple_of(blk * tq, tq)
        vt = vt_ref[0, :, pl.ds(off, tq)]
        s = src[...]
        if diagonal:
            key = lax.broadcasted_iota(jnp.int32, (tq, 2 * tq), 0) // CHUNK
            qry = (lax.broadcasted_iota(jnp.int32, (tq, 2 * tq), 1) % tq) // CHUNK
            s = jnp.where(key <= qry, s, NEG_INF)
        m_old = m_s[...]
        m_new = jnp.maximum(m_old, jnp.max(s, axis=0, keepdims=True))
        alpha = jnp.exp2(m_old - m_new)
        p = jnp.exp2(s - m_new)
        l_s[...] = alpha * l_s[...] + jnp.sum(p, axis=0, keepdims=True)
        acc_s[...] = alpha * acc_s[...] + _dot(vt, p.astype(BF16))
        m_s[...] = m_new

    scores_into(sa_s, 0)

    def pair(p, carry):
        scores_into(sb_s, 2 * p + 1)
        update(sa_s, 2 * p, False)
        scores_into(sa_s, 2 * p + 2)
        update(sb_s, 2 * p + 1, False)
        return carry

    lax.fori_loop(0, i // 2, pair, 0)

    @pl.when(i % 2 == 1)
    def _():
        scores_into(sb_s, i)
        update(sa_s, i - 1, False)
        update(sb_s, i, True)

    @pl.when(i % 2 == 0)
    def _():
        update(sa_s, i, True)

    inv_l = 1.0 / l_s[...]
    o = acc_s[...] * inv_l
    o = o[:, :tq] - lam_ref[0:1, 0:1] * o[:, tq:]
    ms = jnp.mean(o * o, axis=0, keepdims=True)
    y = o * lax.rsqrt(ms + NORM_EPS) * (g_ref[...] * out_scale)
    o_ref[0] = y.T.astype(o_ref.dtype)


def _flash_prompt(q, kb, vt, lam_row, g_col, out_scale):
    B, L, _ = q.shape
    tq = min(L, 512)
    return pl.pallas_call(
        functools.partial(_flash_kernel, tq=tq, out_scale=out_scale),
        grid=(B, A_HEADS, L // tq),
        in_specs=[pl.BlockSpec((1, LANES), lambda b, h, i: (0, 0)),
                  pl.BlockSpec((A_V_DIM, 1), lambda b, h, i: (0, 0)),
                  pl.BlockSpec((1, tq, LANES), lambda b, h, i: (b, i, h)),
                  pl.BlockSpec((1, L, LANES), lambda b, h, i: (b, 0, h)),
                  pl.BlockSpec((1, A_V_DIM, L), lambda b, h, i: (b, h, 0))],
        out_specs=pl.BlockSpec((1, tq, A_V_DIM), lambda b, h, i: (b, i, h)),
        out_shape=jax.ShapeDtypeStruct((B, L, A_WIDTH), BF16),
        scratch_shapes=[pltpu.VMEM((1, 2 * tq), F32), pltpu.VMEM((1, 2 * tq), F32),
                        pltpu.VMEM((A_V_DIM, 2 * tq), F32),
                        pltpu.VMEM((tq, 2 * tq), F32), pltpu.VMEM((tq, 2 * tq), F32)],
        compiler_params=_cparams("parallel", "parallel", "arbitrary"),
        name="flash_prompt",
    )(lam_row, g_col, q, kb, vt)


def _decode_kernel(lam_ref, g_ref, q_ref, kn_ref, vn_ref, kp_ref, vp_ref, o_ref, s_s, acc_s, m_s, r_s,
                   *, nb, tk, L, out_scale):
    j = pl.program_id(1)
    P = nb * tk
    n_new = kn_ref.shape[1]

    def q_rows():
        rep = jnp.concatenate([q_ref[0]] * (LANES // L), axis=0)
        grp = lax.broadcasted_iota(jnp.int32, rep.shape, 0) // L
        cols = lax.broadcasted_iota(jnp.int32, rep.shape, 1) // A_HEAD_DIM
        return jnp.where(grp == cols, rep, jnp.zeros_like(rep))

    def new_scores():
        s = _dot_nt(kn_ref[0], q_rows())
        valid = lax.broadcasted_iota(jnp.int32, s.shape, 0) < L
        return jnp.where(valid, s, NEG_INF)

    @pl.when(j < nb)
    def _():
        off = pl.multiple_of(j * tk, tk)
        s_s[pl.ds(off, tk), :] = _dot_nt(kp_ref[0].astype(BF16), q_rows())

    @pl.when(j == nb - 1)
    def _():
        s_s[P:P + n_new, :] = new_scores()
        s = s_s[...]
        m = jnp.max(s, axis=0, keepdims=True)
        m_s[...] = m
        r_s[...] = 1.0 / jnp.sum(jnp.exp2(s - m), axis=0, keepdims=True)
        acc_s[...] = jnp.zeros(acc_s.shape, F32)

    def weights(rows):
        return (jnp.exp2(s_s[rows, :] - m_s[...]) * r_s[...]).astype(BF16)

    @pl.when(j >= nb)
    def _():
        off = pl.multiple_of((j - nb) * tk, tk)
        acc_s[...] += _dot_tn(weights(pl.ds(off, tk)), vp_ref[0].astype(BF16))

    @pl.when(j == 2 * nb - 1)
    def _():
        acc = acc_s[...] + _dot_tn(weights(slice(P, P + n_new)), vn_ref[0])
        for h in range(A_HEADS):
            hs = slice(h * A_V_DIM, (h + 1) * A_V_DIM)
            o = acc[2 * h * L:(2 * h + 1) * L, hs] - lam_ref[...] * acc[(2 * h + 1) * L:(2 * h + 2) * L, hs]
            ms = jnp.mean(o * o, axis=-1, keepdims=True)
            o_ref[0, :, hs] = (o * lax.rsqrt(ms + NORM_EPS) * g_ref[...] * out_scale).astype(o_ref.dtype)


def _decode_attention(q, k_new, v_new, k_past, v_past, lam_row, g_row, out_scale):
    B, L, _ = q.shape
    assert 2 * A_HEADS * L == LANES, "decode attention packs (head, map, frame) onto the 128 lanes"
    P = k_past.shape[1]
    tk = min(P, 1024)
    nb = P // tk
    pad = ((0, 0), (0, LANES - L), (0, 0))
    k_new = jnp.pad(k_new, pad)
    v_new = jnp.pad(v_new, pad)
    new = lambda b, j: (b, 0, 0)
    return pl.pallas_call(
        functools.partial(_decode_kernel, nb=nb, tk=tk, L=L, out_scale=out_scale),
        grid=(B, 2 * nb),
        in_specs=[pl.BlockSpec((1, LANES), lambda b, j: (0, 0)),
                  pl.BlockSpec((1, A_V_DIM), lambda b, j: (0, 0)),
                  pl.BlockSpec((1, L, A_QK_WIDTH), new),
                  pl.BlockSpec((1, LANES, A_QK_WIDTH), new),
                  pl.BlockSpec((1, LANES, A_WIDTH), new),
                  pl.BlockSpec((1, tk, A_QK_WIDTH), lambda b, j: (b, jnp.minimum(j, nb - 1), 0)),
                  pl.BlockSpec((1, tk, A_WIDTH), lambda b, j: (b, jnp.maximum(j - nb, 0), 0))],
        out_specs=pl.BlockSpec((1, L, A_WIDTH), new),
        out_shape=jax.ShapeDtypeStruct((B, L, A_WIDTH), BF16),
        scratch_shapes=[pltpu.VMEM((P + LANES, LANES), F32), pltpu.VMEM((LANES, A_WIDTH), F32),
                        pltpu.VMEM((1, LANES), F32), pltpu.VMEM((1, LANES), F32)],
        compiler_params=_cparams("parallel", "arbitrary"),
        name="decode_attention",
    )(lam_row, g_row, q, k_new, v_new, k_past, v_past)


def _gelu_tanh(x):
    return 0.5 * x * (1.0 + jnp.tanh(math.sqrt(2.0 / math.pi) * (x + 0.044715 * (x * x * x))))


def _rglru_kernel(x_ref, w_ref, b_ref, cw_ref, cb_ref, gw_ref, gbias_ref, sp_ref, cbuf_ref, h0_ref,
                  y_ref, conv_ref, h_ref, xs, hc, *, tl, pos0):
    j = pl.program_id(1)
    tail = CONV_W - 1
    base = 8

    @pl.when(j == 0)
    def _():
        xs[0:base, :] = jnp.zeros((base, B_WIDTH), F32)
        xs[base - tail:base, :] = cbuf_ref[0]
        hc[...] = h0_ref[0]

    x = x_ref[0].astype(BF16)
    y = _dot(x, w_ref[...]) + b_ref[...]
    xb = y[:, :B_WIDTH]
    gb = y[:, B_WIDTH:]
    xs[base:base + tl, :] = xb
    xc = cb_ref[...] + cw_ref[0:1, :] * xs[base - tail:base - tail + tl, :]
    for t in range(1, CONV_W):
        xc = xc + cw_ref[t:t + 1, :] * xs[base - tail + t:base - tail + t + tl, :]
    new_tail = xs[base + tl - tail:base + tl, :]
    xs[base - tail:base, :] = new_tail
    conv_ref[0] = new_tail

    gates = _dot(xc.astype(BF16), gw_ref[...]) + gbias_ref[...]
    r = _sigmoid(gates[:, :B_WIDTH])
    ig = _sigmoid(gates[:, B_WIDTH:])
    log_a = -LRU_C * r * sp_ref[...]
    a = jnp.exp(log_a)
    t = jnp.tanh(log_a)
    mult = jnp.sqrt(-2.0 * t / (1.0 - t))
    row = lax.broadcasted_iota(jnp.int32, (tl, B_WIDTH), 0)
    mult = jnp.where(row + (pos0 + j * tl) == 0, 1.0, mult)
    bb = xc * ig * mult
    d = 1
    while d < tl:
        keep = row >= d
        a_sh = pltpu.roll(a, d, 0)
        b_sh = pltpu.roll(bb, d, 0)
        bb = jnp.where(keep, a * b_sh + bb, bb)
        a = jnp.where(keep, a * a_sh, a)
        d *= 2
    h = a * hc[...] + bb
    h_last = h[tl - 1:tl, :]
    hc[...] = h_last
    h_ref[0] = h_last
    y_ref[0] = (_gelu_tanh(gb) * h).astype(y_ref.dtype)


def _rglru(x, w, b, conv_w, conv_b, gate_w, gate_b, sp, conv_buf, h0, pos0):
    B, L, _ = x.shape
    tl = min(L, 256)
    tail = CONV_W - 1
    const = lambda bi, j: (0, 0)
    per_b = lambda bi, j: (bi, 0, 0)
    return pl.pallas_call(
        functools.partial(_rglru_kernel, tl=tl, pos0=pos0),
        grid=(B, L // tl),
        in_specs=[pl.BlockSpec((1, tl, D_MODEL), lambda bi, j: (bi, j, 0)),
                  pl.BlockSpec((D_MODEL, 2 * B_WIDTH), const),
                  pl.BlockSpec((1, 2 * B_WIDTH), const),
                  pl.BlockSpec((CONV_W, B_WIDTH), const),
                  pl.BlockSpec((1, B_WIDTH), const),
                  pl.BlockSpec((B_WIDTH, 2 * B_WIDTH), const),
                  pl.BlockSpec((1, 2 * B_WIDTH), const),
                  pl.BlockSpec((1, B_WIDTH), const),
                  pl.BlockSpec((1, tail, B_WIDTH), per_b),
                  pl.BlockSpec((1, 1, B_WIDTH), per_b)],
        out_specs=[pl.BlockSpec((1, tl, B_WIDTH), lambda bi, j: (bi, j, 0)),
                   pl.BlockSpec((1, tail, B_WIDTH), per_b),
                   pl.BlockSpec((1, 1, B_WIDTH), per_b)],
        out_shape=[jax.ShapeDtypeStruct((B, L, B_WIDTH), BF16),
                   jax.ShapeDtypeStruct((B, tail, B_WIDTH), F32),
                   jax.ShapeDtypeStruct((B, 1, B_WIDTH), F32)],
        scratch_shapes=[pltpu.VMEM((8 + tl, B_WIDTH), F32), pltpu.VMEM((1, B_WIDTH), F32)],
        compiler_params=_cparams("parallel", "arbitrary"),
        name="rglru",
    )(x, w, b, conv_w, conv_b, gate_w, gate_b, sp, conv_buf, h0)


def _hgrn_levels(C):
    return [C >> i for i in range(int(math.log2(C)) + 1)]


HGRN_SPLIT = 3


def _hgrn_sum_matrices(C):
    t = np.arange(C)

    def prefix(m):
        return (t[None, :] >= ((t // m) * m)[:, None]) & (t[None, :] <= t[:, None])

    def suffix(m):
        return (t[None, :] > t[:, None]) & (t[None, :] <= ((t // m) * m + m - 1)[:, None])

    mats = [prefix(C), suffix(C)]
    for m in _hgrn_levels(C)[1:-1]:
        upper = ((t // m) % 2 == 1)[:, None]
        mats.append(np.where(upper, prefix(m), suffix(m)))
    one = np.concatenate(mats, axis=0).astype(np.float32)
    return np.concatenate([one] * HGRN_SPLIT, axis=1)


def _hgrn_kernel(x_ref, w_ref, b_ref, lb_ref, ng_ref, dm_ref, s0_ref, y_ref, s_ref,
                 q_s, k_s, g_s, v_s, gate_s, o_s, st_s, *, tl, C):
    j = pl.program_id(1)
    levels = _hgrn_levels(C)
    nlev = len(levels)

    @pl.when(j == 0)
    def _():
        for h in range(C_HEADS):
            st_s[h] = s0_ref[0, h].T

    x = x_ref[0].astype(BF16)
    y = _dot(x, w_ref[...]) + b_ref[...]
    lb = lb_ref[...]
    z = y[:, C_WIDTH:2 * C_WIDTH]
    f = lb + (1.0 - lb) * _sigmoid(z)
    q_s[...] = _silu(y[:, :C_WIDTH])
    k_s[...] = (1.0 - lb) * _sigmoid(-z)
    g_s[...] = jnp.log(jnp.maximum(f, F_TINY))
    v_s[...] = y[:, 2 * C_WIDTH:3 * C_WIDTH]
    gate_s[...] = _silu(y[:, 3 * C_WIDTH:])

    ti = lax.broadcasted_iota(jnp.int32, (C, C), 0)
    si = lax.broadcasted_iota(jnp.int32, (C, C), 1)
    masks = [ti == si]
    for m in levels[1:]:
        masks.append((ti // (2 * m) == si // (2 * m)) & ((ti // m) % 2 == 1) & ((si // m) % 2 == 0))
    odd_row = lax.broadcasted_iota(jnp.int32, (C, C_WIDTH), 0) % 2 == 1

    def chunk(ci, carry):
        r0 = pl.multiple_of(ci * C, C)
        rows = pl.ds(r0, C)
        g = g_s[rows, :]
        pieces = []
        rest = g
        for _ in range(HGRN_SPLIT):
            piece = rest.astype(BF16)
            pieces.append(piece)
            rest = rest - piece.astype(F32)
        sums = _dot(dm_ref[...], jnp.concatenate(pieces, axis=0))
        e = jnp.exp(jnp.minimum(sums, 0.0))
        e_one = jnp.exp(jnp.where(odd_row, g, 0.0))
        for h in range(C_HEADS):
            hs = slice(h * C_DK, (h + 1) * C_DK)
            q = q_s[rows, hs]
            k = k_s[rows, hs]
            vb = v_s[rows, hs].astype(BF16)
            scores = jnp.where(masks[0], _dot_nt(q.astype(BF16), k.astype(BF16)), 0.0)
            for li in range(1, nlev):
                el = e[(li + 1) * C:(li + 2) * C, hs] if li < nlev - 1 else e_one[:, hs]
                scores = scores + jnp.where(masks[li], _dot_nt((q * el).astype(BF16), (k * el).astype(BF16)), 0.0)
            e_cum = e[0:C, hs]
            qc = (q * e_cum).astype(BF16)
            kc = (k * e[C:2 * C, hs]).astype(BF16)
            st = st_s[h]
            o = _dot(scores.astype(BF16), vb) + _dot_nt(qc, st.astype(BF16))
            st_s[h] = st * e_cum[C - 1:C, :] + _dot_tn(vb, kc)
            ms = jnp.mean(o * o, axis=-1, keepdims=True)
            o_s[rows, hs] = o * lax.rsqrt(ms + NORM_EPS) * ng_ref[...] * gate_s[rows, hs]
        return carry

    lax.fori_loop(0, tl // C, chunk, 0, unroll=True)
    y_ref[0] = o_s[...].astype(y_ref.dtype)
    for h in range(C_HEADS):
        s_ref[0, h] = st_s[h].T


def _hgrn(x, w, b, lb, norm_g, s0):
    B, L, _ = x.shape
    C = min(CHUNK, L)
    tl = min(L, 256)
    dm = jnp.asarray(_hgrn_sum_matrices(C), BF16)
    const = lambda bi, j: (0, 0)
    per_b = lambda bi, j: (bi, 0, 0, 0)
    wide = pltpu.VMEM((tl, C_WIDTH), F32)
    return pl.pallas_call(
        functools.partial(_hgrn_kernel, tl=tl, C=C),
        grid=(B, L // tl),
        in_specs=[pl.BlockSpec((1, tl, D_MODEL), lambda bi, j: (bi, j, 0)),
                  pl.BlockSpec((D_MODEL, 4 * C_WIDTH), const),
                  pl.BlockSpec((1, 4 * C_WIDTH), const),
                  pl.BlockSpec((1, C_WIDTH), const),
                  pl.BlockSpec((1, C_DV), const),
                  pl.BlockSpec(dm.shape, const),
                  pl.BlockSpec((1, C_HEADS, C_DK, C_DV), per_b)],
        out_specs=[pl.BlockSpec((1, tl, C_WIDTH), lambda bi, j: (bi, j, 0)),
                   pl.BlockSpec((1, C_HEADS, C_DK, C_DV), per_b)],
        out_shape=[jax.ShapeDtypeStruct((B, L, C_WIDTH), BF16),
                   jax.ShapeDtypeStruct((B, C_HEADS, C_DK, C_DV), F32)],
        scratch_shapes=[wide, wide, wide, wide, wide, wide, pltpu.VMEM((C_HEADS, C_DV, C_DK), F32)],
        compiler_params=_cparams("parallel", "arbitrary"),
        name="hgrn",
    )(x, w, b, lb, norm_g, dm, s0)


def _merge_kernel(x_ref, ya_ref, yb_ref, yc_ref, wg_ref, bg_ref, wo_ref, g1_ref, b1_ref, rw_ref, rb_ref,
                  o_ref, cls_ref, *, alpha):
    x = x_ref[...]
    xb = x.astype(BF16)
    mixed = None
    for n, y_ref in enumerate((ya_ref, yb_ref, yc_ref)):
        cols = slice(n * D_MODEL, (n + 1) * D_MODEL)
        gate = _sigmoid(_dot(xb, wg_ref[:, cols]) + bg_ref[:, cols])
        rows = slice(n * A_WIDTH, (n + 1) * A_WIDTH)
        term = gate * _dot(y_ref[...], wo_ref[rows, :])
        mixed = term if mixed is None else mixed + term
    x1 = _layer_norm(alpha * x + mixed, g1_ref[...], b1_ref[...])
    o_ref[:, :D_MODEL] = x1

    lg = _dot(x1.astype(BF16), rw_ref[...]) + rb_ref[...]
    col = [lg[:, e:e + 1] for e in range(N_EXPERTS)]
    gmax = [functools.reduce(jnp.maximum, col[g * EXPERTS_PER_GROUP:(g + 1) * EXPERTS_PER_GROUP])
            for g in range(N_GROUPS)]
    best = functools.reduce(jnp.maximum, gmax)
    gs = jnp.where(gmax[0] == best, 0, jnp.where(gmax[1] == best, 1, jnp.where(gmax[2] == best, 2, 3)))
    v = []
    for e in range(EXPERTS_PER_GROUP):
        v.append(jnp.where(gs == 0, col[e],
                           jnp.where(gs == 1, col[EXPERTS_PER_GROUP + e],
                                     jnp.where(gs == 2, col[2 * EXPERTS_PER_GROUP + e],
                                               col[3 * EXPERTS_PER_GROUP + e]))))
    m1 = best
    i1 = jnp.where(v[0] == m1, 0, jnp.where(v[1] == m1, 1, jnp.where(v[2] == m1, 2, 3)))
    v2 = [jnp.where(i1 == e, -jnp.inf, v[e]) for e in range(EXPERTS_PER_GROUP)]
    m2 = functools.reduce(jnp.maximum, v2)
    i2 = jnp.where(v2[0] == m2, 0, jnp.where(v2[1] == m2, 1, jnp.where(v2[2] == m2, 2, 3)))
    e2 = jnp.exp(m2 - m1)
    p1 = 1.0 / (1.0 + e2)
    p2 = e2 / (1.0 + e2)
    ea = jnp.minimum(i1, i2)
    eb = jnp.maximum(i1, i2)
    ga = jnp.where(i1 < i2, p1, p2)
    gb = jnp.where(i1 < i2, p2, p1)
    pair = jnp.where(ea == 0, 0, jnp.where(ea == 1, 3, 5)) + (eb - ea - 1)
    cls = (gs * PAIRS_PER_GROUP + pair).astype(F32)
    lane = lax.broadcasted_iota(jnp.int32, (x.shape[0], AUX_W), 1)
    aux = jnp.where(lane == 0, ga, jnp.where(lane == 1, gb, jnp.where(lane == 2, cls, 0.0)))
    o_ref[:, D_MODEL:] = aux
    pick = (lax.broadcasted_iota(jnp.int32, (8, AUX_W), 0) == 0) & (lax.broadcasted_iota(jnp.int32, (8, AUX_W), 1) == 2)
    cls_ref[...] = _dot_nt(jnp.where(pick, 1.0, 0.0).astype(BF16), aux.astype(BF16))


def _merge(x2d, ya, yb, yc, wg, bg, wo, g1, b1, rw, rb, alpha):
    T = x2d.shape[0]
    tm = min(T, 512)
    row = lambda i: (i, 0)
    const = lambda i: (0, 0)
    return pl.pallas_call(
        functools.partial(_merge_kernel, alpha=alpha),
        grid=(T // tm,),
        in_specs=[pl.BlockSpec((tm, D_MODEL), row),
                  pl.BlockSpec((tm, A_WIDTH), row),
                  pl.BlockSpec((tm, B_WIDTH), row),
                  pl.BlockSpec((tm, C_WIDTH), row),
                  pl.BlockSpec((D_MODEL, N_BRANCH * D_MODEL), const),
                  pl.BlockSpec((1, N_BRANCH * D_MODEL), const),
                  pl.BlockSpec((N_BRANCH * A_WIDTH, D_MODEL), const),
                  pl.BlockSpec((1, D_MODEL), const),
                  pl.BlockSpec((1, D_MODEL), const),
                  pl.BlockSpec((D_MODEL, N_EXPERTS), const),
                  pl.BlockSpec((1, N_EXPERTS), const)],
        out_specs=[pl.BlockSpec((tm, D_MODEL + AUX_W), row), pl.BlockSpec((8, tm), lambda i: (0, i))],
        out_shape=[jax.ShapeDtypeStruct((T, D_MODEL + AUX_W), F32), jax.ShapeDtypeStruct((8, T), F32)],
        compiler_params=_cparams("parallel"),
        name="merge",
    )(x2d, ya, yb, yc, wg, bg, wo, g1, b1, rw, rb)


def _dispatch_kernel(dest_ref, x_ref, out_hbm, zbuf, sem, *, tm, n_tok_steps):
    i = pl.program_id(0)

    def scatter_rows(src):
        def start(r, c):
            pltpu.make_async_copy(src.at[pl.ds(r, 1), :], out_hbm.at[pl.ds(dest_ref[0, 0, r], 1), :], sem).start()
            return c

        lax.fori_loop(0, tm, start, 0, unroll=8)
        pltpu.make_async_copy(src, out_hbm.at[pl.ds(0, tm), :], sem).wait()

    @pl.when(i < n_tok_steps)
    def _():
        scatter_rows(x_ref)

    @pl.when(i >= n_tok_steps)
    def _():
        @pl.when(i == n_tok_steps)
        def _():
            zbuf[...] = jnp.zeros(zbuf.shape, F32)

        scatter_rows(zbuf)


def _dispatch(x1w, dest_all, n_rows):
    T, W = x1w.shape
    tm = min(T, 512)
    n_tok_steps = T // tm
    steps = dest_all.shape[0] // tm
    return pl.pallas_call(
        functools.partial(_dispatch_kernel, tm=tm, n_tok_steps=n_tok_steps),
        grid=(steps,),
        in_specs=[pl.BlockSpec((1, 1, tm), lambda i: (i, 0, 0), memory_space=pltpu.SMEM),
                  pl.BlockSpec((tm, W), lambda i: (jnp.minimum(i, n_tok_steps - 1), 0))],
        out_specs=pl.BlockSpec(memory_space=pl.ANY),
        out_shape=jax.ShapeDtypeStruct((n_rows, W), F32),
        scratch_shapes=[pltpu.VMEM((tm, W), F32), pltpu.SemaphoreType.DMA(())],
        compiler_params=_cparams("arbitrary"),
        name="moe_dispatch",
    )(dest_all.reshape(steps, 1, tm), x1w)


def _experts_kernel(ea_ref, eb_ref, nv_ref, x_ref, w1a, w3a, w2a, w1b, w3b, w2b, g2_ref, b2_ref, o_ref, *, alpha):
    i = pl.program_id(0)

    @pl.when(nv_ref[i] > 0)
    def _():
        xw = x_ref[...]
        x = xw[:, :D_MODEL]
        xb = x.astype(BF16)
        y = None
        for n, (w1, w3, w2) in enumerate(((w1a, w3a, w2a), (w1b, w3b, w2b))):
            hmid = _silu(_dot(xb, w1[0])) * _dot(xb, w3[0])
            ye = xw[:, D_MODEL + n:D_MODEL + n + 1] * _dot(hmid.astype(BF16), w2[0])
            y = ye if y is None else y + ye
        o_ref[...] = _layer_norm(alpha * x + y, g2_ref[...], b2_ref[...])

    @pl.when(nv_ref[i] == 0)
    def _():
        o_ref[...] = jnp.zeros(o_ref.shape, F32)


def _experts(xs, ea, eb, nv, w1, w3, w2, g2, b2, alpha):
    n_rows, W = xs.shape
    wa = lambda i, ea, eb, nv: (ea[i], 0, 0)
    wb = lambda i, ea, eb, nv: (eb[i], 0, 0)
    const = lambda i, ea, eb, nv: (0, 0)
    row = lambda i, ea, eb, nv: (i, 0)
    up = (1, D_MODEL, D_EXPERT)
    down = (1, D_EXPERT, D_MODEL)
    grid_spec = pltpu.PrefetchScalarGridSpec(
        num_scalar_prefetch=3,
        grid=(n_rows // MOE_ROWS,),
        in_specs=[pl.BlockSpec((MOE_ROWS, W), row),
                  pl.BlockSpec(up, wa), pl.BlockSpec(up, wa), pl.BlockSpec(down, wa),
                  pl.BlockSpec(up, wb), pl.BlockSpec(up, wb), pl.BlockSpec(down, wb),
                  pl.BlockSpec((1, D_MODEL), const), pl.BlockSpec((1, D_MODEL), const)],
        out_specs=pl.BlockSpec((MOE_ROWS, D_MODEL), row),
    )
    return pl.pallas_call(
        functools.partial(_experts_kernel, alpha=alpha),
        grid_spec=grid_spec,
        out_shape=jax.ShapeDtypeStruct((n_rows, D_MODEL), F32),
        compiler_params=_cparams("arbitrary"),
        name="moe_experts",
    )(ea, eb, nv, xs, w1, w3, w2, w1, w3, w2, g2, b2)


def _combine_kernel(dest_ref, y_hbm, o_ref, sem, *, tm):
    def start(r, c):
        pltpu.make_async_copy(y_hbm.at[pl.ds(dest_ref[0, 0, r], 1), :], o_ref.at[pl.ds(r, 1), :], sem).start()
        return c

    lax.fori_loop(0, tm, start, 0, unroll=8)
    pltpu.make_async_copy(y_hbm.at[pl.ds(0, tm), :], o_ref, sem).wait()


def _combine(ys, dest, T):
    tm = min(T, 512)
    return pl.pallas_call(
        functools.partial(_combine_kernel, tm=tm),
        grid=(T // tm,),
        in_specs=[pl.BlockSpec((1, 1, tm), lambda i: (i, 0, 0), memory_space=pltpu.SMEM),
                  pl.BlockSpec(memory_space=pl.ANY)],
        out_specs=pl.BlockSpec((tm, D_MODEL), lambda i: (i, 0)),
        out_shape=jax.ShapeDtypeStruct((T, D_MODEL), F32),
        scratch_shapes=[pltpu.SemaphoreType.DMA(())],
        compiler_params=_cparams("arbitrary"),
        name="moe_combine",
    )(dest.reshape(T // tm, 1, tm), ys)


def _route(cls, T):
    n_blocks = T // MOE_ROWS + N_CLASSES
    n_rows = n_blocks * MOE_ROWS
    rb = math.gcd(T, MOE_ROWS)
    onehot = (cls[:, None] == jnp.arange(N_CLASSES, dtype=jnp.int32)[None, :]).astype(F32).reshape(T // rb, rb, N_CLASSES)
    strict = jnp.asarray(np.tril(np.ones((rb, rb), np.float32), -1))
    rank_in = jnp.einsum('ij,bjc->bic', strict, onehot)
    blk_counts = jnp.sum(onehot, axis=1)
    blk_off = jnp.cumsum(blk_counts, axis=0) - blk_counts
    rank = jnp.sum(onehot * (rank_in + blk_off[:, None, :]), axis=-1).reshape(T).astype(jnp.int32)
    counts = jnp.sum(blk_counts, axis=0).astype(jnp.int32)
    padded = (counts + MOE_ROWS - 1) // MOE_ROWS * MOE_ROWS
    pad_end = jnp.cumsum(padded)
    pad_start = pad_end - padded
    dest = (pad_start[cls] + rank).astype(jnp.int32)
    seg_start = jnp.concatenate([pad_start + counts, pad_end[-1:]])
    seg_len = jnp.concatenate([padded - counts, n_rows - pad_end[-1:]])
    seg_end = jnp.cumsum(seg_len)
    kk = jnp.arange(n_rows - T, dtype=jnp.int32)
    seg = jnp.sum((seg_end[None, :] <= kk[:, None]).astype(jnp.int32), axis=1)
    pad_rows = (seg_start[seg] + kk - (seg_end[seg] - seg_len[seg])).astype(jnp.int32)
    blk_start = jnp.arange(n_blocks, dtype=jnp.int32) * MOE_ROWS
    blk_cls = jnp.minimum(jnp.sum((pad_end[None, :] <= blk_start[:, None]).astype(jnp.int32), axis=1), N_CLASSES - 1)
    nv = jnp.clip(pad_start[blk_cls] + counts[blk_cls] - blk_start, 0, MOE_ROWS).astype(jnp.int32)
    grp = blk_cls // PAIRS_PER_GROUP
    pair = blk_cls % PAIRS_PER_GROUP
    pa = jnp.asarray([0, 0, 0, 1, 1, 2], jnp.int32)[pair]
    pb = jnp.asarray([1, 2, 3, 2, 3, 3], jnp.int32)[pair]
    ea = grp * EXPERTS_PER_GROUP + pa
    eb = grp * EXPERTS_PER_GROUP + pb
    return ea, eb, nv, dest, pad_rows, n_rows


def _moe(x1w, cls_rows, w1, w3, w2, g2, b2, alpha):
    T = x1w.shape[0]
    cls = cls_rows[0].astype(jnp.int32)
    ea, eb, nv, dest, pad_rows, n_rows = _route(cls, T)
    xs = _dispatch(x1w, jnp.concatenate([dest, pad_rows]), n_rows)
    ys = _experts(xs, ea, eb, nv, w1, w3, w2, g2, b2, alpha)
    return _combine(ys, dest, T)


def _block_diag(w):
    n, d, _ = w.shape
    eye = jnp.eye(n, dtype=w.dtype)
    return jnp.einsum('nde,nm->ndme', w, eye).reshape(n * d, n * d)


def _layer(x, pos0, li, P, past):
    B, L, _ = x.shape
    T = B * L
    depth = P['depth']
    alpha = (2 * depth) ** 0.25
    w_in, b_in = P['w_in'][li], P['b_in'][li]

    def cols(lo, hi):
        return w_in[:, lo:hi], b_in[lo:hi].reshape(1, -1)

    lam_init = 0.8 - 0.6 * math.exp(-0.3 * li)
    dl = P['diff_lambda'][li]
    lam = jnp.exp(jnp.sum(dl[0] * dl[1])) - jnp.exp(jnp.sum(dl[2] * dl[3])) + lam_init
    lam_row = jnp.full((1, LANES), lam, F32)
    subln_g = P['diff_subln_g'][li].reshape(1, -1)

    wa, ba = cols(OFF_A, OFF_B)
    q, k, kb, v, vb, vt = _attn_proj(x, wa, ba, pos0)
    if past is None:
        y_a = _flash_prompt(q, kb, vt, lam_row, subln_g.reshape(-1, 1), 1.0 - lam_init)
        conv_buf = jnp.zeros((B, CONV_W - 1, B_WIDTH), F32)
        h0 = jnp.zeros((B, 1, B_WIDTH), F32)
        s0 = jnp.zeros((B, C_HEADS, C_DK, C_DV), F32)
    else:
        k_past, v_past, conv_buf, h0, s0 = past
        P_len = k_past.shape[1]
        y_a = _decode_attention(q, kb, vb, k_past.reshape(B, P_len, A_QK_WIDTH),
                                v_past.reshape(B, P_len, A_WIDTH), lam_row, subln_g, 1.0 - lam_init)
        h0 = h0.reshape(B, 1, B_WIDTH)

    wb, bb = cols(OFF_B, OFF_C)
    y_b, conv_new, h_new = _rglru(x, wb, bb, P['conv_w'][li], P['conv_b'][li].reshape(1, -1),
                                  P['lru_gate_w'][li], P['lru_gate_b'][li], P['lru_sp'][li].reshape(1, -1),
                                  conv_buf, h0, pos0)
    wc, bc = cols(OFF_C, OFF_G)
    y_c, s_new = _hgrn(x, wc, bc, P['lower_bounds'][li].reshape(1, -1), P['hgrn_norm_g'][li].reshape(1, -1), s0)

    wg, bg = cols(OFF_G, N_IN)
    x1w, cls_rows = _merge(x.reshape(T, D_MODEL), y_a.reshape(T, A_WIDTH), y_b.reshape(T, B_WIDTH),
                 y_c.reshape(T, C_WIDTH), wg, bg, P['w_out'][li], P['ln1_g'][li].reshape(1, -1),
                 P['ln1_b'][li].reshape(1, -1), P['router_w'], P['router_b'].reshape(1, -1), alpha)
    x2 = _moe(x1w, cls_rows, P['moe_w1'][li], P['moe_w3'][li], P['moe_w2'][li], P['ln2_g'][li].reshape(1, -1),
              P['ln2_b'][li].reshape(1, -1), alpha)
    states = (k.reshape(B, L, A_HEADS, 2, A_HEAD_DIM), v.reshape(B, L, A_HEADS, A_V_DIM),
              conv_new, h_new.reshape(B, B_WIDTH), s_new)
    return x2.reshape(B, L, D_MODEL), states


def kernel(x_prompt, x_sample, cache_k, cache_v, state_conv, state_lru, state_hgrn, ln_in_g, ln_in_b, w_in, b_in, diff_lambda, diff_subln_g, conv_w, conv_b, lru_wa, lru_ba, lru_wx, lru_bx, lru_lambda, hgrn_lb, hgrn_norm_g, w_out, ln1_g, ln1_b, router_w, router_b, moe_w1, moe_w3, moe_w2, ln2_g, ln2_b):
    depth = w_in.shape[0]
    p_lb = jax.nn.softmax(hgrn_lb.astype(F32), axis=0)
    P = dict(
        depth=depth,
        w_in=w_in.astype(BF16), b_in=b_in,
        diff_lambda=diff_lambda.astype(F32), diff_subln_g=diff_subln_g,
        conv_w=conv_w, conv_b=conv_b,
        lru_gate_w=jnp.stack([jnp.concatenate([_block_diag(lru_wa[l]), _block_diag(lru_wx[l])], axis=1)
                              for l in range(depth)]).astype(BF16),
        lru_gate_b=jnp.concatenate([lru_ba, lru_bx], axis=1).reshape(depth, 1, -1),
        lru_sp=jax.nn.softplus(-lru_lambda.astype(F32)),
        lower_bounds=jnp.cumsum(p_lb, axis=0) - p_lb[0],
        hgrn_norm_g=hgrn_norm_g,
        w_out=w_out.astype(BF16), ln1_g=ln1_g, ln1_b=ln1_b,
        router_w=router_w.astype(BF16), router_b=router_b,
        moe_w1=moe_w1.astype(BF16), moe_w3=moe_w3.astype(BF16), moe_w2=moe_w2.astype(BF16),
        ln2_g=ln2_g, ln2_b=ln2_b,
    )
    Bp, Lp, _ = x_prompt.shape
    Bs, Ls, _ = x_sample.shape
    past_len = cache_k.shape[2]
    xp = _input_ln(x_prompt.reshape(Bp * Lp, D_MODEL), ln_in_g, ln_in_b).reshape(Bp, Lp, D_MODEL)
    xs = _input_ln(x_sample.reshape(Bs * Ls, D_MODEL), ln_in_g, ln_in_b).reshape(Bs, Ls, D_MODEL)
    prompt_states, sample_states = [], []
    for li in range(depth):
        xp, st_p = _layer(xp, 0, li, P, None)
        xs, st_s = _layer(xs, past_len, li, P,
                          (cache_k[li], cache_v[li], state_conv[li], state_lru[li], state_hgrn[li]))
        prompt_states.append(st_p)
        sample_states.append(st_s)
    k_p, v_p, conv_p, lru_p, hgrn_p = [jnp.stack(t) for t in zip(*prompt_states)]
    k_s, v_s, conv_s, lru_s, hgrn_s = [jnp.stack(t) for t in zip(*sample_states)]
    return (xp, xs, k_p, v_p, conv_p, lru_p, hgrn_p, k_s, v_s, conv_s, lru_s, hgrn_s)
```

```python
import functools
import math

import numpy as np
import jax
import jax.numpy as jnp
from jax import lax
from jax.experimental import pallas as pl
from jax.experimental.pallas import tpu as pltpu

F32 = jnp.float32
BF16 = jnp.bfloat16

D_MODEL = 1024
CHUNK = 64
A_HEADS = 4
A_HEAD_DIM = 64
A_V_DIM = 2 * A_HEAD_DIM
A_QK_WIDTH = A_HEADS * 2 * A_HEAD_DIM
A_WIDTH = A_HEADS * A_V_DIM
ROPE_THETA = 10000.0
NEG_INF = -1e30
Q_SCALE = A_HEAD_DIM ** -0.5 * math.log2(math.e)
B_WIDTH = 512
B_BLOCKS = 8
CONV_W = 4
LRU_C = 8.0
C_HEADS = 4
C_DK = 128
C_DV = 128
C_WIDTH = C_HEADS * C_DV
F_TINY = 1e-30
N_BRANCH = 3
N_EXPERTS = 16
N_GROUPS = 4
EXPERTS_PER_GROUP = N_EXPERTS // N_GROUPS
D_EXPERT = 512
NORM_EPS = 1e-5

LANES = 128
SUBLANES = 8
PAIRS_PER_GROUP = 6
N_CLASSES = N_GROUPS * PAIRS_PER_GROUP
MOE_ROWS = 256
AUX_W = LANES
VMEM_LIMIT = 56 * 1024 * 1024

OFF_A = 0
OFF_B = OFF_A + 2 * A_QK_WIDTH + A_WIDTH
OFF_C = OFF_B + 2 * B_WIDTH
OFF_G = OFF_C + 4 * C_WIDTH
N_IN = OFF_G + N_BRANCH * D_MODEL


def _cparams(*sem):
    return pltpu.CompilerParams(dimension_semantics=sem, vmem_limit_bytes=VMEM_LIMIT)


def _dot(a, b):
    return jnp.dot(a, b, preferred_element_type=F32)


def _dot_nt(a, b):
    return lax.dot_general(a, b, (((1,), (1,)), ((), ())), preferred_element_type=F32)


def _dot_tn(a, b):
    return lax.dot_general(a, b, (((0,), (0,)), ((), ())), preferred_element_type=F32)


def _sigmoid(x):
    return 1.0 / (1.0 + jnp.exp(-x))


def _silu(x):
    return x * _sigmoid(x)


def _layer_norm(x, g, b):
    mu = jnp.mean(x, axis=-1, keepdims=True)
    xc = x - mu
    var = jnp.mean(xc * xc, axis=-1, keepdims=True)
    return xc * lax.rsqrt(var + NORM_EPS) * g + b


def _ln_kernel(x_ref, g_ref, b_ref, o_ref):
    o_ref[...] = _layer_norm(x_ref[...], g_ref[...], b_ref[...])


def _input_ln(x2d, g, b):
    T = x2d.shape[0]
    tm = min(T, 512)
    return pl.pallas_call(
        _ln_kernel,
        grid=(T // tm,),
        in_specs=[pl.BlockSpec((tm, D_MODEL), lambda i: (i, 0)),
                  pl.BlockSpec((1, D_MODEL), lambda i: (0, 0)),
                  pl.BlockSpec((1, D_MODEL), lambda i: (0, 0))],
        out_specs=pl.BlockSpec((tm, D_MODEL), lambda i: (i, 0)),
        out_shape=jax.ShapeDtypeStruct((T, D_MODEL), F32),
        compiler_params=_cparams("parallel"),
        name="input_ln",
    )(x2d, g.reshape(1, -1), b.reshape(1, -1))


def _attn_proj_kernel(x_ref, w_ref, b_ref, cos_ref, sin_ref, q_ref, k_ref, kb_ref, v_ref, vb_ref, vt_ref):
    x = x_ref[0].astype(BF16)
    y = _dot(x, w_ref[...]) + b_ref[...]
    cos = cos_ref[...]
    sin = sin_ref[...]
    lane = lax.broadcasted_iota(jnp.int32, cos.shape, 1)
    first_half = (lane % A_HEAD_DIM) < (A_HEAD_DIM // 2)
    for s in range(2 * A_QK_WIDTH // LANES):
        blk = y[:, s * LANES:(s + 1) * LANES]
        swapped = jnp.where(first_half,
                            pltpu.roll(blk, LANES - A_HEAD_DIM // 2, 1),
                            pltpu.roll(blk, A_HEAD_DIM // 2, 1))
        rot = blk * cos + swapped * sin
        if s < A_QK_WIDTH // LANES:
            q_ref[0, :, s * LANES:(s + 1) * LANES] = (rot * Q_SCALE).astype(BF16)
        else:
            o = s * LANES - A_QK_WIDTH
            k_ref[0, :, o:o + LANES] = rot
            kb_ref[0, :, o:o + LANES] = rot.astype(BF16)
    v = y[:, 2 * A_QK_WIDTH:]
    for h in range(A_HEADS):
        v_ref[0, :, h, :] = v[:, h * A_V_DIM:(h + 1) * A_V_DIM]
    vb_ref[0] = v.astype(BF16)
    vt_ref[0] = v.T.astype(BF16)


def _rope_tables(pos0, L):
    half = A_HEAD_DIM // 2
    inv_freq = ROPE_THETA ** (-jnp.arange(half, dtype=F32) / half)
    pos = (pos0 + jnp.arange(L)).astype(F32)
    ang = pos[:, None] * inv_freq[None, :]
    cos = jnp.cos(ang)
    sin = jnp.sin(ang)
    reps = LANES // A_HEAD_DIM
    cos_t = jnp.tile(jnp.concatenate([cos, cos], axis=1), (1, reps))
    sin_t = jnp.tile(jnp.concatenate([-sin, sin], axis=1), (1, reps))
    return cos_t, sin_t


def _attn_proj(x, w, b, pos0):
    B, L, _ = x.shape
    tl = min(L, 512)
    cos_t, sin_t = _rope_tables(pos0, L)
    wa = A_QK_WIDTH
    n = 2 * A_QK_WIDTH + A_WIDTH
    row = lambda bi, j: (bi, j, 0)
    return pl.pallas_call(
        _attn_proj_kernel,
        grid=(B, L // tl),
        in_specs=[pl.BlockSpec((1, tl, D_MODEL), row),
                  pl.BlockSpec((D_MODEL, n), lambda bi, j: (0, 0)),
                  pl.BlockSpec((1, n), lambda bi, j: (0, 0)),
                  pl.BlockSpec((tl, LANES), lambda bi, j: (j, 0)),
                  pl.BlockSpec((tl, LANES), lambda bi, j: (j, 0))],
        out_specs=[pl.BlockSpec((1, tl, wa), row)] * 3
        + [pl.BlockSpec((1, tl, A_HEADS, A_V_DIM), lambda bi, j: (bi, j, 0, 0)),
           pl.BlockSpec((1, tl, wa), row),
           pl.BlockSpec((1, wa, tl), lambda bi, j: (bi, 0, j))],
        out_shape=[jax.ShapeDtypeStruct((B, L, wa), BF16),
                   jax.ShapeDtypeStruct((B, L, wa), F32),
                   jax.ShapeDtypeStruct((B, L, wa), BF16),
                   jax.ShapeDtypeStruct((B, L, A_HEADS, A_V_DIM), F32),
                   jax.ShapeDtypeStruct((B, L, wa), BF16),
                   jax.ShapeDtypeStruct((B, wa, L), BF16)],
        compiler_params=_cparams("parallel", "parallel"),
        name="attn_proj",
    )(x, w, b, cos_t, sin_t)


def _split_maps(q):
    lane = lax.broadcasted_iota(jnp.int32, q.shape, 1)
    zero = jnp.zeros_like(q)
    return jnp.where(lane < A_HEAD_DIM, q, zero), jnp.where(lane >= A_HEAD_DIM, q, zero)


def _flash_kernel(lam_ref, g_ref, q_ref, k_ref, vt_ref, o_ref, m_s, l_s, acc_s, sa_s, sb_s, *, tq, out_scale):
    i = pl.program_id(2)
    qz = _split_maps(q_ref[0])
    qq = jnp.concatenate(qz, axis=0)
    m_s[...] = jnp.full(m_s.shape, NEG_INF, F32)
    l_s[...] = jnp.zeros(l_s.shape, F32)
    acc_s[...] = jnp.zeros(acc_s.shape, F32)

    def scores_into(dst, blk):
        off = pl.multiple_of(blk * tq, tq)
        dst[...] = _dot_nt(k_ref[0, pl.ds(off, tq), :], qq)

    def update(src, blk, diagonal):
        off = pl.multiple_of(blk * tq, tq)
        vt = vt_ref[0, :, pl.ds(off, tq)]
        s = src[...]
        if diagonal:
            key = lax.broadcasted_iota(jnp.int32, (tq, 2 * tq), 0) // CHUNK
            qry = (lax.broadcasted_iota(jnp.int32, (tq, 2 * tq), 1) % tq) // CHUNK
            s = jnp.where(key <= qry, s, NEG_INF)
        m_old = m_s[...]
        m_new = jnp.maximum(m_old, jnp.max(s, axis=0, keepdims=True))
        alpha = jnp.exp2(m_old - m_new)
        p = jnp.exp2(s - m_new)
        l_s[...] = alpha * l_s[...] + jnp.sum(p, axis=0, keepdims=True)
        acc_s[...] = alpha * acc_s[...] + _dot(vt, p.astype(BF16))
        m_s[...] = m_new

    scores_into(sa_s, 0)

    def pair(p, carry):
        scores_into(sb_s, 2 * p + 1)
        update(sa_s, 2 * p, False)
        scores_into(sa_s, 2 * p + 2)
        update(sb_s, 2 * p + 1, False)
        return carry

    lax.fori_loop(0, i // 2, pair, 0)

    @pl.when(i % 2 == 1)
    def _():
        scores_into(sb_s, i)
        update(sa_s, i - 1, False)
        update(sb_s, i, True)

    @pl.when(i % 2 == 0)
    def _():
        update(sa_s, i, True)

    inv_l = 1.0 / l_s[...]
    o = acc_s[...] * inv_l
    o = o[:, :tq] - lam_ref[0:1, 0:1] * o[:, tq:]
    ms = jnp.mean(o * o, axis=0, keepdims=True)
    y = o * lax.rsqrt(ms + NORM_EPS) * (g_ref[...] * out_scale)
    o_ref[0] = y.T.astype(o_ref.dtype)


def _flash_prompt(q, kb, vt, lam_row, g_col, out_scale):
    B, L, _ = q.shape
    tq = min(L, 512)
    return pl.pallas_call(
        functools.partial(_flash_kernel, tq=tq, out_scale=out_scale),
        grid=(B, A_HEADS, L // tq),
        in_specs=[pl.BlockSpec((1, LANES), lambda b, h, i: (0, 0)),
                  pl.BlockSpec((A_V_DIM, 1), lambda b, h, i: (0, 0)),
                  pl.BlockSpec((1, tq, LANES), lambda b, h, i: (b, i, h)),
                  pl.BlockSpec((1, L, LANES), lambda b, h, i: (b, 0, h)),
                  pl.BlockSpec((1, A_V_DIM, L), lambda b, h, i: (b, h, 0))],
        out_specs=pl.BlockSpec((1, tq, A_V_DIM), lambda b, h, i: (b, i, h)),
        out_shape=jax.ShapeDtypeStruct((B, L, A_WIDTH), BF16),
        scratch_shapes=[pltpu.VMEM((1, 2 * tq), F32), pltpu.VMEM((1, 2 * tq), F32),
                        pltpu.VMEM((A_V_DIM, 2 * tq), F32),
                        pltpu.VMEM((tq, 2 * tq), F32), pltpu.VMEM((tq, 2 * tq), F32)],
        compiler_params=_cparams("parallel", "parallel", "arbitrary"),
        name="flash_prompt",
    )(lam_row, g_col, q, kb, vt)


def _decode_kernel(lam_ref, g_ref, q_ref, kn_ref, vn_ref, kp_ref, vp_ref, o_ref, s_s, acc_s, m_s, r_s,
                   *, nb, tk, L, out_scale):
    j = pl.program_id(1)
    P = nb * tk
    n_new = kn_ref.shape[1]

    def q_rows():
        rep = jnp.concatenate([q_ref[0]] * (LANES // L), axis=0)
        grp = lax.broadcasted_iota(jnp.int32, rep.shape, 0) // L
        cols = lax.broadcasted_iota(jnp.int32, rep.shape, 1) // A_HEAD_DIM
        return jnp.where(grp == cols, rep, jnp.zeros_like(rep))

    @pl.when(j < nb)
    def _():
        off = pl.multiple_of(j * tk, tk)
        s_s[:, pl.ds(off, tk)] = _dot(q_rows(), kp_ref[0, 0].astype(BF16))

    @pl.when(j == nb - 1)
    def _():
        s_new = _dot_nt(q_rows(), kn_ref[0])
        valid = lax.broadcasted_iota(jnp.int32, s_new.shape, 1) < L
        s_s[:, P:P + n_new] = jnp.where(valid, s_new, NEG_INF)
        s = s_s[...]
        m = jnp.max(s, axis=1, keepdims=True)
        m_s[...] = m
        r_s[...] = 1.0 / jnp.sum(jnp.exp2(s - m), axis=1, keepdims=True)
        acc_s[...] = jnp.zeros(acc_s.shape, F32)

    def weights(cols):
        return (jnp.exp2(s_s[:, cols] - m_s[...]) * r_s[...]).astype(BF16)

    @pl.when(j >= nb)
    def _():
        off = pl.multiple_of((j - nb) * tk, tk)
        w = weights(pl.ds(off, tk))
        for h in range(A_HEADS):
            rows = slice(2 * h * L, (2 * h + 2) * L)
            acc_s[rows, :] += _dot(w[rows, :], vp_ref[0, 0, :, h, :].astype(BF16))

    @pl.when(j == 2 * nb - 1)
    def _():
        w = weights(slice(P, P + n_new))
        for h in range(A_HEADS):
            hs = slice(h * A_V_DIM, (h + 1) * A_V_DIM)
            rows = slice(2 * h * L, (2 * h + 2) * L)
            acc = acc_s[rows, :] + _dot(w[rows, :], vn_ref[0, :, hs])
            o = acc[:L] - lam_ref[...] * acc[L:]
            ms = jnp.mean(o * o, axis=-1, keepdims=True)
            o_ref[0, :, hs] = (o * lax.rsqrt(ms + NORM_EPS) * g_ref[...] * out_scale).astype(o_ref.dtype)


def _decode_attention(q, k_new, v_new, kt_cache, v_cache, li, lam_row, g_row, out_scale):
    B, L, _ = q.shape
    assert 2 * A_HEADS * L == LANES, "decode attention packs (head, map, frame) onto the 128 lanes"
    P = kt_cache.shape[3]
    tk = min(P, 1024)
    nb = P // tk
    pad = ((0, 0), (0, LANES - L), (0, 0))
    k_new = jnp.pad(k_new, pad)
    v_new = jnp.pad(v_new, pad)
    new = lambda b, j: (b, 0, 0)
    return pl.pallas_call(
        functools.partial(_decode_kernel, nb=nb, tk=tk, L=L, out_scale=out_scale),
        grid=(B, 2 * nb),
        in_specs=[pl.BlockSpec((1, LANES), lambda b, j: (0, 0)),
                  pl.BlockSpec((1, A_V_DIM), lambda b, j: (0, 0)),
                  pl.BlockSpec((1, L, A_QK_WIDTH), new),
                  pl.BlockSpec((1, LANES, A_QK_WIDTH), new),
                  pl.BlockSpec((1, LANES, A_WIDTH), new),
                  pl.BlockSpec((1, 1, A_QK_WIDTH, tk), lambda b, j: (li, b, 0, jnp.minimum(j, nb - 1))),
                  pl.BlockSpec((1, 1, tk, A_HEADS, A_V_DIM),
                               lambda b, j: (li, b, jnp.maximum(j - nb, 0), 0, 0))],
        out_specs=pl.BlockSpec((1, L, A_WIDTH), new),
        out_shape=jax.ShapeDtypeStruct((B, L, A_WIDTH), BF16),
        scratch_shapes=[pltpu.VMEM((LANES, P + LANES), F32), pltpu.VMEM((LANES, A_V_DIM), F32),
                        pltpu.VMEM((LANES, 1), F32), pltpu.VMEM((LANES, 1), F32)],
        compiler_params=_cparams("parallel", "arbitrary"),
        name="decode_attention",
    )(lam_row, g_row, q, k_new, v_new, kt_cache, v_cache)


def _gelu_tanh(x):
    return 0.5 * x * (1.0 + jnp.tanh(math.sqrt(2.0 / math.pi) * (x + 0.044715 * (x * x * x))))


def _rglru_kernel(x_ref, w_ref, b_ref, cw_ref, cb_ref, gw_ref, gbias_ref, sp_ref, cbuf_ref, h0_ref,
                  y_ref, conv_ref, h_ref, xs, hc, hs, *, tl, pos0):
    j = pl.program_id(1)
    tail = CONV_W - 1
    base = 8

    @pl.when(j == 0)
    def _():
        xs[0:base, :] = jnp.zeros((base, B_WIDTH), F32)
        xs[base - tail:base, :] = cbuf_ref[0]
        hc[...] = h0_ref[0]

    x = x_ref[0].astype(BF16)
    y = _dot(x, w_ref[...]) + b_ref[...]
    xb = y[:, :B_WIDTH]
    gb = y[:, B_WIDTH:]
    xs[base:base + tl, :] = xb
    xc = cb_ref[...] + cw_ref[0:1, :] * xs[base - tail:base - tail + tl, :]
    for t in range(1, CONV_W):
        xc = xc + cw_ref[t:t + 1, :] * xs[base - tail + t:base - tail + t + tl, :]
    new_tail = xs[base + tl - tail:base + tl, :]
    xs[base - tail:base, :] = new_tail
    conv_ref[0] = new_tail

    gates = _dot(xc.astype(BF16), gw_ref[...]) + gbias_ref[...]
    r = _sigmoid(gates[:, :B_WIDTH])
    ig = _sigmoid(gates[:, B_WIDTH:])
    log_a = -LRU_C * r * sp_ref[...]
    a = jnp.exp(log_a)
    t = jnp.tanh(log_a)
    mult = jnp.sqrt(-2.0 * t / (1.0 - t))
    row = lax.broadcasted_iota(jnp.int32, (tl, B_WIDTH), 0)
    mult = jnp.where(row + (pos0 + j * tl) == 0, 1.0, mult)
    bb = xc * ig * mult
    a3 = a.reshape(tl // SUBLANES, SUBLANES, B_WIDTH)
    b3 = bb.reshape(tl // SUBLANES, SUBLANES, B_WIDTH)
    sub = lax.broadcasted_iota(jnp.int32, a3.shape, 1)
    d = 1
    while d < SUBLANES:
        keep = sub >= d
        a_sh = pltpu.roll(a3, d, 1)
        b_sh = pltpu.roll(b3, d, 1)
        b3 = jnp.where(keep, a3 * b_sh + b3, b3)
        a3 = jnp.where(keep, a3 * a_sh, a3)
        d *= 2
    carry = hc[...]
    for g in range(tl // SUBLANES):
        hg = a3[g] * carry + b3[g]
        hs[g * SUBLANES:(g + 1) * SUBLANES, :] = hg
        carry = hg[SUBLANES - 1:SUBLANES, :]
    hc[...] = carry
    h_ref[0] = carry
    y_ref[0] = (_gelu_tanh(gb) * hs[...]).astype(y_ref.dtype)


def _rglru(x, w, b, conv_w, conv_b, gate_w, gate_b, sp, conv_buf, h0, pos0):
    B, L, _ = x.shape
    tl = min(L, 256)
    tail = CONV_W - 1
    const = lambda bi, j: (0, 0)
    per_b = lambda bi, j: (bi, 0, 0)
    return pl.pallas_call(
        functools.partial(_rglru_kernel, tl=tl, pos0=pos0),
        grid=(B, L // tl),
        in_specs=[pl.BlockSpec((1, tl, D_MODEL), lambda bi, j: (bi, j, 0)),
                  pl.BlockSpec((D_MODEL, 2 * B_WIDTH), const),
                  pl.BlockSpec((1, 2 * B_WIDTH), const),
                  pl.BlockSpec((CONV_W, B_WIDTH), const),
                  pl.BlockSpec((1, B_WIDTH), const),
                  pl.BlockSpec((B_WIDTH, 2 * B_WIDTH), const),
                  pl.BlockSpec((1, 2 * B_WIDTH), const),
                  pl.BlockSpec((1, B_WIDTH), const),
                  pl.BlockSpec((1, tail, B_WIDTH), per_b),
                  pl.BlockSpec((1, 1, B_WIDTH), per_b)],
        out_specs=[pl.BlockSpec((1, tl, B_WIDTH), lambda bi, j: (bi, j, 0)),
                   pl.BlockSpec((1, tail, B_WIDTH), per_b),
                   pl.BlockSpec((1, 1, B_WIDTH), per_b)],
        out_shape=[jax.ShapeDtypeStruct((B, L, B_WIDTH), BF16),
                   jax.ShapeDtypeStruct((B, tail, B_WIDTH), F32),
                   jax.ShapeDtypeStruct((B, 1, B_WIDTH), F32)],
        scratch_shapes=[pltpu.VMEM((8 + tl, B_WIDTH), F32), pltpu.VMEM((1, B_WIDTH), F32),
                        pltpu.VMEM((tl, B_WIDTH), F32)],
        compiler_params=_cparams("parallel", "arbitrary"),
        name="rglru",
    )(x, w, b, conv_w, conv_b, gate_w, gate_b, sp, conv_buf, h0)


def _hgrn_levels(C):
    return [C >> i for i in range(int(math.log2(C)) + 1)]


HGRN_SPLIT = 3


def _hgrn_sum_matrices(C):
    t = np.arange(C)

    def prefix(m):
        return (t[None, :] >= ((t // m) * m)[:, None]) & (t[None, :] <= t[:, None])

    def suffix(m):
        return (t[None, :] > t[:, None]) & (t[None, :] <= ((t // m) * m + m - 1)[:, None])

    mats = [prefix(C), suffix(C)]
    for m in _hgrn_levels(C)[1:-1]:
        upper = ((t // m) % 2 == 1)[:, None]
        mats.append(np.where(upper, prefix(m), suffix(m)))
    one = np.concatenate(mats, axis=0).astype(np.float32)
    return np.concatenate([one] * HGRN_SPLIT, axis=1)


def _hgrn_kernel(x_ref, w_ref, b_ref, lb_ref, ng_ref, dm_ref, s0_ref, y_ref, s_ref,
                 q_s, k_s, g_s, v_s, gate_s, o_s, st_s, *, tl, C):
    j = pl.program_id(1)
    levels = _hgrn_levels(C)
    nlev = len(levels)

    @pl.when(j == 0)
    def _():
        for h in range(C_HEADS):
            st_s[h] = s0_ref[0, h].T

    x = x_ref[0].astype(BF16)
    y = _dot(x, w_ref[...]) + b_ref[...]
    lb = lb_ref[...]
    z = y[:, C_WIDTH:2 * C_WIDTH]
    f = lb + (1.0 - lb) * _sigmoid(z)
    q_s[...] = _silu(y[:, :C_WIDTH])
    k_s[...] = (1.0 - lb) * _sigmoid(-z)
    g_s[...] = jnp.log(jnp.maximum(f, F_TINY))
    v_s[...] = y[:, 2 * C_WIDTH:3 * C_WIDTH]
    gate_s[...] = _silu(y[:, 3 * C_WIDTH:])

    ti = lax.broadcasted_iota(jnp.int32, (C, C), 0)
    si = lax.broadcasted_iota(jnp.int32, (C, C), 1)
    masks = [ti == si]
    for m in levels[1:]:
        masks.append((ti // (2 * m) == si // (2 * m)) & ((ti // m) % 2 == 1) & ((si // m) % 2 == 0))
    odd_row = lax.broadcasted_iota(jnp.int32, (C, C_WIDTH), 0) % 2 == 1

    def chunk(ci, carry):
        r0 = pl.multiple_of(ci * C, C)
        rows = pl.ds(r0, C)
        g = g_s[rows, :]
        pieces = []
        rest = g
        for _ in range(HGRN_SPLIT):
            piece = rest.astype(BF16)
            pieces.append(piece)
            rest = rest - piece.astype(F32)
        sums = _dot(dm_ref[...], jnp.concatenate(pieces, axis=0))
        e = jnp.exp(jnp.minimum(sums, 0.0))
        e_one = jnp.exp(jnp.where(odd_row, g, 0.0))
        for h in range(C_HEADS):
            hs = slice(h * C_DK, (h + 1) * C_DK)
            q = q_s[rows, hs]
            k = k_s[rows, hs]
            vb = v_s[rows, hs].astype(BF16)
            scores = jnp.where(masks[0], _dot_nt(q.astype(BF16), k.astype(BF16)), 0.0)
            for li in range(1, nlev):
                el = e[(li + 1) * C:(li + 2) * C, hs] if li < nlev - 1 else e_one[:, hs]
                scores = scores + jnp.where(masks[li], _dot_nt((q * el).astype(BF16), (k * el).astype(BF16)), 0.0)
            e_cum = e[0:C, hs]
            qc = (q * e_cum).astype(BF16)
            kc = (k * e[C:2 * C, hs]).astype(BF16)
            st = st_s[h]
            o = _dot(scores.astype(BF16), vb) + _dot_nt(qc, st.astype(BF16))
            st_s[h] = st * e_cum[C - 1:C, :] + _dot_tn(vb, kc)
            ms = jnp.mean(o * o, axis=-1, keepdims=True)
            o_s[rows, hs] = o * lax.rsqrt(ms + NORM_EPS) * ng_ref[...] * gate_s[rows, hs]
        return carry

    lax.fori_loop(0, tl // C, chunk, 0, unroll=True)
    y_ref[0] = o_s[...].astype(y_ref.dtype)
    for h in range(C_HEADS):
        s_ref[0, h] = st_s[h].T


def _hgrn(x, w, b, lb, norm_g, s0):
    B, L, _ = x.shape
    C = min(CHUNK, L)
    tl = min(L, 256)
    dm = jnp.asarray(_hgrn_sum_matrices(C), BF16)
    const = lambda bi, j: (0, 0)
    per_b = lambda bi, j: (bi, 0, 0, 0)
    wide = pltpu.VMEM((tl, C_WIDTH), F32)
    return pl.pallas_call(
        functools.partial(_hgrn_kernel, tl=tl, C=C),
        grid=(B, L // tl),
        in_specs=[pl.BlockSpec((1, tl, D_MODEL), lambda bi, j: (bi, j, 0)),
                  pl.BlockSpec((D_MODEL, 4 * C_WIDTH), const),
                  pl.BlockSpec((1, 4 * C_WIDTH), const),
                  pl.BlockSpec((1, C_WIDTH), const),
                  pl.BlockSpec((1, C_DV), const),
                  pl.BlockSpec(dm.shape, const),
                  pl.BlockSpec((1, C_HEADS, C_DK, C_DV), per_b)],
        out_specs=[pl.BlockSpec((1, tl, C_WIDTH), lambda bi, j: (bi, j, 0)),
                   pl.BlockSpec((1, C_HEADS, C_DK, C_DV), per_b)],
        out_shape=[jax.ShapeDtypeStruct((B, L, C_WIDTH), BF16),
                   jax.ShapeDtypeStruct((B, C_HEADS, C_DK, C_DV), F32)],
        scratch_shapes=[wide, wide, wide, wide, wide, wide, pltpu.VMEM((C_HEADS, C_DV, C_DK), F32)],
        compiler_params=_cparams("parallel", "arbitrary"),
        name="hgrn",
    )(x, w, b, lb, norm_g, dm, s0)


def _merge_kernel(x_ref, ya_ref, yb_ref, yc_ref, wg_ref, bg_ref, wo_ref, g1_ref, b1_ref, rw_ref, rb_ref,
                  o_ref, cls_ref, *, alpha):
    x = x_ref[...]
    xb = x.astype(BF16)
    mixed = None
    for n, y_ref in enumerate((ya_ref, yb_ref, yc_ref)):
        cols = slice(n * D_MODEL, (n + 1) * D_MODEL)
        gate = _sigmoid(_dot(xb, wg_ref[:, cols]) + bg_ref[:, cols])
        rows = slice(n * A_WIDTH, (n + 1) * A_WIDTH)
        term = gate * _dot(y_ref[...], wo_ref[rows, :])
        mixed = term if mixed is None else mixed + term
    x1 = _layer_norm(alpha * x + mixed, g1_ref[...], b1_ref[...])
    o_ref[:, :D_MODEL] = x1

    lg = _dot(x1.astype(BF16), rw_ref[...]) + rb_ref[...]
    col = [lg[:, e:e + 1] for e in range(N_EXPERTS)]
    gmax = [functools.reduce(jnp.maximum, col[g * EXPERTS_PER_GROUP:(g + 1) * EXPERTS_PER_GROUP])
            for g in range(N_GROUPS)]
    best = functools.reduce(jnp.maximum, gmax)
    gs = jnp.where(gmax[0] == best, 0, jnp.where(gmax[1] == best, 1, jnp.where(gmax[2] == best, 2, 3)))
    v = []
    for e in range(EXPERTS_PER_GROUP):
        v.append(jnp.where(gs == 0, col[e],
                           jnp.where(gs == 1, col[EXPERTS_PER_GROUP + e],
                                     jnp.where(gs == 2, col[2 * EXPERTS_PER_GROUP + e],
                                               col[3 * EXPERTS_PER_GROUP + e]))))
    m1 = best
    i1 = jnp.where(v[0] == m1, 0, jnp.where(v[1] == m1, 1, jnp.where(v[2] == m1, 2, 3)))
    v2 = [jnp.where(i1 == e, -jnp.inf, v[e]) for e in range(EXPERTS_PER_GROUP)]
    m2 = functools.reduce(jnp.maximum, v2)
    i2 = jnp.where(v2[0] == m2, 0, jnp.where(v2[1] == m2, 1, jnp.where(v2[2] == m2, 2, 3)))
    e2 = jnp.exp(m2 - m1)
    p1 = 1.0 / (1.0 + e2)
    p2 = e2 / (1.0 + e2)
    ea = jnp.minimum(i1, i2)
    eb = jnp.maximum(i1, i2)
    ga = jnp.where(i1 < i2, p1, p2)
    gb = jnp.where(i1 < i2, p2, p1)
    pair = jnp.where(ea == 0, 0, jnp.where(ea == 1, 3, 5)) + (eb - ea - 1)
    cls = (gs * PAIRS_PER_GROUP + pair).astype(F32)
    lane = lax.broadcasted_iota(jnp.int32, (x.shape[0], AUX_W), 1)
    aux = jnp.where(lane == 0, ga, jnp.where(lane == 1, gb, jnp.where(lane == 2, cls, 0.0)))
    o_ref[:, D_MODEL:] = aux
    pick = (lax.broadcasted_iota(jnp.int32, (8, AUX_W), 0) == 0) & (lax.broadcasted_iota(jnp.int32, (8, AUX_W), 1) == 2)
    cls_ref[...] = _dot_nt(jnp.where(pick, 1.0, 0.0).astype(BF16), aux.astype(BF16))


def _merge(x2d, ya, yb, yc, wg, bg, wo, g1, b1, rw, rb, alpha):
    T = x2d.shape[0]
    tm = min(T, 512)
    row = lambda i: (i, 0)
    const = lambda i: (0, 0)
    return pl.pallas_call(
        functools.partial(_merge_kernel, alpha=alpha),
        grid=(T // tm,),
        in_specs=[pl.BlockSpec((tm, D_MODEL), row),
                  pl.BlockSpec((tm, A_WIDTH), row),
                  pl.BlockSpec((tm, B_WIDTH), row),
                  pl.BlockSpec((tm, C_WIDTH), row),
                  pl.BlockSpec((D_MODEL, N_BRANCH * D_MODEL), const),
                  pl.BlockSpec((1, N_BRANCH * D_MODEL), const),
                  pl.BlockSpec((N_BRANCH * A_WIDTH, D_MODEL), const),
                  pl.BlockSpec((1, D_MODEL), const),
                  pl.BlockSpec((1, D_MODEL), const),
                  pl.BlockSpec((D_MODEL, N_EXPERTS), const),
                  pl.BlockSpec((1, N_EXPERTS), const)],
        out_specs=[pl.BlockSpec((tm, D_MODEL + AUX_W), row), pl.BlockSpec((8, tm), lambda i: (0, i))],
        out_shape=[jax.ShapeDtypeStruct((T, D_MODEL + AUX_W), F32), jax.ShapeDtypeStruct((8, T), F32)],
        compiler_params=_cparams("parallel"),
        name="merge",
    )(x2d, ya, yb, yc, wg, bg, wo, g1, b1, rw, rb)


def _dispatch_kernel(dest_ref, x_ref, out_hbm, zbuf, sem, *, tm, n_tok_steps):
    i = pl.program_id(0)

    def scatter_rows(src):
        def start(r, c):
            pltpu.make_async_copy(src.at[pl.ds(r, 1), :], out_hbm.at[pl.ds(dest_ref[0, 0, r], 1), :], sem).start()
            return c

        lax.fori_loop(0, tm, start, 0, unroll=8)
        pltpu.make_async_copy(src, out_hbm.at[pl.ds(0, tm), :], sem).wait()

    @pl.when(i < n_tok_steps)
    def _():
        scatter_rows(x_ref)

    @pl.when(i >= n_tok_steps)
    def _():
        @pl.when(i == n_tok_steps)
        def _():
            zbuf[...] = jnp.zeros(zbuf.shape, F32)

        scatter_rows(zbuf)


def _dispatch(x1w, dest_all, n_rows):
    T, W = x1w.shape
    tm = min(T, 512)
    n_tok_steps = T // tm
    steps = dest_all.shape[0] // tm
    return pl.pallas_call(
        functools.partial(_dispatch_kernel, tm=tm, n_tok_steps=n_tok_steps),
        grid=(steps,),
        in_specs=[pl.BlockSpec((1, 1, tm), lambda i: (i, 0, 0), memory_space=pltpu.SMEM),
                  pl.BlockSpec((tm, W), lambda i: (jnp.minimum(i, n_tok_steps - 1), 0))],
        out_specs=pl.BlockSpec(memory_space=pl.ANY),
        out_shape=jax.ShapeDtypeStruct((n_rows, W), F32),
        scratch_shapes=[pltpu.VMEM((tm, W), F32), pltpu.SemaphoreType.DMA(())],
        compiler_params=_cparams("arbitrary"),
        name="moe_dispatch",
    )(dest_all.reshape(steps, 1, tm), x1w)


def _experts_kernel(ea_ref, eb_ref, nv_ref, x_ref, w1a, w3a, w2a, w1b, w3b, w2b, g2_ref, b2_ref, o_ref, *, alpha):
    i = pl.program_id(0)

    @pl.when(nv_ref[i] > 0)
    def _():
        xw = x_ref[...]
        x = xw[:, :D_MODEL]
        xb = x.astype(BF16)
        y = None
        for n, (w1, w3, w2) in enumerate(((w1a, w3a, w2a), (w1b, w3b, w2b))):
            hmid = _silu(_dot(xb, w1[0])) * _dot(xb, w3[0])
            ye = xw[:, D_MODEL + n:D_MODEL + n + 1] * _dot(hmid.astype(BF16), w2[0])
            y = ye if y is None else y + ye
        o_ref[...] = _layer_norm(alpha * x + y, g2_ref[...], b2_ref[...])

    @pl.when(nv_ref[i] == 0)
    def _():
        o_ref[...] = jnp.zeros(o_ref.shape, F32)


def _experts(xs, ea, eb, nv, w1, w3, w2, g2, b2, alpha):
    n_rows, W = xs.shape
    wa = lambda i, ea, eb, nv: (ea[i], 0, 0)
    wb = lambda i, ea, eb, nv: (eb[i], 0, 0)
    const = lambda i, ea, eb, nv: (0, 0)
    row = lambda i, ea, eb, nv: (i, 0)
    up = (1, D_MODEL, D_EXPERT)
    down = (1, D_EXPERT, D_MODEL)
    grid_spec = pltpu.PrefetchScalarGridSpec(
        num_scalar_prefetch=3,
        grid=(n_rows // MOE_ROWS,),
        in_specs=[pl.BlockSpec((MOE_ROWS, W), row),
                  pl.BlockSpec(up, wa), pl.BlockSpec(up, wa), pl.BlockSpec(down, wa),
                  pl.BlockSpec(up, wb), pl.BlockSpec(up, wb), pl.BlockSpec(down, wb),
                  pl.BlockSpec((1, D_MODEL), const), pl.BlockSpec((1, D_MODEL), const)],
        out_specs=pl.BlockSpec((MOE_ROWS, D_MODEL), row),
    )
    return pl.pallas_call(
        functools.partial(_experts_kernel, alpha=alpha),
        grid_spec=grid_spec,
        out_shape=jax.ShapeDtypeStruct((n_rows, D_MODEL), F32),
        compiler_params=_cparams("arbitrary"),
        name="moe_experts",
    )(ea, eb, nv, xs, w1, w3, w2, w1, w3, w2, g2, b2)


def _combine_kernel(dest_ref, y_hbm, o_ref, sem, *, tm):
    def start(r, c):
        pltpu.make_async_copy(y_hbm.at[pl.ds(dest_ref[0, 0, r], 1), :], o_ref.at[pl.ds(r, 1), :], sem).start()
        return c

    lax.fori_loop(0, tm, start, 0, unroll=8)
    pltpu.make_async_copy(y_hbm.at[pl.ds(0, tm), :], o_ref, sem).wait()


def _combine(ys, dest, T):
    tm = min(T, 512)
    return pl.pallas_call(
        functools.partial(_combine_kernel, tm=tm),
        grid=(T // tm,),
        in_specs=[pl.BlockSpec((1, 1, tm), lambda i: (i, 0, 0), memory_space=pltpu.SMEM),
                  pl.BlockSpec(memory_space=pl.ANY)],
        out_specs=pl.BlockSpec((tm, D_MODEL), lambda i: (i, 0)),
        out_shape=jax.ShapeDtypeStruct((T, D_MODEL), F32),
        scratch_shapes=[pltpu.SemaphoreType.DMA(())],
        compiler_params=_cparams("arbitrary"),
        name="moe_combine",
    )(dest.reshape(T // tm, 1, tm), ys)


def _route(cls, T):
    n_blocks = T // MOE_ROWS + N_CLASSES
    n_rows = n_blocks * MOE_ROWS
    rb = math.gcd(T, MOE_ROWS)
    onehot = (cls[:, None] == jnp.arange(N_CLASSES, dtype=jnp.int32)[None, :]).astype(F32).reshape(T // rb, rb, N_CLASSES)
    strict = jnp.asarray(np.tril(np.ones((rb, rb), np.float32), -1))
    rank_in = jnp.einsum('ij,bjc->bic', strict, onehot)
    blk_counts = jnp.sum(onehot, axis=1)
    blk_off = jnp.cumsum(blk_counts, axis=0) - blk_counts
    rank = jnp.sum(onehot * (rank_in + blk_off[:, None, :]), axis=-1).reshape(T).astype(jnp.int32)
    counts = jnp.sum(blk_counts, axis=0).astype(jnp.int32)
    padded = (counts + MOE_ROWS - 1) // MOE_ROWS * MOE_ROWS
    pad_end = jnp.cumsum(padded)
    pad_start = pad_end - padded
    dest = (pad_start[cls] + rank).astype(jnp.int32)
    seg_start = jnp.concatenate([pad_start + counts, pad_end[-1:]])
    seg_len = jnp.concatenate([padded - counts, n_rows - pad_end[-1:]])
    seg_end = jnp.cumsum(seg_len)
    kk = jnp.arange(n_rows - T, dtype=jnp.int32)
    seg = jnp.sum((seg_end[None, :] <= kk[:, None]).astype(jnp.int32), axis=1)
    pad_rows = (seg_start[seg] + kk - (seg_end[seg] - seg_len[seg])).astype(jnp.int32)
    blk_start = jnp.arange(n_blocks, dtype=jnp.int32) * MOE_ROWS
    blk_cls = jnp.minimum(jnp.sum((pad_end[None, :] <= blk_start[:, None]).astype(jnp.int32), axis=1), N_CLASSES - 1)
    nv = jnp.clip(pad_start[blk_cls] + counts[blk_cls] - blk_start, 0, MOE_ROWS).astype(jnp.int32)
    grp = blk_cls // PAIRS_PER_GROUP
    pair = blk_cls % PAIRS_PER_GROUP
    pa = jnp.asarray([0, 0, 0, 1, 1, 2], jnp.int32)[pair]
    pb = jnp.asarray([1, 2, 3, 2, 3, 3], jnp.int32)[pair]
    ea = grp * EXPERTS_PER_GROUP + pa
    eb = grp * EXPERTS_PER_GROUP + pb
    return ea, eb, nv, dest, pad_rows, n_rows


def _moe(x1w, cls_rows, w1, w3, w2, g2, b2, alpha):
    T = x1w.shape[0]
    cls = cls_rows[0].astype(jnp.int32)
    ea, eb, nv, dest, pad_rows, n_rows = _route(cls, T)
    xs = _dispatch(x1w, jnp.concatenate([dest, pad_rows]), n_rows)
    ys = _experts(xs, ea, eb, nv, w1, w3, w2, g2, b2, alpha)
    return _combine(ys, dest, T)


def _block_diag(w):
    n, d, _ = w.shape
    eye = jnp.eye(n, dtype=w.dtype)
    return jnp.einsum('nde,nm->ndme', w, eye).reshape(n * d, n * d)


def _layer(x, pos0, li, P, past):
    B, L, _ = x.shape
    T = B * L
    depth = P['depth']
    alpha = (2 * depth) ** 0.25
    w_in, b_in = P['w_in'][li], P['b_in'][li]

    def cols(lo, hi):
        return w_in[:, lo:hi], b_in[lo:hi].reshape(1, -1)

    lam_init = 0.8 - 0.6 * math.exp(-0.3 * li)
    dl = P['diff_lambda'][li]
    lam = jnp.exp(jnp.sum(dl[0] * dl[1])) - jnp.exp(jnp.sum(dl[2] * dl[3])) + lam_init
    lam_row = jnp.full((1, LANES), lam, F32)
    subln_g = P['diff_subln_g'][li].reshape(1, -1)

    wa, ba = cols(OFF_A, OFF_B)
    q, k, kb, v, vb, vt = _attn_proj(x, wa, ba, pos0)
    if past is None:
        y_a = _flash_prompt(q, kb, vt, lam_row, subln_g.reshape(-1, 1), 1.0 - lam_init)
        conv_buf = jnp.zeros((B, CONV_W - 1, B_WIDTH), F32)
        h0 = jnp.zeros((B, 1, B_WIDTH), F32)
        s0 = jnp.zeros((B, C_HEADS, C_DK, C_DV), F32)
    else:
        kt_cache, v_cache, conv_buf, h0, s0 = past
        y_a = _decode_attention(q, kb, vb, kt_cache, v_cache, li, lam_row, subln_g, 1.0 - lam_init)
        h0 = h0.reshape(B, 1, B_WIDTH)

    wb, bb = cols(OFF_B, OFF_C)
    y_b, conv_new, h_new = _rglru(x, wb, bb, P['conv_w'][li], P['conv_b'][li].reshape(1, -1),
                                  P['lru_gate_w'][li], P['lru_gate_b'][li], P['lru_sp'][li].reshape(1, -1),
                                  conv_buf, h0, pos0)
    wc, bc = cols(OFF_C, OFF_G)
    y_c, s_new = _hgrn(x, wc, bc, P['lower_bounds'][li].reshape(1, -1), P['hgrn_norm_g'][li].reshape(1, -1), s0)

    wg, bg = cols(OFF_G, N_IN)
    x1w, cls_rows = _merge(x.reshape(T, D_MODEL), y_a.reshape(T, A_WIDTH), y_b.reshape(T, B_WIDTH),
                 y_c.reshape(T, C_WIDTH), wg, bg, P['w_out'][li], P['ln1_g'][li].reshape(1, -1),
                 P['ln1_b'][li].reshape(1, -1), P['router_w'], P['router_b'].reshape(1, -1), alpha)
    x2 = _moe(x1w, cls_rows, P['moe_w1'][li], P['moe_w3'][li], P['moe_w2'][li], P['ln2_g'][li].reshape(1, -1),
              P['ln2_b'][li].reshape(1, -1), alpha)
    states = (k.reshape(B, L, A_HEADS, 2, A_HEAD_DIM), v,
              conv_new, h_new.reshape(B, B_WIDTH), s_new)
    return x2.reshape(B, L, D_MODEL), states


def kernel(x_prompt, x_sample, cache_k, cache_v, state_conv, state_lru, state_hgrn, ln_in_g, ln_in_b, w_in, b_in, diff_lambda, diff_subln_g, conv_w, conv_b, lru_wa, lru_ba, lru_wx, lru_bx, lru_lambda, hgrn_lb, hgrn_norm_g, w_out, ln1_g, ln1_b, router_w, router_b, moe_w1, moe_w3, moe_w2, ln2_g, ln2_b):
    depth = w_in.shape[0]
    p_lb = jax.nn.softmax(hgrn_lb.astype(F32), axis=0)
    P = dict(
        depth=depth,
        w_in=w_in.astype(BF16), b_in=b_in,
        diff_lambda=diff_lambda.astype(F32), diff_subln_g=diff_subln_g,
        conv_w=conv_w, conv_b=conv_b,
        lru_gate_w=jnp.stack([jnp.concatenate([_block_diag(lru_wa[l]), _block_diag(lru_wx[l])], axis=1)
                              for l in range(depth)]).astype(BF16),
        lru_gate_b=jnp.concatenate([lru_ba, lru_bx], axis=1).reshape(depth, 1, -1),
        lru_sp=jax.nn.softplus(-lru_lambda.astype(F32)),
        lower_bounds=jnp.cumsum(p_lb, axis=0) - p_lb[0],
        hgrn_norm_g=hgrn_norm_g,
        w_out=w_out.astype(BF16), ln1_g=ln1_g, ln1_b=ln1_b,
        router_w=router_w.astype(BF16), router_b=router_b,
        moe_w1=moe_w1.astype(BF16), moe_w3=moe_w3.astype(BF16), moe_w2=moe_w2.astype(BF16),
        ln2_g=ln2_g, ln2_b=ln2_b,
    )
    Bp, Lp, _ = x_prompt.shape
    Bs, Ls, _ = x_sample.shape
    past_len = cache_k.shape[2]
    kt_cache = jnp.transpose(cache_k, (0, 1, 3, 4, 5, 2)).reshape(depth, Bs, A_QK_WIDTH, past_len)
    v_cache = cache_v
    xp = _input_ln(x_prompt.reshape(Bp * Lp, D_MODEL), ln_in_g, ln_in_b).reshape(Bp, Lp, D_MODEL)
    xs = _input_ln(x_sample.reshape(Bs * Ls, D_MODEL), ln_in_g, ln_in_b).reshape(Bs, Ls, D_MODEL)
    prompt_states, sample_states = [], []
    for li in range(depth):
        xp, st_p = _layer(xp, 0, li, P, None)
        xs, st_s = _layer(xs, past_len, li, P, (kt_cache, v_cache, state_conv[li], state_lru[li], state_hgrn[li]))
        prompt_states.append(st_p)
        sample_states.append(st_s)
    k_p, v_p, conv_p, lru_p, hgrn_p = [jnp.stack(t) for t in zip(*prompt_states)]
    k_s, v_s, conv_s, lru_s, hgrn_s = [jnp.stack(t) for t in zip(*sample_states)]
    return (xp, xs, k_p, v_p, conv_p, lru_p, hgrn_p, k_s, v_s, conv_s, lru_s, hgrn_s)
```

```python
import functools
import math

import numpy as np
import jax
import jax.numpy as jnp
from jax import lax
from jax.experimental import pallas as pl
from jax.experimental.pallas import tpu as pltpu

F32 = jnp.float32
BF16 = jnp.bfloat16

D_MODEL = 1024
CHUNK = 64
A_HEADS = 4
A_HEAD_DIM = 64
A_V_DIM = 2 * A_HEAD_DIM
A_QK_WIDTH = A_HEADS * 2 * A_HEAD_DIM
A_WIDTH = A_HEADS * A_V_DIM
ROPE_THETA = 10000.0
NEG_INF = -1e30
Q_SCALE = A_HEAD_DIM ** -0.5 * math.log2(math.e)
B_WIDTH = 512
B_BLOCKS = 8
CONV_W = 4
LRU_C = 8.0
C_HEADS = 4
C_DK = 128
C_DV = 128
C_WIDTH = C_HEADS * C_DV
F_TINY = 1e-30
N_BRANCH = 3
N_EXPERTS = 16
N_GROUPS = 4
EXPERTS_PER_GROUP = N_EXPERTS // N_GROUPS
D_EXPERT = 512
NORM_EPS = 1e-5

LANES = 128
SUBLANES = 8
PAIRS_PER_GROUP = 6
N_CLASSES = N_GROUPS * PAIRS_PER_GROUP
MOE_ROWS = 256
MERGE_COLS = 256
AUX_W = LANES
VMEM_LIMIT = 56 * 1024 * 1024

OFF_A = 0
OFF_B = OFF_A + 2 * A_QK_WIDTH + A_WIDTH
OFF_C = OFF_B + 2 * B_WIDTH
OFF_G = OFF_C + 4 * C_WIDTH
N_IN = OFF_G + N_BRANCH * D_MODEL


def _cparams(*sem):
    return pltpu.CompilerParams(dimension_semantics=sem, vmem_limit_bytes=VMEM_LIMIT)


def _dot(a, b):
    return jnp.dot(a, b, preferred_element_type=F32)


def _dot_nt(a, b):
    return lax.dot_general(a, b, (((1,), (1,)), ((), ())), preferred_element_type=F32)


def _dot_tn(a, b):
    return lax.dot_general(a, b, (((0,), (0,)), ((), ())), preferred_element_type=F32)


def _sigmoid(x):
    return 1.0 / (1.0 + jnp.exp(-x))


def _silu(x):
    return x * _sigmoid(x)


def _layer_norm(x, g, b):
    mu = jnp.mean(x, axis=-1, keepdims=True)
    xc = x - mu
    var = jnp.mean(xc * xc, axis=-1, keepdims=True)
    return xc * lax.rsqrt(var + NORM_EPS) * g + b


def _ln_kernel(x_ref, g_ref, b_ref, o_ref):
    o_ref[...] = _layer_norm(x_ref[...], g_ref[...], b_ref[...])


def _input_ln(x2d, g, b):
    T = x2d.shape[0]
    tm = min(T, 512)
    return pl.pallas_call(
        _ln_kernel,
        grid=(T // tm,),
        in_specs=[pl.BlockSpec((tm, D_MODEL), lambda i: (i, 0)),
                  pl.BlockSpec((1, D_MODEL), lambda i: (0, 0)),
                  pl.BlockSpec((1, D_MODEL), lambda i: (0, 0))],
        out_specs=pl.BlockSpec((tm, D_MODEL), lambda i: (i, 0)),
        out_shape=jax.ShapeDtypeStruct((T, D_MODEL), F32),
        compiler_params=_cparams("parallel"),
        name="input_ln",
    )(x2d, g.reshape(1, -1), b.reshape(1, -1))


def _attn_proj_kernel(x_ref, w_ref, b_ref, cos_ref, sin_ref, *refs):
    q_ref, k_ref, kb_ref, v_ref, vb_ref, vt_ref = refs[-6:]
    x = x_ref[0].astype(BF16)
    y = _dot(x, w_ref[...]) + b_ref[...]
    cos = cos_ref[...]
    sin = sin_ref[...]
    lane = lax.broadcasted_iota(jnp.int32, cos.shape, 1)
    first_half = (lane % A_HEAD_DIM) < (A_HEAD_DIM // 2)
    for s in range(2 * A_QK_WIDTH // LANES):
        blk = y[:, s * LANES:(s + 1) * LANES]
        swapped = jnp.where(first_half,
                            pltpu.roll(blk, LANES - A_HEAD_DIM // 2, 1),
                            pltpu.roll(blk, A_HEAD_DIM // 2, 1))
        rot = blk * cos + swapped * sin
        if s < A_QK_WIDTH // LANES:
            q_ref[0, :, s * LANES:(s + 1) * LANES] = (rot * Q_SCALE).astype(BF16)
        else:
            o = s * LANES - A_QK_WIDTH
            k_ref[0, 0, :, o:o + LANES] = rot
            kb_ref[0, :, o:o + LANES] = rot.astype(BF16)
    v = y[:, 2 * A_QK_WIDTH:]
    for h in range(A_HEADS):
        v_ref[0, 0, pl.ds(h, v.shape[0], stride=A_HEADS), :] = v[:, h * A_V_DIM:(h + 1) * A_V_DIM]
    vb_ref[0] = v.astype(BF16)
    vt_ref[0] = v.T.astype(BF16)
    for later in range(1, k_ref.shape[0]):
        k_ref[later] = jnp.zeros(k_ref.shape[1:], F32)
        v_ref[later] = jnp.zeros(v_ref.shape[1:], F32)


def _rope_tables(pos0, L):
    half = A_HEAD_DIM // 2
    inv_freq = ROPE_THETA ** (-jnp.arange(half, dtype=F32) / half)
    pos = (pos0 + jnp.arange(L)).astype(F32)
    ang = pos[:, None] * inv_freq[None, :]
    cos = jnp.cos(ang)
    sin = jnp.sin(ang)
    reps = LANES // A_HEAD_DIM
    cos_t = jnp.tile(jnp.concatenate([cos, cos], axis=1), (1, reps))
    sin_t = jnp.tile(jnp.concatenate([-sin, sin], axis=1), (1, reps))
    return cos_t, sin_t


def _attn_proj(x, w, b, pos0, li, depth, carried):
    B, L, _ = x.shape
    tl = min(L, 512)
    cos_t, sin_t = _rope_tables(pos0, L)
    wa = A_QK_WIDTH
    n = 2 * A_QK_WIDTH + A_WIDTH
    row = lambda bi, j: (bi, j, 0)
    layer_row = lambda bi, j: (li, bi, j, 0)
    carried = () if carried is None else tuple(carried)
    n_layers = 1 if carried else depth
    assert carried or li == 0
    n_in = 5
    return pl.pallas_call(
        _attn_proj_kernel,
        grid=(B, L // tl),
        in_specs=[pl.BlockSpec((1, tl, D_MODEL), row),
                  pl.BlockSpec((D_MODEL, n), lambda bi, j: (0, 0)),
                  pl.BlockSpec((1, n), lambda bi, j: (0, 0)),
                  pl.BlockSpec((tl, LANES), lambda bi, j: (j, 0)),
                  pl.BlockSpec((tl, LANES), lambda bi, j: (j, 0))]
        + [pl.BlockSpec(memory_space=pl.ANY)] * len(carried),
        out_specs=[pl.BlockSpec((1, tl, wa), row),
                   pl.BlockSpec((n_layers, 1, tl, wa), layer_row),
                   pl.BlockSpec((1, tl, wa), row),
                   pl.BlockSpec((n_layers, 1, tl * A_HEADS, A_V_DIM), layer_row),
                   pl.BlockSpec((1, tl, wa), row),
                   pl.BlockSpec((1, wa, tl), lambda bi, j: (bi, 0, j))],
        out_shape=[jax.ShapeDtypeStruct((B, L, wa), BF16),
                   jax.ShapeDtypeStruct((depth, B, L, wa), F32),
                   jax.ShapeDtypeStruct((B, L, wa), BF16),
                   jax.ShapeDtypeStruct((depth, B, L * A_HEADS, A_V_DIM), F32),
                   jax.ShapeDtypeStruct((B, L, wa), BF16),
                   jax.ShapeDtypeStruct((B, wa, L), BF16)],
        input_output_aliases={n_in: 1, n_in + 1: 3} if carried else {},
        compiler_params=_cparams("parallel", "parallel"),
        name="attn_proj",
    )(x, w, b, cos_t, sin_t, *carried)


def _split_maps(q):
    lane = lax.broadcasted_iota(jnp.int32, q.shape, 1)
    zero = jnp.zeros_like(q)
    return jnp.where(lane < A_HEAD_DIM, q, zero), jnp.where(lane >= A_HEAD_DIM, q, zero)


def _flash_kernel(lam_ref, g_ref, q_ref, k_ref, vt_ref, o_ref, m_s, l_s, acc_s, sa_s, sb_s, *, tq, out_scale):
    i = pl.program_id(2)
    qz = _split_maps(q_ref[0])
    qq = jnp.concatenate(qz, axis=0)
    m_s[...] = jnp.full(m_s.shape, NEG_INF, F32)
    l_s[...] = jnp.zeros(l_s.shape, F32)
    acc_s[...] = jnp.zeros(acc_s.shape, F32)

    def scores_into(dst, blk):
        off = pl.multiple_of(blk * tq, tq)
        dst[...] = _dot_nt(k_ref[0, pl.ds(off, tq), :], qq)

    def update(src, blk, diagonal):
        off = pl.multiple_of(blk * tq, tq)
        vt = vt_ref[0, :, pl.ds(off, tq)]
        s = src[...]
        if diagonal:
            key = lax.broadcasted_iota(jnp.int32, (tq, 2 * tq), 0) // CHUNK
            qry = (lax.broadcasted_iota(jnp.int32, (tq, 2 * tq), 1) % tq) // CHUNK
            s = jnp.where(key <= qry, s, NEG_INF)
        m_old = m_s[...]
        m_new = jnp.maximum(m_old, jnp.max(s, axis=0, keepdims=True))
        alpha = jnp.exp2(m_old - m_new)
        p = jnp.exp2(s - m_new)
        l_s[...] = alpha * l_s[...] + jnp.sum(p, axis=0, keepdims=True)
        acc_s[...] = alpha * acc_s[...] + _dot(vt, p.astype(BF16))
        m_s[...] = m_new

    scores_into(sa_s, 0)

    def pair(p, carry):
        scores_into(sb_s, 2 * p + 1)
        update(sa_s, 2 * p, False)
        scores_into(sa_s, 2 * p + 2)
        update(sb_s, 2 * p + 1, False)
        return carry

    lax.fori_loop(0, i // 2, pair, 0)

    @pl.when(i % 2 == 1)
    def _():
        scores_into(sb_s, i)
        update(sa_s, i - 1, False)
        update(sb_s, i, True)

    @pl.when(i % 2 == 0)
    def _():
        update(sa_s, i, True)

    inv_l = 1.0 / l_s[...]
    o = acc_s[...] * inv_l
    o = o[:, :tq] - lam_ref[0:1, 0:1] * o[:, tq:]
    ms = jnp.mean(o * o, axis=0, keepdims=True)
    y = o * lax.rsqrt(ms + NORM_EPS) * (g_ref[...] * out_scale)
    o_ref[0] = y.T.astype(o_ref.dtype)


def _flash_prompt(q, kb, vt, lam_row, g_col, out_scale):
    B, L, _ = q.shape
    tq = min(L, 512)
    return pl.pallas_call(
        functools.partial(_flash_kernel, tq=tq, out_scale=out_scale),
        grid=(B, A_HEADS, L // tq),
        in_specs=[pl.BlockSpec((1, LANES), lambda b, h, i: (0, 0)),
                  pl.BlockSpec((A_V_DIM, 1), lambda b, h, i: (0, 0)),
                  pl.BlockSpec((1, tq, LANES), lambda b, h, i: (b, i, h)),
                  pl.BlockSpec((1, L, LANES), lambda b, h, i: (b, 0, h)),
                  pl.BlockSpec((1, A_V_DIM, L), lambda b, h, i: (b, h, 0))],
        out_specs=pl.BlockSpec((1, tq, A_V_DIM), lambda b, h, i: (b, i, h)),
        out_shape=jax.ShapeDtypeStruct((B, L, A_WIDTH), BF16),
        scratch_shapes=[pltpu.VMEM((1, 2 * tq), F32), pltpu.VMEM((1, 2 * tq), F32),
                        pltpu.VMEM((A_V_DIM, 2 * tq), F32),
                        pltpu.VMEM((tq, 2 * tq), F32), pltpu.VMEM((tq, 2 * tq), F32)],
        compiler_params=_cparams("parallel", "parallel", "arbitrary"),
        name="flash_prompt",
    )(lam_row, g_col, q, kb, vt)


def _decode_kernel(lam_ref, g_ref, q_ref, kn_ref, vn_ref, kp_ref, vp_ref, o_ref, s_s, acc_s, m_s, r_s,
                   *, nb, tk, L, out_scale):
    j = pl.program_id(1)
    P = nb * tk
    n_new = kn_ref.shape[1]

    def q_rows():
        rep = jnp.concatenate([q_ref[0]] * (LANES // L), axis=0)
        grp = lax.broadcasted_iota(jnp.int32, rep.shape, 0) // L
        cols = lax.broadcasted_iota(jnp.int32, rep.shape, 1) // A_HEAD_DIM
        return jnp.where(grp == cols, rep, jnp.zeros_like(rep))

    @pl.when(j < nb)
    def _():
        off = pl.multiple_of(j * tk, tk)
        s_s[:, pl.ds(off, tk)] = _dot(q_rows(), kp_ref[0, 0].astype(BF16))

    @pl.when(j == nb - 1)
    def _():
        s_new = _dot_nt(q_rows(), kn_ref[0])
        valid = lax.broadcasted_iota(jnp.int32, s_new.shape, 1) < L
        s_s[:, P:P + n_new] = jnp.where(valid, s_new, NEG_INF)
        s = s_s[...]
        m = jnp.max(s, axis=1, keepdims=True)
        m_s[...] = m
        r_s[...] = 1.0 / jnp.sum(jnp.exp2(s - m), axis=1, keepdims=True)
        acc_s[...] = jnp.zeros(acc_s.shape, F32)

    def weights(cols):
        return (jnp.exp2(s_s[:, cols] - m_s[...]) * r_s[...]).astype(BF16)

    @pl.when(j >= nb)
    def _():
        off = pl.multiple_of((j - nb) * tk, tk)
        w = weights(pl.ds(off, tk))
        for h in range(A_HEADS):
            rows = slice(2 * h * L, (2 * h + 2) * L)
            vh = vp_ref[0, 0, pl.ds(h, tk, stride=A_HEADS), :]
            acc_s[rows, :] += _dot(w[rows, :], vh.astype(BF16))

    @pl.when(j == 2 * nb - 1)
    def _():
        w = weights(slice(P, P + n_new))
        for h in range(A_HEADS):
            hs = slice(h * A_V_DIM, (h + 1) * A_V_DIM)
            rows = slice(2 * h * L, (2 * h + 2) * L)
            acc = acc_s[rows, :] + _dot(w[rows, :], vn_ref[0, :, hs])
            o = acc[:L] - lam_ref[...] * acc[L:]
            ms = jnp.mean(o * o, axis=-1, keepdims=True)
            o_ref[0, :, hs] = (o * lax.rsqrt(ms + NORM_EPS) * g_ref[...] * out_scale).astype(o_ref.dtype)


def _decode_attention(q, k_new, v_new, kt_cache, v_cache, li, lam_row, g_row, out_scale):
    B, L, _ = q.shape
    assert 2 * A_HEADS * L == LANES, "decode attention packs (head, map, frame) onto the 128 lanes"
    P = kt_cache.shape[3]
    tk = min(P, 1024)
    nb = P // tk
    pad = ((0, 0), (0, LANES - L), (0, 0))
    k_new = jnp.pad(k_new, pad)
    v_new = jnp.pad(v_new, pad)
    new = lambda b, j: (b, 0, 0)
    return pl.pallas_call(
        functools.partial(_decode_kernel, nb=nb, tk=tk, L=L, out_scale=out_scale),
        grid=(B, 2 * nb),
        in_specs=[pl.BlockSpec((1, LANES), lambda b, j: (0, 0)),
                  pl.BlockSpec((1, A_V_DIM), lambda b, j: (0, 0)),
                  pl.BlockSpec((1, L, A_QK_WIDTH), new),
                  pl.BlockSpec((1, LANES, A_QK_WIDTH), new),
                  pl.BlockSpec((1, LANES, A_WIDTH), new),
                  pl.BlockSpec((1, 1, A_QK_WIDTH, tk), lambda b, j: (li, b, 0, jnp.minimum(j, nb - 1))),
                  pl.BlockSpec((1, 1, tk * A_HEADS, A_V_DIM),
                               lambda b, j: (li, b, jnp.maximum(j - nb, 0), 0))],
        out_specs=pl.BlockSpec((1, L, A_WIDTH), new),
        out_shape=jax.ShapeDtypeStruct((B, L, A_WIDTH), BF16),
        scratch_shapes=[pltpu.VMEM((LANES, P + LANES), F32), pltpu.VMEM((LANES, A_V_DIM), F32),
                        pltpu.VMEM((LANES, 1), F32), pltpu.VMEM((LANES, 1), F32)],
        compiler_params=_cparams("parallel", "arbitrary"),
        name="decode_attention",
    )(lam_row, g_row, q, k_new, v_new, kt_cache, v_cache)


def _gelu_tanh(x):
    return 0.5 * x * (1.0 + jnp.tanh(math.sqrt(2.0 / math.pi) * (x + 0.044715 * (x * x * x))))


def _rglru_kernel(x_ref, w_ref, b_ref, cw_ref, cb_ref, gw_ref, gbias_ref, sp_ref, cbuf_ref, h0_ref,
                  y_ref, conv_ref, h_ref, xs, hc, hs, *, tl, pos0):
    j = pl.program_id(1)
    tail = CONV_W - 1
    base = 8

    @pl.when(j == 0)
    def _():
        xs[0:base, :] = jnp.zeros((base, B_WIDTH), F32)
        xs[base - tail:base, :] = cbuf_ref[0]
        hc[...] = h0_ref[0]

    x = x_ref[0].astype(BF16)
    y = _dot(x, w_ref[...]) + b_ref[...]
    xb = y[:, :B_WIDTH]
    gb = y[:, B_WIDTH:]
    xs[base:base + tl, :] = xb
    xc = cb_ref[...] + cw_ref[0:1, :] * xs[base - tail:base - tail + tl, :]
    for t in range(1, CONV_W):
        xc = xc + cw_ref[t:t + 1, :] * xs[base - tail + t:base - tail + t + tl, :]
    new_tail = xs[base + tl - tail:base + tl, :]
    xs[base - tail:base, :] = new_tail
    conv_ref[0] = new_tail

    gates = _dot(xc.astype(BF16), gw_ref[...]) + gbias_ref[...]
    r = _sigmoid(gates[:, :B_WIDTH])
    ig = _sigmoid(gates[:, B_WIDTH:])
    log_a = -LRU_C * r * sp_ref[...]
    a = jnp.exp(log_a)
    t = jnp.tanh(log_a)
    mult = jnp.sqrt(-2.0 * t / (1.0 - t))
    row = lax.broadcasted_iota(jnp.int32, (tl, B_WIDTH), 0)
    mult = jnp.where(row + (pos0 + j * tl) == 0, 1.0, mult)
    bb = xc * ig * mult
    a3 = a.reshape(tl // SUBLANES, SUBLANES, B_WIDTH)
    b3 = bb.reshape(tl // SUBLANES, SUBLANES, B_WIDTH)
    sub = lax.broadcasted_iota(jnp.int32, a3.shape, 1)
    d = 1
    while d < SUBLANES:
        keep = sub >= d
        a_sh = pltpu.roll(a3, d, 1)
        b_sh = pltpu.roll(b3, d, 1)
        b3 = jnp.where(keep, a3 * b_sh + b3, b3)
        a3 = jnp.where(keep, a3 * a_sh, a3)
        d *= 2
    carry = hc[...]
    for g in range(tl // SUBLANES):
        hg = a3[g] * carry + b3[g]
        hs[g * SUBLANES:(g + 1) * SUBLANES, :] = hg
        carry = hg[SUBLANES - 1:SUBLANES, :]
    hc[...] = carry
    h_ref[0] = carry
    y_ref[0] = (_gelu_tanh(gb) * hs[...]).astype(y_ref.dtype)


def _rglru(x, w, b, conv_w, conv_b, gate_w, gate_b, sp, conv_buf, h0, pos0):
    B, L, _ = x.shape
    tl = min(L, 256)
    tail = CONV_W - 1
    const = lambda bi, j: (0, 0)
    per_b = lambda bi, j: (bi, 0, 0)
    return pl.pallas_call(
        functools.partial(_rglru_kernel, tl=tl, pos0=pos0),
        grid=(B, L // tl),
        in_specs=[pl.BlockSpec((1, tl, D_MODEL), lambda bi, j: (bi, j, 0)),
                  pl.BlockSpec((D_MODEL, 2 * B_WIDTH), const),
                  pl.BlockSpec((1, 2 * B_WIDTH), const),
                  pl.BlockSpec((CONV_W, B_WIDTH), const),
                  pl.BlockSpec((1, B_WIDTH), const),
                  pl.BlockSpec((B_WIDTH, 2 * B_WIDTH), const),
                  pl.BlockSpec((1, 2 * B_WIDTH), const),
                  pl.BlockSpec((1, B_WIDTH), const),
                  pl.BlockSpec((1, tail, B_WIDTH), per_b),
                  pl.BlockSpec((1, 1, B_WIDTH), per_b)],
        out_specs=[pl.BlockSpec((1, tl, B_WIDTH), lambda bi, j: (bi, j, 0)),
                   pl.BlockSpec((1, tail, B_WIDTH), per_b),
                   pl.BlockSpec((1, 1, B_WIDTH), per_b)],
        out_shape=[jax.ShapeDtypeStruct((B, L, B_WIDTH), BF16),
                   jax.ShapeDtypeStruct((B, tail, B_WIDTH), F32),
                   jax.ShapeDtypeStruct((B, 1, B_WIDTH), F32)],
        scratch_shapes=[pltpu.VMEM((8 + tl, B_WIDTH), F32), pltpu.VMEM((1, B_WIDTH), F32),
                        pltpu.VMEM((tl, B_WIDTH), F32)],
        compiler_params=_cparams("parallel", "arbitrary"),
        name="rglru",
    )(x, w, b, conv_w, conv_b, gate_w, gate_b, sp, conv_buf, h0)


def _hgrn_levels(C):
    return [C >> i for i in range(int(math.log2(C)) + 1)]


HGRN_SPLIT = 3


def _hgrn_sum_matrices(C):
    t = np.arange(C)

    def prefix(m):
        return (t[None, :] >= ((t // m) * m)[:, None]) & (t[None, :] <= t[:, None])

    def suffix(m):
        return (t[None, :] > t[:, None]) & (t[None, :] <= ((t // m) * m + m - 1)[:, None])

    mats = [prefix(C), suffix(C)]
    for m in _hgrn_levels(C)[1:-1]:
        upper = ((t // m) % 2 == 1)[:, None]
        mats.append(np.where(upper, prefix(m), suffix(m)))
    one = np.concatenate(mats, axis=0).astype(np.float32)
    return np.concatenate([one] * HGRN_SPLIT, axis=1)


def _hgrn_kernel(x_ref, w_ref, b_ref, lb_ref, ng_ref, dm_ref, s0_ref, y_ref, s_ref,
                 q_s, k_s, g_s, v_s, gate_s, o_s, st_s, *, tl, C):
    j = pl.program_id(1)
    levels = _hgrn_levels(C)
    nlev = len(levels)

    @pl.when(j == 0)
    def _():
        for h in range(C_HEADS):
            st_s[h] = s0_ref[0, h].T

    x = x_ref[0].astype(BF16)
    y = _dot(x, w_ref[...]) + b_ref[...]
    lb = lb_ref[...]
    z = y[:, C_WIDTH:2 * C_WIDTH]
    f = lb + (1.0 - lb) * _sigmoid(z)
    q_s[...] = _silu(y[:, :C_WIDTH])
    k_s[...] = (1.0 - lb) * _sigmoid(-z)
    g_s[...] = jnp.log(jnp.maximum(f, F_TINY))
    v_s[...] = y[:, 2 * C_WIDTH:3 * C_WIDTH]
    gate_s[...] = _silu(y[:, 3 * C_WIDTH:])

    ti = lax.broadcasted_iota(jnp.int32, (C, C), 0)
    si = lax.broadcasted_iota(jnp.int32, (C, C), 1)
    masks = [ti == si]
    for m in levels[1:]:
        masks.append((ti // (2 * m) == si // (2 * m)) & ((ti // m) % 2 == 1) & ((si // m) % 2 == 0))
    odd_row = lax.broadcasted_iota(jnp.int32, (C, C_WIDTH), 0) % 2 == 1

    def chunk(ci, carry):
        r0 = pl.multiple_of(ci * C, C)
        rows = pl.ds(r0, C)
        g = g_s[rows, :]
        pieces = []
        rest = g
        for _ in range(HGRN_SPLIT):
            piece = rest.astype(BF16)
            pieces.append(piece)
            rest = rest - piece.astype(F32)
        sums = _dot(dm_ref[...], jnp.concatenate(pieces, axis=0))
        e = jnp.exp(jnp.minimum(sums, 0.0))
        e_one = jnp.exp(jnp.where(odd_row, g, 0.0))
        for h in range(C_HEADS):
            hs = slice(h * C_DK, (h + 1) * C_DK)
            q = q_s[rows, hs]
            k = k_s[rows, hs]
            vb = v_s[rows, hs].astype(BF16)
            scores = jnp.where(masks[0], _dot_nt(q.astype(BF16), k.astype(BF16)), 0.0)
            for li in range(1, nlev):
                el = e[(li + 1) * C:(li + 2) * C, hs] if li < nlev - 1 else e_one[:, hs]
                scores = scores + jnp.where(masks[li], _dot_nt((q * el).astype(BF16), (k * el).astype(BF16)), 0.0)
            e_cum = e[0:C, hs]
            qc = (q * e_cum).astype(BF16)
            kc = (k * e[C:2 * C, hs]).astype(BF16)
            st = st_s[h]
            o = _dot(scores.astype(BF16), vb) + _dot_nt(qc, st.astype(BF16))
            st_s[h] = st * e_cum[C - 1:C, :] + _dot_tn(vb, kc)
            ms = jnp.mean(o * o, axis=-1, keepdims=True)
            o_s[rows, hs] = o * lax.rsqrt(ms + NORM_EPS) * ng_ref[...] * gate_s[rows, hs]
        return carry

    lax.fori_loop(0, tl // C, chunk, 0, unroll=True)
    y_ref[0] = o_s[...].astype(y_ref.dtype)
    for h in range(C_HEADS):
        s_ref[0, h] = st_s[h].T


def _hgrn(x, w, b, lb, norm_g, s0):
    B, L, _ = x.shape
    C = min(CHUNK, L)
    tl = min(L, 512)
    dm = jnp.asarray(_hgrn_sum_matrices(C), BF16)
    const = lambda bi, j: (0, 0)
    per_b = lambda bi, j: (bi, 0, 0, 0)
    wide = pltpu.VMEM((tl, C_WIDTH), F32)
    return pl.pallas_call(
        functools.partial(_hgrn_kernel, tl=tl, C=C),
        grid=(B, L // tl),
        in_specs=[pl.BlockSpec((1, tl, D_MODEL), lambda bi, j: (bi, j, 0)),
                  pl.BlockSpec((D_MODEL, 4 * C_WIDTH), const),
                  pl.BlockSpec((1, 4 * C_WIDTH), const),
                  pl.BlockSpec((1, C_WIDTH), const),
                  pl.BlockSpec((1, C_DV), const),
                  pl.BlockSpec(dm.shape, const),
                  pl.BlockSpec((1, C_HEADS, C_DK, C_DV), per_b)],
        out_specs=[pl.BlockSpec((1, tl, C_WIDTH), lambda bi, j: (bi, j, 0)),
                   pl.BlockSpec((1, C_HEADS, C_DK, C_DV), per_b)],
        out_shape=[jax.ShapeDtypeStruct((B, L, C_WIDTH), BF16),
                   jax.ShapeDtypeStruct((B, C_HEADS, C_DK, C_DV), F32)],
        scratch_shapes=[wide, wide, wide, wide, wide, wide, pltpu.VMEM((C_HEADS, C_DV, C_DK), F32)],
        compiler_params=_cparams("parallel", "arbitrary"),
        name="hgrn",
    )(x, w, b, lb, norm_g, dm, s0)


def _merge_kernel(x_ref, ya_ref, yb_ref, yc_ref, wg_ref, bg_ref, wo_ref, g1_ref, b1_ref, rw_ref, rb_ref,
                  o_ref, cls_ref, *, alpha):
    xb = x_ref[...].astype(BF16)
    for c in range(D_MODEL // MERGE_COLS):
        cs = slice(c * MERGE_COLS, (c + 1) * MERGE_COLS)
        pre = alpha * x_ref[:, cs]
        for n, y_ref in enumerate((ya_ref, yb_ref, yc_ref)):
            gs = slice(n * D_MODEL + c * MERGE_COLS, n * D_MODEL + (c + 1) * MERGE_COLS)
            gate = _sigmoid(_dot(xb, wg_ref[:, gs]) + bg_ref[:, gs])
            pre = pre + gate * _dot(y_ref[...], wo_ref[n * A_WIDTH:(n + 1) * A_WIDTH, cs])
        o_ref[:, cs] = pre
    x1 = _layer_norm(o_ref[:, :D_MODEL], g1_ref[...], b1_ref[...])
    o_ref[:, :D_MODEL] = x1

    lg = _dot(x1.astype(BF16), rw_ref[...]) + rb_ref[...]
    col = [lg[:, e:e + 1] for e in range(N_EXPERTS)]
    gmax = [functools.reduce(jnp.maximum, col[g * EXPERTS_PER_GROUP:(g + 1) * EXPERTS_PER_GROUP])
            for g in range(N_GROUPS)]
    best = functools.reduce(jnp.maximum, gmax)
    gs = jnp.where(gmax[0] == best, 0, jnp.where(gmax[1] == best, 1, jnp.where(gmax[2] == best, 2, 3)))
    v = []
    for e in range(EXPERTS_PER_GROUP):
        v.append(jnp.where(gs == 0, col[e],
                           jnp.where(gs == 1, col[EXPERTS_PER_GROUP + e],
                                     jnp.where(gs == 2, col[2 * EXPERTS_PER_GROUP + e],
                                               col[3 * EXPERTS_PER_GROUP + e]))))
    m1 = best
    i1 = jnp.where(v[0] == m1, 0, jnp.where(v[1] == m1, 1, jnp.where(v[2] == m1, 2, 3)))
    v2 = [jnp.where(i1 == e, -jnp.inf, v[e]) for e in range(EXPERTS_PER_GROUP)]
    m2 = functools.reduce(jnp.maximum, v2)
    i2 = jnp.where(v2[0] == m2, 0, jnp.where(v2[1] == m2, 1, jnp.where(v2[2] == m2, 2, 3)))
    e2 = jnp.exp(m2 - m1)
    p1 = 1.0 / (1.0 + e2)
    p2 = e2 / (1.0 + e2)
    ea = jnp.minimum(i1, i2)
    eb = jnp.maximum(i1, i2)
    ga = jnp.where(i1 < i2, p1, p2)
    gb = jnp.where(i1 < i2, p2, p1)
    pair = jnp.where(ea == 0, 0, jnp.where(ea == 1, 3, 5)) + (eb - ea - 1)
    cls = (gs * PAIRS_PER_GROUP + pair).astype(F32)
    lane = lax.broadcasted_iota(jnp.int32, (x_ref.shape[0], AUX_W), 1)
    aux = jnp.where(lane == 0, ga, jnp.where(lane == 1, gb, jnp.where(lane == 2, cls, 0.0)))
    o_ref[:, D_MODEL:] = aux
    pick = (lax.broadcasted_iota(jnp.int32, (8, AUX_W), 0) == 0) & (lax.broadcasted_iota(jnp.int32, (8, AUX_W), 1) == 2)
    cls_ref[...] = _dot_nt(jnp.where(pick, 1.0, 0.0).astype(BF16), aux.astype(BF16))


def _merge(x2d, ya, yb, yc, wg, bg, wo, g1, b1, rw, rb, alpha):
    T = x2d.shape[0]
    tm = min(T, 512)
    row = lambda i: (i, 0)
    const = lambda i: (0, 0)
    return pl.pallas_call(
        functools.partial(_merge_kernel, alpha=alpha),
        grid=(T // tm,),
        in_specs=[pl.BlockSpec((tm, D_MODEL), row),
                  pl.BlockSpec((tm, A_WIDTH), row),
                  pl.BlockSpec((tm, B_WIDTH), row),
                  pl.BlockSpec((tm, C_WIDTH), row),
                  pl.BlockSpec((D_MODEL, N_BRANCH * D_MODEL), const),
                  pl.BlockSpec((1, N_BRANCH * D_MODEL), const),
                  pl.BlockSpec((N_BRANCH * A_WIDTH, D_MODEL), const),
                  pl.BlockSpec((1, D_MODEL), const),
                  pl.BlockSpec((1, D_MODEL), const),
                  pl.BlockSpec((D_MODEL, N_EXPERTS), const),
                  pl.BlockSpec((1, N_EXPERTS), const)],
        out_specs=[pl.BlockSpec((tm, D_MODEL + AUX_W), row), pl.BlockSpec((8, tm), lambda i: (0, i))],
        out_shape=[jax.ShapeDtypeStruct((T, D_MODEL + AUX_W), F32), jax.ShapeDtypeStruct((8, T), F32)],
        compiler_params=_cparams("parallel"),
        name="merge",
    )(x2d, ya, yb, yc, wg, bg, wo, g1, b1, rw, rb)


def _dispatch_kernel(dest_ref, x_ref, out_hbm, zbuf, sem, *, tm, n_tok_steps):
    i = pl.program_id(0)

    def scatter_rows(src):
        def start(r, c):
            pltpu.make_async_copy(src.at[pl.ds(r, 1), :], out_hbm.at[pl.ds(dest_ref[0, 0, r], 1), :], sem).start()
            return c

        lax.fori_loop(0, tm, start, 0, unroll=8)
        pltpu.make_async_copy(src, out_hbm.at[pl.ds(0, tm), :], sem).wait()

    @pl.when(i < n_tok_steps)
    def _():
        scatter_rows(x_ref)

    @pl.when(i >= n_tok_steps)
    def _():
        @pl.when(i == n_tok_steps)
        def _():
            zbuf[...] = jnp.zeros(zbuf.shape, F32)

        scatter_rows(zbuf)


def _dispatch(x1w, dest_all, n_rows):
    T, W = x1w.shape
    tm = min(T, 512)
    n_tok_steps = T // tm
    steps = dest_all.shape[0] // tm
    return pl.pallas_call(
        functools.partial(_dispatch_kernel, tm=tm, n_tok_steps=n_tok_steps),
        grid=(steps,),
        in_specs=[pl.BlockSpec((1, 1, tm), lambda i: (i, 0, 0), memory_space=pltpu.SMEM),
                  pl.BlockSpec((tm, W), lambda i: (jnp.minimum(i, n_tok_steps - 1), 0))],
        out_specs=pl.BlockSpec(memory_space=pl.ANY),
        out_shape=jax.ShapeDtypeStruct((n_rows, W), F32),
        scratch_shapes=[pltpu.VMEM((tm, W), F32), pltpu.SemaphoreType.DMA(())],
        compiler_params=_cparams("arbitrary"),
        name="moe_dispatch",
    )(dest_all.reshape(steps, 1, tm), x1w)


def _experts_kernel(ea_ref, eb_ref, nv_ref, x_ref, w1a, w3a, w2a, w1b, w3b, w2b, g2_ref, b2_ref, o_ref, *, alpha):
    i = pl.program_id(0)

    @pl.when(nv_ref[i] > 0)
    def _():
        xw = x_ref[...]
        x = xw[:, :D_MODEL]
        xb = x.astype(BF16)
        y = None
        for n, (w1, w3, w2) in enumerate(((w1a, w3a, w2a), (w1b, w3b, w2b))):
            hmid = _silu(_dot(xb, w1[0])) * _dot(xb, w3[0])
            ye = xw[:, D_MODEL + n:D_MODEL + n + 1] * _dot(hmid.astype(BF16), w2[0])
            y = ye if y is None else y + ye
        o_ref[...] = _layer_norm(alpha * x + y, g2_ref[...], b2_ref[...])

    @pl.when(nv_ref[i] == 0)
    def _():
        o_ref[...] = jnp.zeros(o_ref.shape, F32)


def _experts(xs, ea, eb, nv, w1, w3, w2, g2, b2, alpha):
    n_rows, W = xs.shape
    wa = lambda i, ea, eb, nv: (ea[i], 0, 0)
    wb = lambda i, ea, eb, nv: (eb[i], 0, 0)
    const = lambda i, ea, eb, nv: (0, 0)
    row = lambda i, ea, eb, nv: (i, 0)
    up = (1, D_MODEL, D_EXPERT)
    down = (1, D_EXPERT, D_MODEL)
    grid_spec = pltpu.PrefetchScalarGridSpec(
        num_scalar_prefetch=3,
        grid=(n_rows // MOE_ROWS,),
        in_specs=[pl.BlockSpec((MOE_ROWS, W), row),
                  pl.BlockSpec(up, wa), pl.BlockSpec(up, wa), pl.BlockSpec(down, wa),
                  pl.BlockSpec(up, wb), pl.BlockSpec(up, wb), pl.BlockSpec(down, wb),
                  pl.BlockSpec((1, D_MODEL), const), pl.BlockSpec((1, D_MODEL), const)],
        out_specs=pl.BlockSpec((MOE_ROWS, D_MODEL), row),
    )
    return pl.pallas_call(
        functools.partial(_experts_kernel, alpha=alpha),
        grid_spec=grid_spec,
        out_shape=jax.ShapeDtypeStruct((n_rows, D_MODEL), F32),
        compiler_params=_cparams("arbitrary"),
        name="moe_experts",
    )(ea, eb, nv, xs, w1, w3, w2, w1, w3, w2, g2, b2)


def _combine_kernel(dest_ref, y_hbm, o_ref, sem, *, tm):
    def start(r, c):
        pltpu.make_async_copy(y_hbm.at[pl.ds(dest_ref[0, 0, r], 1), :], o_ref.at[pl.ds(r, 1), :], sem).start()
        return c

    lax.fori_loop(0, tm, start, 0, unroll=8)
    pltpu.make_async_copy(y_hbm.at[pl.ds(0, tm), :], o_ref, sem).wait()


def _combine(ys, dest, T):
    tm = min(T, 512)
    return pl.pallas_call(
        functools.partial(_combine_kernel, tm=tm),
        grid=(T // tm,),
        in_specs=[pl.BlockSpec((1, 1, tm), lambda i: (i, 0, 0), memory_space=pltpu.SMEM),
                  pl.BlockSpec(memory_space=pl.ANY)],
        out_specs=pl.BlockSpec((tm, D_MODEL), lambda i: (i, 0)),
        out_shape=jax.ShapeDtypeStruct((T, D_MODEL), F32),
        scratch_shapes=[pltpu.SemaphoreType.DMA(())],
        compiler_params=_cparams("arbitrary"),
        name="moe_combine",
    )(dest.reshape(T // tm, 1, tm), ys)


def _route(cls, T):
    n_blocks = T // MOE_ROWS + N_CLASSES
    n_rows = n_blocks * MOE_ROWS
    rb = math.gcd(T, MOE_ROWS)
    onehot = (cls[:, None] == jnp.arange(N_CLASSES, dtype=jnp.int32)[None, :]).astype(F32).reshape(T // rb, rb, N_CLASSES)
    strict = jnp.asarray(np.tril(np.ones((rb, rb), np.float32), -1))
    rank_in = jnp.einsum('ij,bjc->bic', strict, onehot)
    blk_counts = jnp.sum(onehot, axis=1)
    blk_off = jnp.cumsum(blk_counts, axis=0) - blk_counts
    rank = jnp.sum(onehot * (rank_in + blk_off[:, None, :]), axis=-1).reshape(T).astype(jnp.int32)
    counts = jnp.sum(blk_counts, axis=0).astype(jnp.int32)
    padded = (counts + MOE_ROWS - 1) // MOE_ROWS * MOE_ROWS
    pad_end = jnp.cumsum(padded)
    pad_start = pad_end - padded
    dest = (pad_start[cls] + rank).astype(jnp.int32)
    seg_start = jnp.concatenate([pad_start + counts, pad_end[-1:]])
    seg_len = jnp.concatenate([padded - counts, n_rows - pad_end[-1:]])
    seg_end = jnp.cumsum(seg_len)
    kk = jnp.arange(n_rows - T, dtype=jnp.int32)
    seg = jnp.sum((seg_end[None, :] <= kk[:, None]).astype(jnp.int32), axis=1)
    pad_rows = (seg_start[seg] + kk - (seg_end[seg] - seg_len[seg])).astype(jnp.int32)
    blk_start = jnp.arange(n_blocks, dtype=jnp.int32) * MOE_ROWS
    blk_cls = jnp.minimum(jnp.sum((pad_end[None, :] <= blk_start[:, None]).astype(jnp.int32), axis=1), N_CLASSES - 1)
    nv = jnp.clip(pad_start[blk_cls] + counts[blk_cls] - blk_start, 0, MOE_ROWS).astype(jnp.int32)
    grp = blk_cls // PAIRS_PER_GROUP
    pair = blk_cls % PAIRS_PER_GROUP
    pa = jnp.asarray([0, 0, 0, 1, 1, 2], jnp.int32)[pair]
    pb = jnp.asarray([1, 2, 3, 2, 3, 3], jnp.int32)[pair]
    ea = grp * EXPERTS_PER_GROUP + pa
    eb = grp * EXPERTS_PER_GROUP + pb
    return ea, eb, nv, dest, pad_rows, n_rows


def _moe(x1w, cls_rows, w1, w3, w2, g2, b2, alpha):
    T = x1w.shape[0]
    cls = cls_rows[0].astype(jnp.int32)
    ea, eb, nv, dest, pad_rows, n_rows = _route(cls, T)
    xs = _dispatch(x1w, jnp.concatenate([dest, pad_rows]), n_rows)
    ys = _experts(xs, ea, eb, nv, w1, w3, w2, g2, b2, alpha)
    return _combine(ys, dest, T)


def _block_diag(w):
    n, d, _ = w.shape
    eye = jnp.eye(n, dtype=w.dtype)
    return jnp.einsum('nde,nm->ndme', w, eye).reshape(n * d, n * d)


def _layer(x, pos0, li, P, past, kv_carried):
    B, L, _ = x.shape
    T = B * L
    depth = P['depth']
    alpha = (2 * depth) ** 0.25
    w_in, b_in = P['w_in'][li], P['b_in'][li]

    def cols(lo, hi):
        return w_in[:, lo:hi], b_in[lo:hi].reshape(1, -1)

    lam_init = 0.8 - 0.6 * math.exp(-0.3 * li)
    dl = P['diff_lambda'][li]
    lam = jnp.exp(jnp.sum(dl[0] * dl[1])) - jnp.exp(jnp.sum(dl[2] * dl[3])) + lam_init
    lam_row = jnp.full((1, LANES), lam, F32)
    subln_g = P['diff_subln_g'][li].reshape(1, -1)

    wa, ba = cols(OFF_A, OFF_B)
    q, k_all, kb, v_all, vb, vt = _attn_proj(x, wa, ba, pos0, li, depth, kv_carried)
    if past is None:
        y_a = _flash_prompt(q, kb, vt, lam_row, subln_g.reshape(-1, 1), 1.0 - lam_init)
        conv_buf = jnp.zeros((B, CONV_W - 1, B_WIDTH), F32)
        h0 = jnp.zeros((B, 1, B_WIDTH), F32)
        s0 = jnp.zeros((B, C_HEADS, C_DK, C_DV), F32)
    else:
        kt_cache, v_cache, conv_buf, h0, s0 = past
        y_a = _decode_attention(q, kb, vb, kt_cache, v_cache, li, lam_row, subln_g, 1.0 - lam_init)
        h0 = h0.reshape(B, 1, B_WIDTH)

    wb, bb = cols(OFF_B, OFF_C)
    y_b, conv_new, h_new = _rglru(x, wb, bb, P['conv_w'][li], P['conv_b'][li].reshape(1, -1),
                                  P['lru_gate_w'][li], P['lru_gate_b'][li], P['lru_sp'][li].reshape(1, -1),
                                  conv_buf, h0, pos0)
    wc, bc = cols(OFF_C, OFF_G)
    y_c, s_new = _hgrn(x, wc, bc, P['lower_bounds'][li].reshape(1, -1), P['hgrn_norm_g'][li].reshape(1, -1), s0)

    wg, bg = cols(OFF_G, N_IN)
    x1w, cls_rows = _merge(x.reshape(T, D_MODEL), y_a.reshape(T, A_WIDTH), y_b.reshape(T, B_WIDTH),
                 y_c.reshape(T, C_WIDTH), wg, bg, P['w_out'][li], P['ln1_g'][li].reshape(1, -1),
                 P['ln1_b'][li].reshape(1, -1), P['router_w'], P['router_b'].reshape(1, -1), alpha)
    x2 = _moe(x1w, cls_rows, P['moe_w1'][li], P['moe_w3'][li], P['moe_w2'][li], P['ln2_g'][li].reshape(1, -1),
              P['ln2_b'][li].reshape(1, -1), alpha)
    states = (conv_new, h_new.reshape(B, B_WIDTH), s_new)
    return x2.reshape(B, L, D_MODEL), (k_all, v_all), states


def kernel(x_prompt, x_sample, cache_k, cache_v, state_conv, state_lru, state_hgrn, ln_in_g, ln_in_b, w_in, b_in, diff_lambda, diff_subln_g, conv_w, conv_b, lru_wa, lru_ba, lru_wx, lru_bx, lru_lambda, hgrn_lb, hgrn_norm_g, w_out, ln1_g, ln1_b, router_w, router_b, moe_w1, moe_w3, moe_w2, ln2_g, ln2_b):
    depth = w_in.shape[0]
    p_lb = jax.nn.softmax(hgrn_lb.astype(F32), axis=0)
    P = dict(
        depth=depth,
        w_in=w_in.astype(BF16), b_in=b_in,
        diff_lambda=diff_lambda.astype(F32), diff_subln_g=diff_subln_g,
        conv_w=conv_w, conv_b=conv_b,
        lru_gate_w=jnp.stack([jnp.concatenate([_block_diag(lru_wa[l]), _block_diag(lru_wx[l])], axis=1)
                              for l in range(depth)]).astype(BF16),
        lru_gate_b=jnp.concatenate([lru_ba, lru_bx], axis=1).reshape(depth, 1, -1),
        lru_sp=jax.nn.softplus(-lru_lambda.astype(F32)),
        lower_bounds=jnp.cumsum(p_lb, axis=0) - p_lb[0],
        hgrn_norm_g=hgrn_norm_g,
        w_out=w_out.astype(BF16), ln1_g=ln1_g, ln1_b=ln1_b,
        router_w=router_w.astype(BF16), router_b=router_b,
        moe_w1=moe_w1.astype(BF16), moe_w3=moe_w3.astype(BF16), moe_w2=moe_w2.astype(BF16),
        ln2_g=ln2_g, ln2_b=ln2_b,
    )
    Bp, Lp, _ = x_prompt.shape
    Bs, Ls, _ = x_sample.shape
    past_len = cache_k.shape[2]
    kt_cache = jnp.transpose(cache_k, (0, 1, 3, 4, 5, 2)).reshape(depth, Bs, A_QK_WIDTH, past_len)
    v_cache = cache_v.reshape(depth, Bs, past_len * A_HEADS, A_V_DIM)
    xp = _input_ln(x_prompt.reshape(Bp * Lp, D_MODEL), ln_in_g, ln_in_b).reshape(Bp, Lp, D_MODEL)
    xs = _input_ln(x_sample.reshape(Bs * Ls, D_MODEL), ln_in_g, ln_in_b).reshape(Bs, Ls, D_MODEL)
    prompt_states, sample_states = [], []
    kv_p = kv_s = None
    for li in range(depth):
        xp, kv_p, st_p = _layer(xp, 0, li, P, None, kv_p)
        xs, kv_s, st_s = _layer(xs, past_len, li, P,
                                (kt_cache, v_cache, state_conv[li], state_lru[li], state_hgrn[li]), kv_s)
        prompt_states.append(st_p)
        sample_states.append(st_s)
    conv_p, lru_p, hgrn_p = [jnp.stack(t) for t in zip(*prompt_states)]
    conv_s, lru_s, hgrn_s = [jnp.stack(t) for t in zip(*sample_states)]
    k_p = kv_p[0].reshape(depth, Bp, Lp, A_HEADS, 2, A_HEAD_DIM)
    v_p = kv_p[1].reshape(depth, Bp, Lp, A_HEADS, A_V_DIM)
    k_s = kv_s[0].reshape(depth, Bs, Ls, A_HEADS, 2, A_HEAD_DIM)
    v_s = kv_s[1].reshape(depth, Bs, Ls, A_HEADS, A_V_DIM)
    return (xp, xs, k_p, v_p, conv_p, lru_p, hgrn_p, k_s, v_s, conv_s, lru_s, hgrn_s)
```

```python
import functools
import math

import numpy as np
import jax
import jax.numpy as jnp
from jax import lax
from jax.experimental import pallas as pl
from jax.experimental.pallas import tpu as pltpu

F32 = jnp.float32
BF16 = jnp.bfloat16

D_MODEL = 1024
CHUNK = 64
A_HEADS = 4
A_HEAD_DIM = 64
A_V_DIM = 2 * A_HEAD_DIM
A_QK_WIDTH = A_HEADS * 2 * A_HEAD_DIM
A_WIDTH = A_HEADS * A_V_DIM
ROPE_THETA = 10000.0
NEG_INF = -1e30
Q_SCALE = A_HEAD_DIM ** -0.5 * math.log2(math.e)
B_WIDTH = 512
B_BLOCKS = 8
CONV_W = 4
LRU_C = 8.0
C_HEADS = 4
C_DK = 128
C_DV = 128
C_WIDTH = C_HEADS * C_DV
F_TINY = 1e-30
N_BRANCH = 3
N_EXPERTS = 16
N_GROUPS = 4
EXPERTS_PER_GROUP = N_EXPERTS // N_GROUPS
D_EXPERT = 512
NORM_EPS = 1e-5

LANES = 128
SUBLANES = 8
PAIRS_PER_GROUP = 6
N_CLASSES = N_GROUPS * PAIRS_PER_GROUP
MOE_ROWS = 256
MERGE_COLS = 256
MOE_DMA_ROWS = 1024
AUX_W = LANES
VMEM_LIMIT = 56 * 1024 * 1024

OFF_A = 0
OFF_B = OFF_A + 2 * A_QK_WIDTH + A_WIDTH
OFF_C = OFF_B + 2 * B_WIDTH
OFF_G = OFF_C + 4 * C_WIDTH
N_IN = OFF_G + N_BRANCH * D_MODEL


def _cparams(*sem):
    return pltpu.CompilerParams(dimension_semantics=sem, vmem_limit_bytes=VMEM_LIMIT)


def _dot(a, b):
    return jnp.dot(a, b, preferred_element_type=F32)


def _dot_nt(a, b):
    return lax.dot_general(a, b, (((1,), (1,)), ((), ())), preferred_element_type=F32)


def _dot_tn(a, b):
    return lax.dot_general(a, b, (((0,), (0,)), ((), ())), preferred_element_type=F32)


def _sigmoid(x):
    return 1.0 / (1.0 + jnp.exp(-x))


def _silu(x):
    return x * _sigmoid(x)


def _layer_norm(x, g, b):
    mu = jnp.mean(x, axis=-1, keepdims=True)
    xc = x - mu
    var = jnp.mean(xc * xc, axis=-1, keepdims=True)
    return xc * lax.rsqrt(var + NORM_EPS) * g + b


def _ln_kernel(x_ref, g_ref, b_ref, o_ref):
    o_ref[...] = _layer_norm(x_ref[...], g_ref[...], b_ref[...])


def _input_ln(x2d, g, b):
    T = x2d.shape[0]
    tm = min(T, 512)
    return pl.pallas_call(
        _ln_kernel,
        grid=(T // tm,),
        in_specs=[pl.BlockSpec((tm, D_MODEL), lambda i: (i, 0)),
                  pl.BlockSpec((1, D_MODEL), lambda i: (0, 0)),
                  pl.BlockSpec((1, D_MODEL), lambda i: (0, 0))],
        out_specs=pl.BlockSpec((tm, D_MODEL), lambda i: (i, 0)),
        out_shape=jax.ShapeDtypeStruct((T, D_MODEL), F32),
        compiler_params=_cparams("parallel"),
        name="input_ln",
    )(x2d, g.reshape(1, -1), b.reshape(1, -1))


def _attn_proj_kernel(x_ref, w_ref, b_ref, cos_ref, sin_ref, *refs):
    q_ref, k_ref, kb_ref, v_ref, vb_ref, vt_ref = refs[-6:]
    x = x_ref[0].astype(BF16)
    y = _dot(x, w_ref[...]) + b_ref[...]
    cos = cos_ref[...]
    sin = sin_ref[...]
    lane = lax.broadcasted_iota(jnp.int32, cos.shape, 1)
    first_half = (lane % A_HEAD_DIM) < (A_HEAD_DIM // 2)
    for s in range(2 * A_QK_WIDTH // LANES):
        blk = y[:, s * LANES:(s + 1) * LANES]
        swapped = jnp.where(first_half,
                            pltpu.roll(blk, LANES - A_HEAD_DIM // 2, 1),
                            pltpu.roll(blk, A_HEAD_DIM // 2, 1))
        rot = blk * cos + swapped * sin
        if s < A_QK_WIDTH // LANES:
            q_ref[0, :, s * LANES:(s + 1) * LANES] = (rot * Q_SCALE).astype(BF16)
        else:
            o = s * LANES - A_QK_WIDTH
            k_ref[0, 0, :, o:o + LANES] = rot
            kb_ref[0, :, o:o + LANES] = rot.astype(BF16)
    v = y[:, 2 * A_QK_WIDTH:]
    for h in range(A_HEADS):
        v_ref[0, 0, pl.ds(h, v.shape[0], stride=A_HEADS), :] = v[:, h * A_V_DIM:(h + 1) * A_V_DIM]
    vb_ref[0] = v.astype(BF16)
    vt_ref[0] = v.T.astype(BF16)
    for later in range(1, k_ref.shape[0]):
        k_ref[later] = jnp.zeros(k_ref.shape[1:], F32)
        v_ref[later] = jnp.zeros(v_ref.shape[1:], F32)


def _rope_tables(pos0, L):
    half = A_HEAD_DIM // 2
    inv_freq = ROPE_THETA ** (-jnp.arange(half, dtype=F32) / half)
    pos = (pos0 + jnp.arange(L)).astype(F32)
    ang = pos[:, None] * inv_freq[None, :]
    cos = jnp.cos(ang)
    sin = jnp.sin(ang)
    reps = LANES // A_HEAD_DIM
    cos_t = jnp.tile(jnp.concatenate([cos, cos], axis=1), (1, reps))
    sin_t = jnp.tile(jnp.concatenate([-sin, sin], axis=1), (1, reps))
    return cos_t, sin_t


def _attn_proj(x, w, b, pos0, li, depth, carried):
    B, L, _ = x.shape
    tl = min(L, 512)
    cos_t, sin_t = _rope_tables(pos0, L)
    wa = A_QK_WIDTH
    n = 2 * A_QK_WIDTH + A_WIDTH
    row = lambda bi, j: (bi, j, 0)
    layer_row = lambda bi, j: (li, bi, j, 0)
    carried = () if carried is None else tuple(carried)
    n_layers = 1 if carried else depth
    assert carried or li == 0
    n_in = 5
    return pl.pallas_call(
        _attn_proj_kernel,
        grid=(B, L // tl),
        in_specs=[pl.BlockSpec((1, tl, D_MODEL), row),
                  pl.BlockSpec((D_MODEL, n), lambda bi, j: (0, 0)),
                  pl.BlockSpec((1, n), lambda bi, j: (0, 0)),
                  pl.BlockSpec((tl, LANES), lambda bi, j: (j, 0)),
                  pl.BlockSpec((tl, LANES), lambda bi, j: (j, 0))]
        + [pl.BlockSpec(memory_space=pl.ANY)] * len(carried),
        out_specs=[pl.BlockSpec((1, tl, wa), row),
                   pl.BlockSpec((n_layers, 1, tl, wa), layer_row),
                   pl.BlockSpec((1, tl, wa), row),
                   pl.BlockSpec((n_layers, 1, tl * A_HEADS, A_V_DIM), layer_row),
                   pl.BlockSpec((1, tl, wa), row),
                   pl.BlockSpec((1, wa, tl), lambda bi, j: (bi, 0, j))],
        out_shape=[jax.ShapeDtypeStruct((B, L, wa), BF16),
                   jax.ShapeDtypeStruct((depth, B, L, wa), F32),
                   jax.ShapeDtypeStruct((B, L, wa), BF16),
                   jax.ShapeDtypeStruct((depth, B, L * A_HEADS, A_V_DIM), F32),
                   jax.ShapeDtypeStruct((B, L, wa), BF16),
                   jax.ShapeDtypeStruct((B, wa, L), BF16)],
        input_output_aliases={n_in: 1, n_in + 1: 3} if carried else {},
        compiler_params=_cparams("parallel", "parallel"),
        name="attn_proj",
    )(x, w, b, cos_t, sin_t, *carried)


def _split_maps(q):
    lane = lax.broadcasted_iota(jnp.int32, q.shape, 1)
    zero = jnp.zeros_like(q)
    return jnp.where(lane < A_HEAD_DIM, q, zero), jnp.where(lane >= A_HEAD_DIM, q, zero)


def _flash_kernel(lam_ref, g_ref, q_ref, k_ref, vt_ref, o_ref, m_s, l_s, acc_s, sa_s, sb_s, *, tq, out_scale):
    i = pl.program_id(2)
    qz = _split_maps(q_ref[0])
    qq = jnp.concatenate(qz, axis=0)
    m_s[...] = jnp.full(m_s.shape, NEG_INF, F32)
    l_s[...] = jnp.zeros(l_s.shape, F32)
    acc_s[...] = jnp.zeros(acc_s.shape, F32)

    def scores_into(dst, blk):
        off = pl.multiple_of(blk * tq, tq)
        dst[...] = _dot_nt(k_ref[0, pl.ds(off, tq), :], qq)

    def update(src, blk, diagonal):
        off = pl.multiple_of(blk * tq, tq)
        vt = vt_ref[0, :, pl.ds(off, tq)]
        s = src[...]
        if diagonal:
            key = lax.broadcasted_iota(jnp.int32, (tq, 2 * tq), 0) // CHUNK
            qry = (lax.broadcasted_iota(jnp.int32, (tq, 2 * tq), 1) % tq) // CHUNK
            s = jnp.where(key <= qry, s, NEG_INF)
        m_old = m_s[...]
        m_new = jnp.maximum(m_old, jnp.max(s, axis=0, keepdims=True))
        alpha = jnp.exp2(m_old - m_new)
        p = jnp.exp2(s - m_new)
        l_s[...] = alpha * l_s[...] + jnp.sum(p, axis=0, keepdims=True)
        acc_s[...] = alpha * acc_s[...] + _dot(vt, p.astype(BF16))
        m_s[...] = m_new

    scores_into(sa_s, 0)

    def pair(p, carry):
        scores_into(sb_s, 2 * p + 1)
        update(sa_s, 2 * p, False)
        scores_into(sa_s, 2 * p + 2)
        update(sb_s, 2 * p + 1, False)
        return carry

    lax.fori_loop(0, i // 2, pair, 0)

    @pl.when(i % 2 == 1)
    def _():
        scores_into(sb_s, i)
        update(sa_s, i - 1, False)
        update(sb_s, i, True)

    @pl.when(i % 2 == 0)
    def _():
        update(sa_s, i, True)

    inv_l = 1.0 / l_s[...]
    o = acc_s[...] * inv_l
    o = o[:, :tq] - lam_ref[0:1, 0:1] * o[:, tq:]
    ms = jnp.mean(o * o, axis=0, keepdims=True)
    y = o * lax.rsqrt(ms + NORM_EPS) * (g_ref[...] * out_scale)
    o_ref[0] = y.T.astype(o_ref.dtype)


def _flash_prompt(q, kb, vt, lam_row, g_col, out_scale):
    B, L, _ = q.shape
    tq = min(L, 512)
    return pl.pallas_call(
        functools.partial(_flash_kernel, tq=tq, out_scale=out_scale),
        grid=(B, A_HEADS, L // tq),
        in_specs=[pl.BlockSpec((1, LANES), lambda b, h, i: (0, 0)),
                  pl.BlockSpec((A_V_DIM, 1), lambda b, h, i: (0, 0)),
                  pl.BlockSpec((1, tq, LANES), lambda b, h, i: (b, i, h)),
                  pl.BlockSpec((1, L, LANES), lambda b, h, i: (b, 0, h)),
                  pl.BlockSpec((1, A_V_DIM, L), lambda b, h, i: (b, h, 0))],
        out_specs=pl.BlockSpec((1, tq, A_V_DIM), lambda b, h, i: (b, i, h)),
        out_shape=jax.ShapeDtypeStruct((B, L, A_WIDTH), BF16),
        scratch_shapes=[pltpu.VMEM((1, 2 * tq), F32), pltpu.VMEM((1, 2 * tq), F32),
                        pltpu.VMEM((A_V_DIM, 2 * tq), F32),
                        pltpu.VMEM((tq, 2 * tq), F32), pltpu.VMEM((tq, 2 * tq), F32)],
        compiler_params=_cparams("parallel", "parallel", "arbitrary"),
        name="flash_prompt",
    )(lam_row, g_col, q, kb, vt)


def _decode_kernel(lam_ref, g_ref, q_ref, kn_ref, vn_ref, kp_ref, vp_ref, o_ref, s_s, acc_s, m_s, r_s,
                   *, nb, tk, L, out_scale):
    j = pl.program_id(1)
    P = nb * tk
    n_new = kn_ref.shape[1]

    def q_rows():
        rep = jnp.concatenate([q_ref[0]] * (LANES // L), axis=0)
        grp = lax.broadcasted_iota(jnp.int32, rep.shape, 0) // L
        cols = lax.broadcasted_iota(jnp.int32, rep.shape, 1) // A_HEAD_DIM
        return jnp.where(grp == cols, rep, jnp.zeros_like(rep))

    @pl.when(j < nb)
    def _():
        off = pl.multiple_of(j * tk, tk)
        s_s[:, pl.ds(off, tk)] = _dot(q_rows(), kp_ref[0, 0].astype(BF16))

    @pl.when(j == nb - 1)
    def _():
        s_new = _dot_nt(q_rows(), kn_ref[0])
        valid = lax.broadcasted_iota(jnp.int32, s_new.shape, 1) < L
        s_s[:, P:P + n_new] = jnp.where(valid, s_new, NEG_INF)
        s = s_s[...]
        m = jnp.max(s, axis=1, keepdims=True)
        m_s[...] = m
        r_s[...] = 1.0 / jnp.sum(jnp.exp2(s - m), axis=1, keepdims=True)
        acc_s[...] = jnp.zeros(acc_s.shape, F32)

    def weights(cols):
        return (jnp.exp2(s_s[:, cols] - m_s[...]) * r_s[...]).astype(BF16)

    @pl.when(j >= nb)
    def _():
        off = pl.multiple_of((j - nb) * tk, tk)
        w = weights(pl.ds(off, tk))
        for h in range(A_HEADS):
            rows = slice(2 * h * L, (2 * h + 2) * L)
            vh = vp_ref[0, 0, pl.ds(h, tk, stride=A_HEADS), :]
            acc_s[rows, :] += _dot(w[rows, :], vh.astype(BF16))

    @pl.when(j == 2 * nb - 1)
    def _():
        w = weights(slice(P, P + n_new))
        for h in range(A_HEADS):
            hs = slice(h * A_V_DIM, (h + 1) * A_V_DIM)
            rows = slice(2 * h * L, (2 * h + 2) * L)
            acc = acc_s[rows, :] + _dot(w[rows, :], vn_ref[0, :, hs])
            o = acc[:L] - lam_ref[...] * acc[L:]
            ms = jnp.mean(o * o, axis=-1, keepdims=True)
            o_ref[0, :, hs] = (o * lax.rsqrt(ms + NORM_EPS) * g_ref[...] * out_scale).astype(o_ref.dtype)


def _decode_attention(q, k_new, v_new, kt_cache, v_cache, li, lam_row, g_row, out_scale):
    B, L, _ = q.shape
    assert 2 * A_HEADS * L == LANES, "decode attention packs (head, map, frame) onto the 128 lanes"
    P = kt_cache.shape[3]
    tk = min(P, 1024)
    nb = P // tk
    pad = ((0, 0), (0, LANES - L), (0, 0))
    k_new = jnp.pad(k_new, pad)
    v_new = jnp.pad(v_new, pad)
    new = lambda b, j: (b, 0, 0)
    return pl.pallas_call(
        functools.partial(_decode_kernel, nb=nb, tk=tk, L=L, out_scale=out_scale),
        grid=(B, 2 * nb),
        in_specs=[pl.BlockSpec((1, LANES), lambda b, j: (0, 0)),
                  pl.BlockSpec((1, A_V_DIM), lambda b, j: (0, 0)),
                  pl.BlockSpec((1, L, A_QK_WIDTH), new),
                  pl.BlockSpec((1, LANES, A_QK_WIDTH), new),
                  pl.BlockSpec((1, LANES, A_WIDTH), new),
                  pl.BlockSpec((1, 1, A_QK_WIDTH, tk), lambda b, j: (li, b, 0, jnp.minimum(j, nb - 1))),
                  pl.BlockSpec((1, 1, tk * A_HEADS, A_V_DIM),
                               lambda b, j: (li, b, jnp.maximum(j - nb, 0), 0))],
        out_specs=pl.BlockSpec((1, L, A_WIDTH), new),
        out_shape=jax.ShapeDtypeStruct((B, L, A_WIDTH), BF16),
        scratch_shapes=[pltpu.VMEM((LANES, P + LANES), F32), pltpu.VMEM((LANES, A_V_DIM), F32),
                        pltpu.VMEM((LANES, 1), F32), pltpu.VMEM((LANES, 1), F32)],
        compiler_params=_cparams("parallel", "arbitrary"),
        name="decode_attention",
    )(lam_row, g_row, q, k_new, v_new, kt_cache, v_cache)


def _gelu_tanh(x):
    return 0.5 * x * (1.0 + jnp.tanh(math.sqrt(2.0 / math.pi) * (x + 0.044715 * (x * x * x))))


def _rglru_kernel(x_ref, w_ref, b_ref, cw_ref, cb_ref, gw_ref, gbias_ref, sp_ref, cbuf_ref, h0_ref,
                  y_ref, conv_ref, h_ref, xs, hc, hs, *, tl, pos0):
    j = pl.program_id(1)
    tail = CONV_W - 1
    base = 8

    @pl.when(j == 0)
    def _():
        xs[0:base, :] = jnp.zeros((base, B_WIDTH), F32)
        xs[base - tail:base, :] = cbuf_ref[0]
        hc[...] = h0_ref[0]

    x = x_ref[0].astype(BF16)
    y = _dot(x, w_ref[...]) + b_ref[...]
    xb = y[:, :B_WIDTH]
    gb = y[:, B_WIDTH:]
    xs[base:base + tl, :] = xb
    xc = cb_ref[...] + cw_ref[0:1, :] * xs[base - tail:base - tail + tl, :]
    for t in range(1, CONV_W):
        xc = xc + cw_ref[t:t + 1, :] * xs[base - tail + t:base - tail + t + tl, :]
    new_tail = xs[base + tl - tail:base + tl, :]
    xs[base - tail:base, :] = new_tail
    conv_ref[0] = new_tail

    gates = _dot(xc.astype(BF16), gw_ref[...]) + gbias_ref[...]
    r = _sigmoid(gates[:, :B_WIDTH])
    ig = _sigmoid(gates[:, B_WIDTH:])
    log_a = -LRU_C * r * sp_ref[...]
    a = jnp.exp(log_a)
    t = jnp.tanh(log_a)
    mult = jnp.sqrt(-2.0 * t / (1.0 - t))
    row = lax.broadcasted_iota(jnp.int32, (tl, B_WIDTH), 0)
    mult = jnp.where(row + (pos0 + j * tl) == 0, 1.0, mult)
    bb = xc * ig * mult
    a3 = a.reshape(tl // SUBLANES, SUBLANES, B_WIDTH)
    b3 = bb.reshape(tl // SUBLANES, SUBLANES, B_WIDTH)
    sub = lax.broadcasted_iota(jnp.int32, a3.shape, 1)
    d = 1
    while d < SUBLANES:
        keep = sub >= d
        a_sh = pltpu.roll(a3, d, 1)
        b_sh = pltpu.roll(b3, d, 1)
        b3 = jnp.where(keep, a3 * b_sh + b3, b3)
        a3 = jnp.where(keep, a3 * a_sh, a3)
        d *= 2
    carry = hc[...]
    for g in range(tl // SUBLANES):
        hg = a3[g] * carry + b3[g]
        hs[g * SUBLANES:(g + 1) * SUBLANES, :] = hg
        carry = hg[SUBLANES - 1:SUBLANES, :]
    hc[...] = carry
    h_ref[0] = carry
    y_ref[0] = (_gelu_tanh(gb) * hs[...]).astype(y_ref.dtype)


def _rglru(x, w, b, conv_w, conv_b, gate_w, gate_b, sp, conv_buf, h0, pos0):
    B, L, _ = x.shape
    tl = min(L, 256)
    tail = CONV_W - 1
    const = lambda bi, j: (0, 0)
    per_b = lambda bi, j: (bi, 0, 0)
    return pl.pallas_call(
        functools.partial(_rglru_kernel, tl=tl, pos0=pos0),
        grid=(B, L // tl),
        in_specs=[pl.BlockSpec((1, tl, D_MODEL), lambda bi, j: (bi, j, 0)),
                  pl.BlockSpec((D_MODEL, 2 * B_WIDTH), const),
                  pl.BlockSpec((1, 2 * B_WIDTH), const),
                  pl.BlockSpec((CONV_W, B_WIDTH), const),
                  pl.BlockSpec((1, B_WIDTH), const),
                  pl.BlockSpec((B_WIDTH, 2 * B_WIDTH), const),
                  pl.BlockSpec((1, 2 * B_WIDTH), const),
                  pl.BlockSpec((1, B_WIDTH), const),
                  pl.BlockSpec((1, tail, B_WIDTH), per_b),
                  pl.BlockSpec((1, 1, B_WIDTH), per_b)],
        out_specs=[pl.BlockSpec((1, tl, B_WIDTH), lambda bi, j: (bi, j, 0)),
                   pl.BlockSpec((1, tail, B_WIDTH), per_b),
                   pl.BlockSpec((1, 1, B_WIDTH), per_b)],
        out_shape=[jax.ShapeDtypeStruct((B, L, B_WIDTH), BF16),
                   jax.ShapeDtypeStruct((B, tail, B_WIDTH), F32),
                   jax.ShapeDtypeStruct((B, 1, B_WIDTH), F32)],
        scratch_shapes=[pltpu.VMEM((8 + tl, B_WIDTH), F32), pltpu.VMEM((1, B_WIDTH), F32),
                        pltpu.VMEM((tl, B_WIDTH), F32)],
        compiler_params=_cparams("parallel", "arbitrary"),
        name="rglru",
    )(x, w, b, conv_w, conv_b, gate_w, gate_b, sp, conv_buf, h0)


def _hgrn_levels(C):
    return [C >> i for i in range(int(math.log2(C)) + 1)]


HGRN_SPLIT = 3


def _hgrn_sum_matrices(C):
    t = np.arange(C)

    def prefix(m):
        return (t[None, :] >= ((t // m) * m)[:, None]) & (t[None, :] <= t[:, None])

    def suffix(m):
        return (t[None, :] > t[:, None]) & (t[None, :] <= ((t // m) * m + m - 1)[:, None])

    mats = [prefix(C), suffix(C)]
    for m in _hgrn_levels(C)[1:-1]:
        upper = ((t // m) % 2 == 1)[:, None]
        mats.append(np.where(upper, prefix(m), suffix(m)))
    one = np.concatenate(mats, axis=0).astype(np.float32)
    return np.concatenate([one] * HGRN_SPLIT, axis=1)


def _hgrn_kernel(x_ref, w_ref, b_ref, lb_ref, ng_ref, dm_ref, s0_ref, y_ref, s_ref,
                 q_s, k_s, g_s, v_s, gate_s, o_s, st_s, *, tl, C):
    j = pl.program_id(1)
    levels = _hgrn_levels(C)
    nlev = len(levels)

    @pl.when(j == 0)
    def _():
        for h in range(C_HEADS):
            st_s[h] = s0_ref[0, h].T

    x = x_ref[0].astype(BF16)
    y = _dot(x, w_ref[...]) + b_ref[...]
    lb = lb_ref[...]
    z = y[:, C_WIDTH:2 * C_WIDTH]
    f = lb + (1.0 - lb) * _sigmoid(z)
    q_s[...] = _silu(y[:, :C_WIDTH])
    k_s[...] = (1.0 - lb) * _sigmoid(-z)
    g_s[...] = jnp.log(jnp.maximum(f, F_TINY))
    v_s[...] = y[:, 2 * C_WIDTH:3 * C_WIDTH]
    gate_s[...] = _silu(y[:, 3 * C_WIDTH:])

    ti = lax.broadcasted_iota(jnp.int32, (C, C), 0)
    si = lax.broadcasted_iota(jnp.int32, (C, C), 1)
    masks = [ti == si]
    for m in levels[1:]:
        masks.append((ti // (2 * m) == si // (2 * m)) & ((ti // m) % 2 == 1) & ((si // m) % 2 == 0))
    odd_row = lax.broadcasted_iota(jnp.int32, (C, C_WIDTH), 0) % 2 == 1

    def chunk(ci, carry):
        r0 = pl.multiple_of(ci * C, C)
        rows = pl.ds(r0, C)
        g = g_s[rows, :]
        pieces = []
        rest = g
        for _ in range(HGRN_SPLIT):
            piece = rest.astype(BF16)
            pieces.append(piece)
            rest = rest - piece.astype(F32)
        sums = _dot(dm_ref[...], jnp.concatenate(pieces, axis=0))
        e = jnp.exp(jnp.minimum(sums, 0.0))
        e_one = jnp.exp(jnp.where(odd_row, g, 0.0))
        for h in range(C_HEADS):
            hs = slice(h * C_DK, (h + 1) * C_DK)
            q = q_s[rows, hs]
            k = k_s[rows, hs]
            vb = v_s[rows, hs].astype(BF16)
            scores = jnp.where(masks[0], _dot_nt(q.astype(BF16), k.astype(BF16)), 0.0)
            for li in range(1, nlev):
                el = e[(li + 1) * C:(li + 2) * C, hs] if li < nlev - 1 else e_one[:, hs]
                scores = scores + jnp.where(masks[li], _dot_nt((q * el).astype(BF16), (k * el).astype(BF16)), 0.0)
            e_cum = e[0:C, hs]
            qc = (q * e_cum).astype(BF16)
            kc = (k * e[C:2 * C, hs]).astype(BF16)
            st = st_s[h]
            o = _dot(scores.astype(BF16), vb) + _dot_nt(qc, st.astype(BF16))
            st_s[h] = st * e_cum[C - 1:C, :] + _dot_tn(vb, kc)
            ms = jnp.mean(o * o, axis=-1, keepdims=True)
            o_s[rows, hs] = o * lax.rsqrt(ms + NORM_EPS) * ng_ref[...] * gate_s[rows, hs]
        return carry

    lax.fori_loop(0, tl // C, chunk, 0, unroll=True)
    y_ref[0] = o_s[...].astype(y_ref.dtype)
    for h in range(C_HEADS):
        s_ref[0, h] = st_s[h].T


def _hgrn(x, w, b, lb, norm_g, s0):
    B, L, _ = x.shape
    C = min(CHUNK, L)
    tl = min(L, 512)
    dm = jnp.asarray(_hgrn_sum_matrices(C), BF16)
    const = lambda bi, j: (0, 0)
    per_b = lambda bi, j: (bi, 0, 0, 0)
    wide = pltpu.VMEM((tl, C_WIDTH), F32)
    return pl.pallas_call(
        functools.partial(_hgrn_kernel, tl=tl, C=C),
        grid=(B, L // tl),
        in_specs=[pl.BlockSpec((1, tl, D_MODEL), lambda bi, j: (bi, j, 0)),
                  pl.BlockSpec((D_MODEL, 4 * C_WIDTH), const),
                  pl.BlockSpec((1, 4 * C_WIDTH), const),
                  pl.BlockSpec((1, C_WIDTH), const),
                  pl.BlockSpec((1, C_DV), const),
                  pl.BlockSpec(dm.shape, const),
                  pl.BlockSpec((1, C_HEADS, C_DK, C_DV), per_b)],
        out_specs=[pl.BlockSpec((1, tl, C_WIDTH), lambda bi, j: (bi, j, 0)),
                   pl.BlockSpec((1, C_HEADS, C_DK, C_DV), per_b)],
        out_shape=[jax.ShapeDtypeStruct((B, L, C_WIDTH), BF16),
                   jax.ShapeDtypeStruct((B, C_HEADS, C_DK, C_DV), F32)],
        scratch_shapes=[wide, wide, wide, wide, wide, wide, pltpu.VMEM((C_HEADS, C_DV, C_DK), F32)],
        compiler_params=_cparams("parallel", "arbitrary"),
        name="hgrn",
    )(x, w, b, lb, norm_g, dm, s0)


def _merge_kernel(x_ref, ya_ref, yb_ref, yc_ref, wg_ref, bg_ref, wo_ref, g1_ref, b1_ref, rw_ref, rb_ref,
                  o_ref, cls_ref, *, alpha):
    xb = x_ref[...].astype(BF16)
    for c in range(D_MODEL // MERGE_COLS):
        cs = slice(c * MERGE_COLS, (c + 1) * MERGE_COLS)
        pre = alpha * x_ref[:, cs]
        for n, y_ref in enumerate((ya_ref, yb_ref, yc_ref)):
            gs = slice(n * D_MODEL + c * MERGE_COLS, n * D_MODEL + (c + 1) * MERGE_COLS)
            gate = _sigmoid(_dot(xb, wg_ref[:, gs]) + bg_ref[:, gs])
            pre = pre + gate * _dot(y_ref[...], wo_ref[n * A_WIDTH:(n + 1) * A_WIDTH, cs])
        o_ref[:, cs] = pre
    x1 = _layer_norm(o_ref[:, :D_MODEL], g1_ref[...], b1_ref[...])
    o_ref[:, :D_MODEL] = x1

    lg = _dot(x1.astype(BF16), rw_ref[...]) + rb_ref[...]
    col = [lg[:, e:e + 1] for e in range(N_EXPERTS)]
    gmax = [functools.reduce(jnp.maximum, col[g * EXPERTS_PER_GROUP:(g + 1) * EXPERTS_PER_GROUP])
            for g in range(N_GROUPS)]
    best = functools.reduce(jnp.maximum, gmax)
    gs = jnp.where(gmax[0] == best, 0, jnp.where(gmax[1] == best, 1, jnp.where(gmax[2] == best, 2, 3)))
    v = []
    for e in range(EXPERTS_PER_GROUP):
        v.append(jnp.where(gs == 0, col[e],
                           jnp.where(gs == 1, col[EXPERTS_PER_GROUP + e],
                                     jnp.where(gs == 2, col[2 * EXPERTS_PER_GROUP + e],
                                               col[3 * EXPERTS_PER_GROUP + e]))))
    m1 = best
    i1 = jnp.where(v[0] == m1, 0, jnp.where(v[1] == m1, 1, jnp.where(v[2] == m1, 2, 3)))
    v2 = [jnp.where(i1 == e, -jnp.inf, v[e]) for e in range(EXPERTS_PER_GROUP)]
    m2 = functools.reduce(jnp.maximum, v2)
    i2 = jnp.where(v2[0] == m2, 0, jnp.where(v2[1] == m2, 1, jnp.where(v2[2] == m2, 2, 3)))
    e2 = jnp.exp(m2 - m1)
    p1 = 1.0 / (1.0 + e2)
    p2 = e2 / (1.0 + e2)
    ea = jnp.minimum(i1, i2)
    eb = jnp.maximum(i1, i2)
    ga = jnp.where(i1 < i2, p1, p2)
    gb = jnp.where(i1 < i2, p2, p1)
    pair = jnp.where(ea == 0, 0, jnp.where(ea == 1, 3, 5)) + (eb - ea - 1)
    cls = (gs * PAIRS_PER_GROUP + pair).astype(F32)
    lane = lax.broadcasted_iota(jnp.int32, (x_ref.shape[0], AUX_W), 1)
    aux = jnp.where(lane == 0, ga, jnp.where(lane == 1, gb, jnp.where(lane == 2, cls, 0.0)))
    o_ref[:, D_MODEL:] = aux
    pick = (lax.broadcasted_iota(jnp.int32, (8, AUX_W), 0) == 0) & (lax.broadcasted_iota(jnp.int32, (8, AUX_W), 1) == 2)
    cls_ref[...] = _dot_nt(jnp.where(pick, 1.0, 0.0).astype(BF16), aux.astype(BF16))


def _merge(x2d, ya, yb, yc, wg, bg, wo, g1, b1, rw, rb, alpha):
    T = x2d.shape[0]
    tm = min(T, 512)
    row = lambda i: (i, 0)
    const = lambda i: (0, 0)
    return pl.pallas_call(
        functools.partial(_merge_kernel, alpha=alpha),
        grid=(T // tm,),
        in_specs=[pl.BlockSpec((tm, D_MODEL), row),
                  pl.BlockSpec((tm, A_WIDTH), row),
                  pl.BlockSpec((tm, B_WIDTH), row),
                  pl.BlockSpec((tm, C_WIDTH), row),
                  pl.BlockSpec((D_MODEL, N_BRANCH * D_MODEL), const),
                  pl.BlockSpec((1, N_BRANCH * D_MODEL), const),
                  pl.BlockSpec((N_BRANCH * A_WIDTH, D_MODEL), const),
                  pl.BlockSpec((1, D_MODEL), const),
                  pl.BlockSpec((1, D_MODEL), const),
                  pl.BlockSpec((D_MODEL, N_EXPERTS), const),
                  pl.BlockSpec((1, N_EXPERTS), const)],
        out_specs=[pl.BlockSpec((tm, D_MODEL + AUX_W), row), pl.BlockSpec((8, tm), lambda i: (0, i))],
        out_shape=[jax.ShapeDtypeStruct((T, D_MODEL + AUX_W), F32), jax.ShapeDtypeStruct((8, T), F32)],
        compiler_params=_cparams("parallel"),
        name="merge",
    )(x2d, ya, yb, yc, wg, bg, wo, g1, b1, rw, rb)


def _dispatch_kernel(dest_ref, x_ref, out_hbm, zbuf, sem, *, tm, n_tok_steps):
    i = pl.program_id(0)

    def scatter_rows(src):
        for r in range(tm):
            pltpu.make_async_copy(src.at[pl.ds(r, 1), :], out_hbm.at[pl.ds(dest_ref[0, 0, r], 1), :], sem).start()
        pltpu.make_async_copy(src, out_hbm.at[pl.ds(0, tm), :], sem).wait()

    @pl.when(i < n_tok_steps)
    def _():
        scatter_rows(x_ref)

    @pl.when(i >= n_tok_steps)
    def _():
        @pl.when(i == n_tok_steps)
        def _():
            zbuf[...] = jnp.zeros(zbuf.shape, F32)

        scatter_rows(zbuf)


def _dispatch(x1w, dest_all, n_rows):
    T, W = x1w.shape
    tm = min(T, MOE_DMA_ROWS)
    assert T % tm == 0 and dest_all.shape[0] % tm == 0
    n_tok_steps = T // tm
    steps = dest_all.shape[0] // tm
    return pl.pallas_call(
        functools.partial(_dispatch_kernel, tm=tm, n_tok_steps=n_tok_steps),
        grid=(steps,),
        in_specs=[pl.BlockSpec((1, 1, tm), lambda i: (i, 0, 0), memory_space=pltpu.SMEM),
                  pl.BlockSpec((tm, W), lambda i: (jnp.minimum(i, n_tok_steps - 1), 0))],
        out_specs=pl.BlockSpec(memory_space=pl.ANY),
        out_shape=jax.ShapeDtypeStruct((n_rows, W), F32),
        scratch_shapes=[pltpu.VMEM((tm, W), F32), pltpu.SemaphoreType.DMA(())],
        compiler_params=_cparams("arbitrary"),
        name="moe_dispatch",
    )(dest_all.reshape(steps, 1, tm), x1w)


def _experts_kernel(ea_ref, eb_ref, nv_ref, x_ref, w1a, w3a, w2a, w1b, w3b, w2b, g2_ref, b2_ref, o_ref, *, alpha):
    i = pl.program_id(0)

    @pl.when(nv_ref[i] > 0)
    def _():
        xw = x_ref[...]
        x = xw[:, :D_MODEL]
        xb = x.astype(BF16)
        y = None
        for n, (w1, w3, w2) in enumerate(((w1a, w3a, w2a), (w1b, w3b, w2b))):
            hmid = _silu(_dot(xb, w1[0])) * _dot(xb, w3[0])
            ye = xw[:, D_MODEL + n:D_MODEL + n + 1] * _dot(hmid.astype(BF16), w2[0])
            y = ye if y is None else y + ye
        o_ref[...] = _layer_norm(alpha * x + y, g2_ref[...], b2_ref[...])

    @pl.when(nv_ref[i] == 0)
    def _():
        o_ref[...] = jnp.zeros(o_ref.shape, F32)


def _experts(xs, ea, eb, nv, w1, w3, w2, g2, b2, alpha):
    n_rows, W = xs.shape
    wa = lambda i, ea, eb, nv: (ea[i], 0, 0)
    wb = lambda i, ea, eb, nv: (eb[i], 0, 0)
    const = lambda i, ea, eb, nv: (0, 0)
    row = lambda i, ea, eb, nv: (i, 0)
    up = (1, D_MODEL, D_EXPERT)
    down = (1, D_EXPERT, D_MODEL)
    grid_spec = pltpu.PrefetchScalarGridSpec(
        num_scalar_prefetch=3,
        grid=(n_rows // MOE_ROWS,),
        in_specs=[pl.BlockSpec((MOE_ROWS, W), row),
                  pl.BlockSpec(up, wa), pl.BlockSpec(up, wa), pl.BlockSpec(down, wa),
                  pl.BlockSpec(up, wb), pl.BlockSpec(up, wb), pl.BlockSpec(down, wb),
                  pl.BlockSpec((1, D_MODEL), const), pl.BlockSpec((1, D_MODEL), const)],
        out_specs=pl.BlockSpec((MOE_ROWS, D_MODEL), row),
    )
    return pl.pallas_call(
        functools.partial(_experts_kernel, alpha=alpha),
        grid_spec=grid_spec,
        out_shape=jax.ShapeDtypeStruct((n_rows, D_MODEL), F32),
        compiler_params=_cparams("arbitrary"),
        name="moe_experts",
    )(ea, eb, nv, xs, w1, w3, w2, w1, w3, w2, g2, b2)


def _combine_kernel(dest_ref, y_hbm, o_ref, sem, *, tm):
    for r in range(tm):
        pltpu.make_async_copy(y_hbm.at[pl.ds(dest_ref[0, 0, r], 1), :], o_ref.at[pl.ds(r, 1), :], sem).start()
    pltpu.make_async_copy(y_hbm.at[pl.ds(0, tm), :], o_ref, sem).wait()


def _combine(ys, dest, T):
    tm = min(T, MOE_DMA_ROWS)
    return pl.pallas_call(
        functools.partial(_combine_kernel, tm=tm),
        grid=(T // tm,),
        in_specs=[pl.BlockSpec((1, 1, tm), lambda i: (i, 0, 0), memory_space=pltpu.SMEM),
                  pl.BlockSpec(memory_space=pl.ANY)],
        out_specs=pl.BlockSpec((tm, D_MODEL), lambda i: (i, 0)),
        out_shape=jax.ShapeDtypeStruct((T, D_MODEL), F32),
        scratch_shapes=[pltpu.SemaphoreType.DMA(())],
        compiler_params=_cparams("arbitrary"),
        name="moe_combine",
    )(dest.reshape(T // tm, 1, tm), ys)


def _route(cls, T):
    n_blocks = T // MOE_ROWS + N_CLASSES
    n_rows = n_blocks * MOE_ROWS
    rb = math.gcd(T, MOE_ROWS)
    onehot = (cls[:, None] == jnp.arange(N_CLASSES, dtype=jnp.int32)[None, :]).astype(F32).reshape(T // rb, rb, N_CLASSES)
    strict = jnp.asarray(np.tril(np.ones((rb, rb), np.float32), -1))
    rank_in = jnp.einsum('ij,bjc->bic', strict, onehot)
    blk_counts = jnp.sum(onehot, axis=1)
    blk_off = jnp.cumsum(blk_counts, axis=0) - blk_counts
    rank = jnp.sum(onehot * (rank_in + blk_off[:, None, :]), axis=-1).reshape(T).astype(jnp.int32)
    counts = jnp.sum(blk_counts, axis=0).astype(jnp.int32)
    padded = (counts + MOE_ROWS - 1) // MOE_ROWS * MOE_ROWS
    pad_end = jnp.cumsum(padded)
    pad_start = pad_end - padded
    dest = (pad_start[cls] + rank).astype(jnp.int32)
    seg_start = jnp.concatenate([pad_start + counts, pad_end[-1:]])
    seg_len = jnp.concatenate([padded - counts, n_rows - pad_end[-1:]])
    seg_end = jnp.cumsum(seg_len)
    kk = jnp.arange(n_rows - T, dtype=jnp.int32)
    seg = jnp.sum((seg_end[None, :] <= kk[:, None]).astype(jnp.int32), axis=1)
    pad_rows = (seg_start[seg] + kk - (seg_end[seg] - seg_len[seg])).astype(jnp.int32)
    blk_start = jnp.arange(n_blocks, dtype=jnp.int32) * MOE_ROWS
    blk_cls = jnp.minimum(jnp.sum((pad_end[None, :] <= blk_start[:, None]).astype(jnp.int32), axis=1), N_CLASSES - 1)
    nv = jnp.clip(pad_start[blk_cls] + counts[blk_cls] - blk_start, 0, MOE_ROWS).astype(jnp.int32)
    grp = blk_cls // PAIRS_PER_GROUP
    pair = blk_cls % PAIRS_PER_GROUP
    pa = jnp.asarray([0, 0, 0, 1, 1, 2], jnp.int32)[pair]
    pb = jnp.asarray([1, 2, 3, 2, 3, 3], jnp.int32)[pair]
    ea = grp * EXPERTS_PER_GROUP + pa
    eb = grp * EXPERTS_PER_GROUP + pb
    return ea, eb, nv, dest, pad_rows, n_rows


def _moe(x1w, cls_rows, w1, w3, w2, g2, b2, alpha):
    T = x1w.shape[0]
    cls = cls_rows[0].astype(jnp.int32)
    ea, eb, nv, dest, pad_rows, n_rows = _route(cls, T)
    xs = _dispatch(x1w, jnp.concatenate([dest, pad_rows]), n_rows)
    ys = _experts(xs, ea, eb, nv, w1, w3, w2, g2, b2, alpha)
    return _combine(ys, dest, T)


def _block_diag(w):
    n, d, _ = w.shape
    eye = jnp.eye(n, dtype=w.dtype)
    return jnp.einsum('nde,nm->ndme', w, eye).reshape(n * d, n * d)


def _layer(x, pos0, li, P, past, kv_carried):
    B, L, _ = x.shape
    T = B * L
    depth = P['depth']
    alpha = (2 * depth) ** 0.25
    w_in, b_in = P['w_in'][li], P['b_in'][li]

    def cols(lo, hi):
        return w_in[:, lo:hi], b_in[lo:hi].reshape(1, -1)

    lam_init = 0.8 - 0.6 * math.exp(-0.3 * li)
    dl = P['diff_lambda'][li]
    lam = jnp.exp(jnp.sum(dl[0] * dl[1])) - jnp.exp(jnp.sum(dl[2] * dl[3])) + lam_init
    lam_row = jnp.full((1, LANES), lam, F32)
    subln_g = P['diff_subln_g'][li].reshape(1, -1)

    wa, ba = cols(OFF_A, OFF_B)
    q, k_all, kb, v_all, vb, vt = _attn_proj(x, wa, ba, pos0, li, depth, kv_carried)
    if past is None:
        y_a = _flash_prompt(q, kb, vt, lam_row, subln_g.reshape(-1, 1), 1.0 - lam_init)
        conv_buf = jnp.zeros((B, CONV_W - 1, B_WIDTH), F32)
        h0 = jnp.zeros((B, 1, B_WIDTH), F32)
        s0 = jnp.zeros((B, C_HEADS, C_DK, C_DV), F32)
    else:
        kt_cache, v_cache, conv_buf, h0, s0 = past
        y_a = _decode_attention(q, kb, vb, kt_cache, v_cache, li, lam_row, subln_g, 1.0 - lam_init)
        h0 = h0.reshape(B, 1, B_WIDTH)

    wb, bb = cols(OFF_B, OFF_C)
    y_b, conv_new, h_new = _rglru(x, wb, bb, P['conv_w'][li], P['conv_b'][li].reshape(1, -1),
                                  P['lru_gate_w'][li], P['lru_gate_b'][li], P['lru_sp'][li].reshape(1, -1),
                                  conv_buf, h0, pos0)
    wc, bc = cols(OFF_C, OFF_G)
    y_c, s_new = _hgrn(x, wc, bc, P['lower_bounds'][li].reshape(1, -1), P['hgrn_norm_g'][li].reshape(1, -1), s0)

    wg, bg = cols(OFF_G, N_IN)
    x1w, cls_rows = _merge(x.reshape(T, D_MODEL), y_a.reshape(T, A_WIDTH), y_b.reshape(T, B_WIDTH),
                 y_c.reshape(T, C_WIDTH), wg, bg, P['w_out'][li], P['ln1_g'][li].reshape(1, -1),
                 P['ln1_b'][li].reshape(1, -1), P['router_w'], P['router_b'].reshape(1, -1), alpha)
    x2 = _moe(x1w, cls_rows, P['moe_w1'][li], P['moe_w3'][li], P['moe_w2'][li], P['ln2_g'][li].reshape(1, -1),
              P['ln2_b'][li].reshape(1, -1), alpha)
    states = (conv_new, h_new.reshape(B, B_WIDTH), s_new)
    return x2.reshape(B, L, D_MODEL), (k_all, v_all), states


def kernel(x_prompt, x_sample, cache_k, cache_v, state_conv, state_lru, state_hgrn, ln_in_g, ln_in_b, w_in, b_in, diff_lambda, diff_subln_g, conv_w, conv_b, lru_wa, lru_ba, lru_wx, lru_bx, lru_lambda, hgrn_lb, hgrn_norm_g, w_out, ln1_g, ln1_b, router_w, router_b, moe_w1, moe_w3, moe_w2, ln2_g, ln2_b):
    depth = w_in.shape[0]
    p_lb = jax.nn.softmax(hgrn_lb.astype(F32), axis=0)
    P = dict(
        depth=depth,
        w_in=w_in.astype(BF16), b_in=b_in,
        diff_lambda=diff_lambda.astype(F32), diff_subln_g=diff_subln_g,
        conv_w=conv_w, conv_b=conv_b,
        lru_gate_w=jnp.stack([jnp.concatenate([_block_diag(lru_wa[l]), _block_diag(lru_wx[l])], axis=1)
                              for l in range(depth)]).astype(BF16),
        lru_gate_b=jnp.concatenate([lru_ba, lru_bx], axis=1).reshape(depth, 1, -1),
        lru_sp=jax.nn.softplus(-lru_lambda.astype(F32)),
        lower_bounds=jnp.cumsum(p_lb, axis=0) - p_lb[0],
        hgrn_norm_g=hgrn_norm_g,
        w_out=w_out.astype(BF16), ln1_g=ln1_g, ln1_b=ln1_b,
        router_w=router_w.astype(BF16), router_b=router_b,
        moe_w1=moe_w1.astype(BF16), moe_w3=moe_w3.astype(BF16), moe_w2=moe_w2.astype(BF16),
        ln2_g=ln2_g, ln2_b=ln2_b,
    )
    Bp, Lp, _ = x_prompt.shape
    Bs, Ls, _ = x_sample.shape
    past_len = cache_k.shape[2]
    kt_cache = jnp.transpose(cache_k, (0, 1, 3, 4, 5, 2)).reshape(depth, Bs, A_QK_WIDTH, past_len)
    v_cache = cache_v.reshape(depth, Bs, past_len * A_HEADS, A_V_DIM)
    xp = _input_ln(x_prompt.reshape(Bp * Lp, D_MODEL), ln_in_g, ln_in_b).reshape(Bp, Lp, D_MODEL)
    xs = _input_ln(x_sample.reshape(Bs * Ls, D_MODEL), ln_in_g, ln_in_b).reshape(Bs, Ls, D_MODEL)
    prompt_states, sample_states = [], []
    kv_p = kv_s = None
    for li in range(depth):
        xp, kv_p, st_p = _layer(xp, 0, li, P, None, kv_p)
        xs, kv_s, st_s = _layer(xs, past_len, li, P,
                                (kt_cache, v_cache, state_conv[li], state_lru[li], state_hgrn[li]), kv_s)
        prompt_states.append(st_p)
        sample_states.append(st_s)
    conv_p, lru_p, hgrn_p = [jnp.stack(t) for t in zip(*prompt_states)]
    conv_s, lru_s, hgrn_s = [jnp.stack(t) for t in zip(*sample_states)]
    k_p = kv_p[0].reshape(depth, Bp, Lp, A_HEADS, 2, A_HEAD_DIM)
    v_p = kv_p[1].reshape(depth, Bp, Lp, A_HEADS, A_V_DIM)
    k_s = kv_s[0].reshape(depth, Bs, Ls, A_HEADS, 2, A_HEAD_DIM)
    v_s = kv_s[1].reshape(depth, Bs, Ls, A_HEADS, A_V_DIM)
    return (xp, xs, k_p, v_p, conv_p, lru_p, hgrn_p, k_s, v_s, conv_s, lru_s, hgrn_s)
```

```python
import functools
import math

import numpy as np
import jax
import jax.numpy as jnp
from jax import lax
from jax.experimental import pallas as pl
from jax.experimental.pallas import tpu as pltpu

F32 = jnp.float32
BF16 = jnp.bfloat16

D_MODEL = 1024
CHUNK = 64
A_HEADS = 4
A_HEAD_DIM = 64
A_V_DIM = 2 * A_HEAD_DIM
A_QK_WIDTH = A_HEADS * 2 * A_HEAD_DIM
A_WIDTH = A_HEADS * A_V_DIM
ROPE_THETA = 10000.0
NEG_INF = -1e30
Q_SCALE = A_HEAD_DIM ** -0.5 * math.log2(math.e)
B_WIDTH = 512
B_BLOCKS = 8
CONV_W = 4
LRU_C = 8.0
C_HEADS = 4
C_DK = 128
C_DV = 128
C_WIDTH = C_HEADS * C_DV
F_TINY = 1e-30
N_BRANCH = 3
N_EXPERTS = 16
N_GROUPS = 4
EXPERTS_PER_GROUP = N_EXPERTS // N_GROUPS
D_EXPERT = 512
NORM_EPS = 1e-5

LANES = 128
SUBLANES = 8
PAIRS_PER_GROUP = 6
N_CLASSES = N_GROUPS * PAIRS_PER_GROUP
MOE_ROWS = 256
MERGE_COLS = 256
MERGE_ROWS = 512
MERGE_GROUP = 512
MOE_DMA_ROWS = 1024
AUX_W = LANES
VMEM_LIMIT = 56 * 1024 * 1024

OFF_A = 0
OFF_B = OFF_A + 2 * A_QK_WIDTH + A_WIDTH
OFF_C = OFF_B + 2 * B_WIDTH
OFF_G = OFF_C + 4 * C_WIDTH
N_IN = OFF_G + N_BRANCH * D_MODEL


def _cparams(*sem):
    return pltpu.CompilerParams(dimension_semantics=sem, vmem_limit_bytes=VMEM_LIMIT)


def _dot(a, b):
    return jnp.dot(a, b, preferred_element_type=F32)


def _dot_nt(a, b):
    return lax.dot_general(a, b, (((1,), (1,)), ((), ())), preferred_element_type=F32)


def _dot_tn(a, b):
    return lax.dot_general(a, b, (((0,), (0,)), ((), ())), preferred_element_type=F32)


def _sigmoid(x):
    return 1.0 / (1.0 + jnp.exp(-x))


def _silu(x):
    return x * _sigmoid(x)


def _layer_norm(x, g, b):
    mu = jnp.mean(x, axis=-1, keepdims=True)
    xc = x - mu
    var = jnp.mean(xc * xc, axis=-1, keepdims=True)
    return xc * lax.rsqrt(var + NORM_EPS) * g + b


def _ln_kernel(x_ref, g_ref, b_ref, o_ref):
    o_ref[...] = _layer_norm(x_ref[...], g_ref[...], b_ref[...])


def _input_ln(x2d, g, b):
    T = x2d.shape[0]
    tm = min(T, 512)
    return pl.pallas_call(
        _ln_kernel,
        grid=(T // tm,),
        in_specs=[pl.BlockSpec((tm, D_MODEL), lambda i: (i, 0)),
                  pl.BlockSpec((1, D_MODEL), lambda i: (0, 0)),
                  pl.BlockSpec((1, D_MODEL), lambda i: (0, 0))],
        out_specs=pl.BlockSpec((tm, D_MODEL), lambda i: (i, 0)),
        out_shape=jax.ShapeDtypeStruct((T, D_MODEL), F32),
        compiler_params=_cparams("parallel"),
        name="input_ln",
    )(x2d, g.reshape(1, -1), b.reshape(1, -1))


def _attn_proj_kernel(x_ref, w_ref, b_ref, cos_ref, sin_ref, *refs):
    q_ref, k_ref, kb_ref, v_ref, vb_ref, vt_ref = refs[-6:]
    x = x_ref[0].astype(BF16)
    y = _dot(x, w_ref[...]) + b_ref[...]
    cos = cos_ref[...]
    sin = sin_ref[...]
    lane = lax.broadcasted_iota(jnp.int32, cos.shape, 1)
    first_half = (lane % A_HEAD_DIM) < (A_HEAD_DIM // 2)
    for s in range(2 * A_QK_WIDTH // LANES):
        blk = y[:, s * LANES:(s + 1) * LANES]
        swapped = jnp.where(first_half,
                            pltpu.roll(blk, LANES - A_HEAD_DIM // 2, 1),
                            pltpu.roll(blk, A_HEAD_DIM // 2, 1))
        rot = blk * cos + swapped * sin
        if s < A_QK_WIDTH // LANES:
            q_ref[0, :, s * LANES:(s + 1) * LANES] = (rot * Q_SCALE).astype(BF16)
        else:
            o = s * LANES - A_QK_WIDTH
            k_ref[0, 0, :, o:o + LANES] = rot
            kb_ref[0, :, o:o + LANES] = rot.astype(BF16)
    v = y[:, 2 * A_QK_WIDTH:]
    for h in range(A_HEADS):
        v_ref[0, 0, pl.ds(h, v.shape[0], stride=A_HEADS), :] = v[:, h * A_V_DIM:(h + 1) * A_V_DIM]
    vb_ref[0] = v.astype(BF16)
    vt_ref[0] = v.T.astype(BF16)
    for later in range(1, k_ref.shape[0]):
        k_ref[later] = jnp.zeros(k_ref.shape[1:], F32)
        v_ref[later] = jnp.zeros(v_ref.shape[1:], F32)


def _rope_tables(pos0, L):
    half = A_HEAD_DIM // 2
    inv_freq = ROPE_THETA ** (-jnp.arange(half, dtype=F32) / half)
    pos = (pos0 + jnp.arange(L)).astype(F32)
    ang = pos[:, None] * inv_freq[None, :]
    cos = jnp.cos(ang)
    sin = jnp.sin(ang)
    reps = LANES // A_HEAD_DIM
    cos_t = jnp.tile(jnp.concatenate([cos, cos], axis=1), (1, reps))
    sin_t = jnp.tile(jnp.concatenate([-sin, sin], axis=1), (1, reps))
    return cos_t, sin_t


def _attn_proj(x, w, b, pos0, li, depth, carried):
    B, L, _ = x.shape
    tl = min(L, 512)
    cos_t, sin_t = _rope_tables(pos0, L)
    wa = A_QK_WIDTH
    n = 2 * A_QK_WIDTH + A_WIDTH
    row = lambda bi, j: (bi, j, 0)
    layer_row = lambda bi, j: (li, bi, j, 0)
    carried = () if carried is None else tuple(carried)
    n_layers = 1 if carried else depth
    assert carried or li == 0
    n_in = 5
    return pl.pallas_call(
        _attn_proj_kernel,
        grid=(B, L // tl),
        in_specs=[pl.BlockSpec((1, tl, D_MODEL), row),
                  pl.BlockSpec((D_MODEL, n), lambda bi, j: (0, 0)),
                  pl.BlockSpec((1, n), lambda bi, j: (0, 0)),
                  pl.BlockSpec((tl, LANES), lambda bi, j: (j, 0)),
                  pl.BlockSpec((tl, LANES), lambda bi, j: (j, 0))]
        + [pl.BlockSpec(memory_space=pl.ANY)] * len(carried),
        out_specs=[pl.BlockSpec((1, tl, wa), row),
                   pl.BlockSpec((n_layers, 1, tl, wa), layer_row),
                   pl.BlockSpec((1, tl, wa), row),
                   pl.BlockSpec((n_layers, 1, tl * A_HEADS, A_V_DIM), layer_row),
                   pl.BlockSpec((1, tl, wa), row),
                   pl.BlockSpec((1, wa, tl), lambda bi, j: (bi, 0, j))],
        out_shape=[jax.ShapeDtypeStruct((B, L, wa), BF16),
                   jax.ShapeDtypeStruct((depth, B, L, wa), F32),
                   jax.ShapeDtypeStruct((B, L, wa), BF16),
                   jax.ShapeDtypeStruct((depth, B, L * A_HEADS, A_V_DIM), F32),
                   jax.ShapeDtypeStruct((B, L, wa), BF16),
                   jax.ShapeDtypeStruct((B, wa, L), BF16)],
        input_output_aliases={n_in: 1, n_in + 1: 3} if carried else {},
        compiler_params=_cparams("parallel", "parallel"),
        name="attn_proj",
    )(x, w, b, cos_t, sin_t, *carried)


def _split_maps(q):
    lane = lax.broadcasted_iota(jnp.int32, q.shape, 1)
    zero = jnp.zeros_like(q)
    return jnp.where(lane < A_HEAD_DIM, q, zero), jnp.where(lane >= A_HEAD_DIM, q, zero)


def _flash_kernel(lam_ref, g_ref, q_ref, k_ref, vt_ref, o_ref, m_s, l_s, acc_s, sa_s, sb_s, *, tq, out_scale):
    i = pl.program_id(2)
    qz = _split_maps(q_ref[0])
    qq = jnp.concatenate(qz, axis=0)
    m_s[...] = jnp.full(m_s.shape, NEG_INF, F32)
    l_s[...] = jnp.zeros(l_s.shape, F32)
    acc_s[...] = jnp.zeros(acc_s.shape, F32)

    def scores_into(dst, blk):
        off = pl.multiple_of(blk * tq, tq)
        dst[...] = _dot_nt(k_ref[0, pl.ds(off, tq), :], qq)

    def update(src, blk, diagonal):
        off = pl.multiple_of(blk * tq, tq)
        vt = vt_ref[0, :, pl.ds(off, tq)]
        s = src[...]
        if diagonal:
            key = lax.broadcasted_iota(jnp.int32, (tq, 2 * tq), 0) // CHUNK
            qry = (lax.broadcasted_iota(jnp.int32, (tq, 2 * tq), 1) % tq) // CHUNK
            s = jnp.where(key <= qry, s, NEG_INF)
        m_old = m_s[...]
        m_new = jnp.maximum(m_old, jnp.max(s, axis=0, keepdims=True))
        alpha = jnp.exp2(m_old - m_new)
        p = jnp.exp2(s - m_new)
        l_s[...] = alpha * l_s[...] + jnp.sum(p, axis=0, keepdims=True)
        acc_s[...] = alpha * acc_s[...] + _dot(vt, p.astype(BF16))
        m_s[...] = m_new

    scores_into(sa_s, 0)

    def pair(p):
        scores_into(sb_s, 2 * p + 1)
        update(sa_s, 2 * p, False)
        scores_into(sa_s, 2 * p + 2)
        update(sb_s, 2 * p + 1, False)

    def two_pairs(t, carry):
        pair(2 * t)
        pair(2 * t + 1)
        return carry

    n_pairs = i // 2
    lax.fori_loop(0, n_pairs // 2, two_pairs, 0)

    @pl.when(n_pairs % 2 == 1)
    def _():
        pair(n_pairs - 1)

    @pl.when(i % 2 == 1)
    def _():
        scores_into(sb_s, i)
        update(sa_s, i - 1, False)
        update(sb_s, i, True)

    @pl.when(i % 2 == 0)
    def _():
        update(sa_s, i, True)

    inv_l = 1.0 / l_s[...]
    o = acc_s[...] * inv_l
    o = o[:, :tq] - lam_ref[0:1, 0:1] * o[:, tq:]
    ms = jnp.mean(o * o, axis=0, keepdims=True)
    y = o * lax.rsqrt(ms + NORM_EPS) * (g_ref[...] * out_scale)
    o_ref[0] = y.T.astype(o_ref.dtype)


def _flash_prompt(q, kb, vt, lam_row, g_col, out_scale):
    B, L, _ = q.shape
    tq = min(L, 512)
    return pl.pallas_call(
        functools.partial(_flash_kernel, tq=tq, out_scale=out_scale),
        grid=(B, A_HEADS, L // tq),
        in_specs=[pl.BlockSpec((1, LANES), lambda b, h, i: (0, 0)),
                  pl.BlockSpec((A_V_DIM, 1), lambda b, h, i: (0, 0)),
                  pl.BlockSpec((1, tq, LANES), lambda b, h, i: (b, i, h)),
                  pl.BlockSpec((1, L, LANES), lambda b, h, i: (b, 0, h)),
                  pl.BlockSpec((1, A_V_DIM, L), lambda b, h, i: (b, h, 0))],
        out_specs=pl.BlockSpec((1, tq, A_V_DIM), lambda b, h, i: (b, i, h)),
        out_shape=jax.ShapeDtypeStruct((B, L, A_WIDTH), BF16),
        scratch_shapes=[pltpu.VMEM((1, 2 * tq), F32), pltpu.VMEM((1, 2 * tq), F32),
                        pltpu.VMEM((A_V_DIM, 2 * tq), F32),
                        pltpu.VMEM((tq, 2 * tq), F32), pltpu.VMEM((tq, 2 * tq), F32)],
        compiler_params=_cparams("parallel", "parallel", "arbitrary"),
        name="flash_prompt",
    )(lam_row, g_col, q, kb, vt)


def _decode_kernel(lam_ref, g_ref, q_ref, kn_ref, vn_ref, kp_ref, vp_ref, o_ref, s_s, acc_s, m_s, r_s,
                   *, nb, tk, L, out_scale):
    j = pl.program_id(1)
    P = nb * tk
    n_new = kn_ref.shape[1]

    def q_rows():
        rep = jnp.concatenate([q_ref[0]] * (LANES // L), axis=0)
        grp = lax.broadcasted_iota(jnp.int32, rep.shape, 0) // L
        cols = lax.broadcasted_iota(jnp.int32, rep.shape, 1) // A_HEAD_DIM
        return jnp.where(grp == cols, rep, jnp.zeros_like(rep))

    @pl.when(j < nb)
    def _():
        off = pl.multiple_of(j * tk, tk)
        s_s[:, pl.ds(off, tk)] = _dot(q_rows(), kp_ref[0, 0].astype(BF16))

    @pl.when(j == nb - 1)
    def _():
        s_new = _dot_nt(q_rows(), kn_ref[0])
        valid = lax.broadcasted_iota(jnp.int32, s_new.shape, 1) < L
        s_s[:, P:P + n_new] = jnp.where(valid, s_new, NEG_INF)
        s = s_s[...]
        m = jnp.max(s, axis=1, keepdims=True)
        m_s[...] = m
        r_s[...] = 1.0 / jnp.sum(jnp.exp2(s - m), axis=1, keepdims=True)
        acc_s[...] = jnp.zeros(acc_s.shape, F32)

    def weights(cols):
        return (jnp.exp2(s_s[:, cols] - m_s[...]) * r_s[...]).astype(BF16)

    @pl.when(j >= nb)
    def _():
        off = pl.multiple_of((j - nb) * tk, tk)
        w = weights(pl.ds(off, tk))
        for h in range(A_HEADS):
            rows = slice(2 * h * L, (2 * h + 2) * L)
            vh = vp_ref[0, 0, pl.ds(h, tk, stride=A_HEADS), :]
            acc_s[rows, :] += _dot(w[rows, :], vh.astype(BF16))

    @pl.when(j == 2 * nb - 1)
    def _():
        w = weights(slice(P, P + n_new))
        for h in range(A_HEADS):
            hs = slice(h * A_V_DIM, (h + 1) * A_V_DIM)
            rows = slice(2 * h * L, (2 * h + 2) * L)
            acc = acc_s[rows, :] + _dot(w[rows, :], vn_ref[0, :, hs])
            o = acc[:L] - lam_ref[...] * acc[L:]
            ms = jnp.mean(o * o, axis=-1, keepdims=True)
            o_ref[0, :, hs] = (o * lax.rsqrt(ms + NORM_EPS) * g_ref[...] * out_scale).astype(o_ref.dtype)


def _decode_attention(q, k_new, v_new, kt_cache, v_cache, li, lam_row, g_row, out_scale):
    B, L, _ = q.shape
    assert 2 * A_HEADS * L == LANES, "decode attention packs (head, map, frame) onto the 128 lanes"
    P = kt_cache.shape[3]
    tk = min(P, 1024)
    nb = P // tk
    pad = ((0, 0), (0, LANES - L), (0, 0))
    k_new = jnp.pad(k_new, pad)
    v_new = jnp.pad(v_new, pad)
    new = lambda b, j: (b, 0, 0)
    return pl.pallas_call(
        functools.partial(_decode_kernel, nb=nb, tk=tk, L=L, out_scale=out_scale),
        grid=(B, 2 * nb),
        in_specs=[pl.BlockSpec((1, LANES), lambda b, j: (0, 0)),
                  pl.BlockSpec((1, A_V_DIM), lambda b, j: (0, 0)),
                  pl.BlockSpec((1, L, A_QK_WIDTH), new),
                  pl.BlockSpec((1, LANES, A_QK_WIDTH), new),
                  pl.BlockSpec((1, LANES, A_WIDTH), new),
                  pl.BlockSpec((1, 1, A_QK_WIDTH, tk), lambda b, j: (li, b, 0, jnp.minimum(j, nb - 1))),
                  pl.BlockSpec((1, 1, tk * A_HEADS, A_V_DIM),
                               lambda b, j: (li, b, jnp.maximum(j - nb, 0), 0))],
        out_specs=pl.BlockSpec((1, L, A_WIDTH), new),
        out_shape=jax.ShapeDtypeStruct((B, L, A_WIDTH), BF16),
        scratch_shapes=[pltpu.VMEM((LANES, P + LANES), F32), pltpu.VMEM((LANES, A_V_DIM), F32),
                        pltpu.VMEM((LANES, 1), F32), pltpu.VMEM((LANES, 1), F32)],
        compiler_params=_cparams("parallel", "arbitrary"),
        name="decode_attention",
    )(lam_row, g_row, q, k_new, v_new, kt_cache, v_cache)


def _gelu_tanh(x):
    return 0.5 * x * (1.0 + jnp.tanh(math.sqrt(2.0 / math.pi) * (x + 0.044715 * (x * x * x))))


def _rglru_kernel(x_ref, w_ref, b_ref, cw_ref, cb_ref, gw_ref, gbias_ref, sp_ref, cbuf_ref, h0_ref,
                  y_ref, conv_ref, h_ref, xs, hc, hs, *, tl, pos0):
    j = pl.program_id(1)
    tail = CONV_W - 1
    base = 8

    @pl.when(j == 0)
    def _():
        xs[0:base, :] = jnp.zeros((base, B_WIDTH), F32)
        xs[base - tail:base, :] = cbuf_ref[0]
        hc[...] = h0_ref[0]

    x = x_ref[0].astype(BF16)
    y = _dot(x, w_ref[...]) + b_ref[...]
    xb = y[:, :B_WIDTH]
    gb = y[:, B_WIDTH:]
    xs[base:base + tl, :] = xb
    xc = cb_ref[...] + cw_ref[0:1, :] * xs[base - tail:base - tail + tl, :]
    for t in range(1, CONV_W):
        xc = xc + cw_ref[t:t + 1, :] * xs[base - tail + t:base - tail + t + tl, :]
    new_tail = xs[base + tl - tail:base + tl, :]
    xs[base - tail:base, :] = new_tail
    conv_ref[0] = new_tail

    gates = _dot(xc.astype(BF16), gw_ref[...]) + gbias_ref[...]
    r = _sigmoid(gates[:, :B_WIDTH])
    ig = _sigmoid(gates[:, B_WIDTH:])
    log_a = -LRU_C * r * sp_ref[...]
    a = jnp.exp(log_a)
    t = jnp.tanh(log_a)
    mult = jnp.sqrt(-2.0 * t / (1.0 - t))
    row = lax.broadcasted_iota(jnp.int32, (tl, B_WIDTH), 0)
    mult = jnp.where(row + (pos0 + j * tl) == 0, 1.0, mult)
    bb = xc * ig * mult
    a3 = a.reshape(tl // SUBLANES, SUBLANES, B_WIDTH)
    b3 = bb.reshape(tl // SUBLANES, SUBLANES, B_WIDTH)
    sub = lax.broadcasted_iota(jnp.int32, a3.shape, 1)
    d = 1
    while d < SUBLANES:
        keep = sub >= d
        a_sh = pltpu.roll(a3, d, 1)
        b_sh = pltpu.roll(b3, d, 1)
        b3 = jnp.where(keep, a3 * b_sh + b3, b3)
        a3 = jnp.where(keep, a3 * a_sh, a3)
        d *= 2
    carry = hc[...]
    for g in range(tl // SUBLANES):
        hg = a3[g] * carry + b3[g]
        hs[g * SUBLANES:(g + 1) * SUBLANES, :] = hg
        carry = hg[SUBLANES - 1:SUBLANES, :]
    hc[...] = carry
    h_ref[0] = carry
    y_ref[0] = (_gelu_tanh(gb) * hs[...]).astype(y_ref.dtype)


def _rglru(x, w, b, conv_w, conv_b, gate_w, gate_b, sp, conv_buf, h0, pos0):
    B, L, _ = x.shape
    tl = min(L, 256)
    tail = CONV_W - 1
    const = lambda bi, j: (0, 0)
    per_b = lambda bi, j: (bi, 0, 0)
    return pl.pallas_call(
        functools.partial(_rglru_kernel, tl=tl, pos0=pos0),
        grid=(B, L // tl),
        in_specs=[pl.BlockSpec((1, tl, D_MODEL), lambda bi, j: (bi, j, 0)),
                  pl.BlockSpec((D_MODEL, 2 * B_WIDTH), const),
                  pl.BlockSpec((1, 2 * B_WIDTH), const),
                  pl.BlockSpec((CONV_W, B_WIDTH), const),
                  pl.BlockSpec((1, B_WIDTH), const),
                  pl.BlockSpec((B_WIDTH, 2 * B_WIDTH), const),
                  pl.BlockSpec((1, 2 * B_WIDTH), const),
                  pl.BlockSpec((1, B_WIDTH), const),
                  pl.BlockSpec((1, tail, B_WIDTH), per_b),
                  pl.BlockSpec((1, 1, B_WIDTH), per_b)],
        out_specs=[pl.BlockSpec((1, tl, B_WIDTH), lambda bi, j: (bi, j, 0)),
                   pl.BlockSpec((1, tail, B_WIDTH), per_b),
                   pl.BlockSpec((1, 1, B_WIDTH), per_b)],
        out_shape=[jax.ShapeDtypeStruct((B, L, B_WIDTH), BF16),
                   jax.ShapeDtypeStruct((B, tail, B_WIDTH), F32),
                   jax.ShapeDtypeStruct((B, 1, B_WIDTH), F32)],
        scratch_shapes=[pltpu.VMEM((8 + tl, B_WIDTH), F32), pltpu.VMEM((1, B_WIDTH), F32),
                        pltpu.VMEM((tl, B_WIDTH), F32)],
        compiler_params=_cparams("parallel", "arbitrary"),
        name="rglru",
    )(x, w, b, conv_w, conv_b, gate_w, gate_b, sp, conv_buf, h0)


def _hgrn_levels(C):
    return [C >> i for i in range(int(math.log2(C)) + 1)]


HGRN_SPLIT = 3


def _hgrn_sum_matrices(C):
    t = np.arange(C)

    def prefix(m):
        return (t[None, :] >= ((t // m) * m)[:, None]) & (t[None, :] <= t[:, None])

    def suffix(m):
        return (t[None, :] > t[:, None]) & (t[None, :] <= ((t // m) * m + m - 1)[:, None])

    mats = [prefix(C), suffix(C)]
    for m in _hgrn_levels(C)[1:-1]:
        upper = ((t // m) % 2 == 1)[:, None]
        mats.append(np.where(upper, prefix(m), suffix(m)))
    one = np.concatenate(mats, axis=0).astype(np.float32)
    return np.concatenate([one] * HGRN_SPLIT, axis=1)


def _hgrn_kernel(x_ref, w_ref, b_ref, lb_ref, ng_ref, dm_ref, s0_ref, y_ref, s_ref, st_s, *, tl, C):
    j = pl.program_id(1)
    levels = _hgrn_levels(C)
    nlev = len(levels)

    @pl.when(j == 0)
    def _():
        for h in range(C_HEADS):
            st_s[h] = s0_ref[0, h].T

    x = x_ref[0].astype(BF16)
    y = _dot(x, w_ref[...]) + b_ref[...]
    lb = lb_ref[...]
    z = y[:, C_WIDTH:2 * C_WIDTH]
    f = lb + (1.0 - lb) * _sigmoid(z)
    q_all = _silu(y[:, :C_WIDTH])
    k_all = (1.0 - lb) * _sigmoid(-z)
    g_all = jnp.log(jnp.maximum(f, F_TINY))
    v_all = y[:, 2 * C_WIDTH:3 * C_WIDTH]
    gate_all = _silu(y[:, 3 * C_WIDTH:])
    states = [st_s[h] for h in range(C_HEADS)]

    ti = lax.broadcasted_iota(jnp.int32, (C, C), 0)
    si = lax.broadcasted_iota(jnp.int32, (C, C), 1)
    masks = [ti == si]
    for m in levels[1:]:
        masks.append((ti // (2 * m) == si // (2 * m)) & ((ti // m) % 2 == 1) & ((si // m) % 2 == 0))
    odd_row = lax.broadcasted_iota(jnp.int32, (C, C_WIDTH), 0) % 2 == 1

    for ci in range(tl // C):
        rows = slice(ci * C, (ci + 1) * C)
        g = g_all[rows, :]
        pieces = []
        rest = g
        for _ in range(HGRN_SPLIT):
            piece = rest.astype(BF16)
            pieces.append(piece)
            rest = rest - piece.astype(F32)
        sums = _dot(dm_ref[...], jnp.concatenate(pieces, axis=0))
        e = jnp.exp(jnp.minimum(sums, 0.0))
        e_one = jnp.exp(jnp.where(odd_row, g, 0.0))
        for h in range(C_HEADS):
            hs = slice(h * C_DK, (h + 1) * C_DK)
            q = q_all[rows, hs]
            k = k_all[rows, hs]
            vb = v_all[rows, hs].astype(BF16)
            scores = jnp.where(masks[0], _dot_nt(q.astype(BF16), k.astype(BF16)), 0.0)
            for li in range(1, nlev):
                el = e[(li + 1) * C:(li + 2) * C, hs] if li < nlev - 1 else e_one[:, hs]
                scores = scores + jnp.where(masks[li], _dot_nt((q * el).astype(BF16), (k * el).astype(BF16)), 0.0)
            e_cum = e[0:C, hs]
            qc = (q * e_cum).astype(BF16)
            kc = (k * e[C:2 * C, hs]).astype(BF16)
            st = states[h]
            o = _dot(scores.astype(BF16), vb) + _dot_nt(qc, st.astype(BF16))
            states[h] = st * e_cum[C - 1:C, :] + _dot_tn(vb, kc)
            ms = jnp.mean(o * o, axis=-1, keepdims=True)
            y_ref[0, rows, hs] = (o * lax.rsqrt(ms + NORM_EPS) * ng_ref[...] * gate_all[rows, hs]).astype(y_ref.dtype)
    for h in range(C_HEADS):
        st_s[h] = states[h]
        s_ref[0, h] = states[h].T


def _hgrn(x, w, b, lb, norm_g, s0):
    B, L, _ = x.shape
    C = min(CHUNK, L)
    tl = min(L, 512)
    dm = jnp.asarray(_hgrn_sum_matrices(C), BF16)
    const = lambda bi, j: (0, 0)
    per_b = lambda bi, j: (bi, 0, 0, 0)
    return pl.pallas_call(
        functools.partial(_hgrn_kernel, tl=tl, C=C),
        grid=(B, L // tl),
        in_specs=[pl.BlockSpec((1, tl, D_MODEL), lambda bi, j: (bi, j, 0)),
                  pl.BlockSpec((D_MODEL, 4 * C_WIDTH), const),
                  pl.BlockSpec((1, 4 * C_WIDTH), const),
                  pl.BlockSpec((1, C_WIDTH), const),
                  pl.BlockSpec((1, C_DV), const),
                  pl.BlockSpec(dm.shape, const),
                  pl.BlockSpec((1, C_HEADS, C_DK, C_DV), per_b)],
        out_specs=[pl.BlockSpec((1, tl, C_WIDTH), lambda bi, j: (bi, j, 0)),
                   pl.BlockSpec((1, C_HEADS, C_DK, C_DV), per_b)],
        out_shape=[jax.ShapeDtypeStruct((B, L, C_WIDTH), BF16),
                   jax.ShapeDtypeStruct((B, C_HEADS, C_DK, C_DV), F32)],
        scratch_shapes=[pltpu.VMEM((C_HEADS, C_DV, C_DK), F32)],
        compiler_params=_cparams("parallel", "arbitrary"),
        name="hgrn",
    )(x, w, b, lb, norm_g, dm, s0)


def _merge_kernel(x_ref, ya_ref, yb_ref, yc_ref, wg_ref, bg_ref, wo_ref, g1_ref, b1_ref, rw_ref, rb_ref,
                  o_ref, cls_ref, *, alpha, group):
    for r0 in range(0, x_ref.shape[0], group):
        _merge_rows(slice(r0, r0 + group), x_ref, ya_ref, yb_ref, yc_ref, wg_ref, bg_ref, wo_ref, g1_ref, b1_ref,
                    rw_ref, rb_ref, o_ref, cls_ref, alpha)


def _merge_rows(rs, x_ref, ya_ref, yb_ref, yc_ref, wg_ref, bg_ref, wo_ref, g1_ref, b1_ref, rw_ref, rb_ref,
                o_ref, cls_ref, alpha):
    xb = x_ref[rs, :].astype(BF16)
    for c in range(D_MODEL // MERGE_COLS):
        cs = slice(c * MERGE_COLS, (c + 1) * MERGE_COLS)
        pre = alpha * x_ref[rs, cs]
        for n, y_ref in enumerate((ya_ref, yb_ref, yc_ref)):
            gs = slice(n * D_MODEL + c * MERGE_COLS, n * D_MODEL + (c + 1) * MERGE_COLS)
            gate = _sigmoid(_dot(xb, wg_ref[:, gs]) + bg_ref[:, gs])
            pre = pre + gate * _dot(y_ref[rs, :], wo_ref[n * A_WIDTH:(n + 1) * A_WIDTH, cs])
        o_ref[rs, cs] = pre
    x1 = _layer_norm(o_ref[rs, :D_MODEL], g1_ref[...], b1_ref[...])
    o_ref[rs, :D_MODEL] = x1

    lg = _dot(x1.astype(BF16), rw_ref[...]) + rb_ref[...]
    col = [lg[:, e:e + 1] for e in range(N_EXPERTS)]
    gmax = [functools.reduce(jnp.maximum, col[g * EXPERTS_PER_GROUP:(g + 1) * EXPERTS_PER_GROUP])
            for g in range(N_GROUPS)]
    best = functools.reduce(jnp.maximum, gmax)
    gs = jnp.where(gmax[0] == best, 0, jnp.where(gmax[1] == best, 1, jnp.where(gmax[2] == best, 2, 3)))
    v = []
    for e in range(EXPERTS_PER_GROUP):
        v.append(jnp.where(gs == 0, col[e],
                           jnp.where(gs == 1, col[EXPERTS_PER_GROUP + e],
                                     jnp.where(gs == 2, col[2 * EXPERTS_PER_GROUP + e],
                                               col[3 * EXPERTS_PER_GROUP + e]))))
    m1 = best
    i1 = jnp.where(v[0] == m1, 0, jnp.where(v[1] == m1, 1, jnp.where(v[2] == m1, 2, 3)))
    v2 = [jnp.where(i1 == e, -jnp.inf, v[e]) for e in range(EXPERTS_PER_GROUP)]
    m2 = functools.reduce(jnp.maximum, v2)
    i2 = jnp.where(v2[0] == m2, 0, jnp.where(v2[1] == m2, 1, jnp.where(v2[2] == m2, 2, 3)))
    e2 = jnp.exp(m2 - m1)
    p1 = 1.0 / (1.0 + e2)
    p2 = e2 / (1.0 + e2)
    ea = jnp.minimum(i1, i2)
    eb = jnp.maximum(i1, i2)
    ga = jnp.where(i1 < i2, p1, p2)
    gb = jnp.where(i1 < i2, p2, p1)
    pair = jnp.where(ea == 0, 0, jnp.where(ea == 1, 3, 5)) + (eb - ea - 1)
    cls = (gs * PAIRS_PER_GROUP + pair).astype(F32)
    lane = lax.broadcasted_iota(jnp.int32, (rs.stop - rs.start, AUX_W), 1)
    aux = jnp.where(lane == 0, ga, jnp.where(lane == 1, gb, jnp.where(lane == 2, cls, 0.0)))
    o_ref[rs, D_MODEL:] = aux
    pick = (lax.broadcasted_iota(jnp.int32, (8, AUX_W), 0) == 0) & (lax.broadcasted_iota(jnp.int32, (8, AUX_W), 1) == 2)
    cls_ref[:, rs] = _dot_nt(jnp.where(pick, 1.0, 0.0).astype(BF16), aux.astype(BF16))


def _merge(x2d, ya, yb, yc, wg, bg, wo, g1, b1, rw, rb, alpha):
    T = x2d.shape[0]
    tm = min(T, MERGE_ROWS)
    group = min(tm, MERGE_GROUP)
    row = lambda i: (i, 0)
    const = lambda i: (0, 0)
    return pl.pallas_call(
        functools.partial(_merge_kernel, alpha=alpha, group=group),
        grid=(T // tm,),
        in_specs=[pl.BlockSpec((tm, D_MODEL), row),
                  pl.BlockSpec((tm, A_WIDTH), row),
                  pl.BlockSpec((tm, B_WIDTH), row),
                  pl.BlockSpec((tm, C_WIDTH), row),
                  pl.BlockSpec((D_MODEL, N_BRANCH * D_MODEL), const),
                  pl.BlockSpec((1, N_BRANCH * D_MODEL), const),
                  pl.BlockSpec((N_BRANCH * A_WIDTH, D_MODEL), const),
                  pl.BlockSpec((1, D_MODEL), const),
                  pl.BlockSpec((1, D_MODEL), const),
                  pl.BlockSpec((D_MODEL, N_EXPERTS), const),
                  pl.BlockSpec((1, N_EXPERTS), const)],
        out_specs=[pl.BlockSpec((tm, D_MODEL + AUX_W), row), pl.BlockSpec((8, tm), lambda i: (0, i))],
        out_shape=[jax.ShapeDtypeStruct((T, D_MODEL + AUX_W), F32), jax.ShapeDtypeStruct((8, T), F32)],
        compiler_params=_cparams("parallel"),
        name="merge",
    )(x2d, ya, yb, yc, wg, bg, wo, g1, b1, rw, rb)


def _dispatch_kernel(dest_ref, x_ref, out_hbm, zbuf, sem, *, tm, n_tok_steps):
    i = pl.program_id(0)

    def scatter_rows(src):
        for r in range(tm):
            pltpu.make_async_copy(src.at[pl.ds(r, 1), :], out_hbm.at[pl.ds(dest_ref[0, 0, r], 1), :], sem).start()
        pltpu.make_async_copy(src, out_hbm.at[pl.ds(0, tm), :], sem).wait()

    @pl.when(i < n_tok_steps)
    def _():
        scatter_rows(x_ref)

    @pl.when(i >= n_tok_steps)
    def _():
        @pl.when(i == n_tok_steps)
        def _():
            zbuf[...] = jnp.zeros(zbuf.shape, F32)

        scatter_rows(zbuf)


def _dispatch(x1w, dest_all, n_rows):
    T, W = x1w.shape
    tm = min(T, MOE_DMA_ROWS)
    assert T % tm == 0 and dest_all.shape[0] % tm == 0
    n_tok_steps = T // tm
    steps = dest_all.shape[0] // tm
    return pl.pallas_call(
        functools.partial(_dispatch_kernel, tm=tm, n_tok_steps=n_tok_steps),
        grid=(steps,),
        in_specs=[pl.BlockSpec((1, 1, tm), lambda i: (i, 0, 0), memory_space=pltpu.SMEM),
                  pl.BlockSpec((tm, W), lambda i: (jnp.minimum(i, n_tok_steps - 1), 0))],
        out_specs=pl.BlockSpec(memory_space=pl.ANY),
        out_shape=jax.ShapeDtypeStruct((n_rows, W), F32),
        scratch_shapes=[pltpu.VMEM((tm, W), F32), pltpu.SemaphoreType.DMA(())],
        compiler_params=_cparams("arbitrary"),
        name="moe_dispatch",
    )(dest_all.reshape(steps, 1, tm), x1w)


def _experts_kernel(ea_ref, eb_ref, nv_ref, x_ref, w1a, w3a, w2a, w1b, w3b, w2b, g2_ref, b2_ref, o_ref, *, alpha):
    i = pl.program_id(0)

    @pl.when(nv_ref[i] > 0)
    def _():
        xw = x_ref[...]
        x = xw[:, :D_MODEL]
        xb = x.astype(BF16)
        y = None
        for n, (w1, w3, w2) in enumerate(((w1a, w3a, w2a), (w1b, w3b, w2b))):
            hmid = _silu(_dot(xb, w1[0])) * _dot(xb, w3[0])
            ye = xw[:, D_MODEL + n:D_MODEL + n + 1] * _dot(hmid.astype(BF16), w2[0])
            y = ye if y is None else y + ye
        o_ref[...] = _layer_norm(alpha * x + y, g2_ref[...], b2_ref[...])

    @pl.when(nv_ref[i] == 0)
    def _():
        o_ref[...] = jnp.zeros(o_ref.shape, F32)


def _experts(xs, ea, eb, nv, w1, w3, w2, g2, b2, alpha):
    n_rows, W = xs.shape
    wa = lambda i, ea, eb, nv: (ea[i], 0, 0)
    wb = lambda i, ea, eb, nv: (eb[i], 0, 0)
    const = lambda i, ea, eb, nv: (0, 0)
    row = lambda i, ea, eb, nv: (i, 0)
    up = (1, D_MODEL, D_EXPERT)
    down = (1, D_EXPERT, D_MODEL)
    grid_spec = pltpu.PrefetchScalarGridSpec(
        num_scalar_prefetch=3,
        grid=(n_rows // MOE_ROWS,),
        in_specs=[pl.BlockSpec((MOE_ROWS, W), row),
                  pl.BlockSpec(up, wa), pl.BlockSpec(up, wa), pl.BlockSpec(down, wa),
                  pl.BlockSpec(up, wb), pl.BlockSpec(up, wb), pl.BlockSpec(down, wb),
                  pl.BlockSpec((1, D_MODEL), const), pl.BlockSpec((1, D_MODEL), const)],
        out_specs=pl.BlockSpec((MOE_ROWS, D_MODEL), row),
    )
    return pl.pallas_call(
        functools.partial(_experts_kernel, alpha=alpha),
        grid_spec=grid_spec,
        out_shape=jax.ShapeDtypeStruct((n_rows, D_MODEL), F32),
        compiler_params=_cparams("arbitrary"),
        name="moe_experts",
    )(ea, eb, nv, xs, w1, w3, w2, w1, w3, w2, g2, b2)


def _combine_kernel(dest_ref, y_hbm, o_ref, sem, *, tm):
    for r in range(tm):
        pltpu.make_async_copy(y_hbm.at[pl.ds(dest_ref[0, 0, r], 1), :], o_ref.at[pl.ds(r, 1), :], sem).start()
    pltpu.make_async_copy(y_hbm.at[pl.ds(0, tm), :], o_ref, sem).wait()


def _combine(ys, dest, T):
    tm = min(T, MOE_DMA_ROWS)
    return pl.pallas_call(
        functools.partial(_combine_kernel, tm=tm),
        grid=(T // tm,),
        in_specs=[pl.BlockSpec((1, 1, tm), lambda i: (i, 0, 0), memory_space=pltpu.SMEM),
                  pl.BlockSpec(memory_space=pl.ANY)],
        out_specs=pl.BlockSpec((tm, D_MODEL), lambda i: (i, 0)),
        out_shape=jax.ShapeDtypeStruct((T, D_MODEL), F32),
        scratch_shapes=[pltpu.SemaphoreType.DMA(())],
        compiler_params=_cparams("arbitrary"),
        name="moe_combine",
    )(dest.reshape(T // tm, 1, tm), ys)


def _route(cls, T):
    n_blocks = T // MOE_ROWS + N_CLASSES
    n_rows = n_blocks * MOE_ROWS
    rb = math.gcd(T, MOE_ROWS)
    onehot = (cls[:, None] == jnp.arange(N_CLASSES, dtype=jnp.int32)[None, :]).astype(F32).reshape(T // rb, rb, N_CLASSES)
    strict = jnp.asarray(np.tril(np.ones((rb, rb), np.float32), -1))
    rank_in = jnp.einsum('ij,bjc->bic', strict, onehot)
    blk_counts = jnp.sum(onehot, axis=1)
    blk_off = jnp.cumsum(blk_counts, axis=0) - blk_counts
    rank = jnp.sum(onehot * (rank_in + blk_off[:, None, :]), axis=-1).reshape(T).astype(jnp.int32)
    counts = jnp.sum(blk_counts, axis=0).astype(jnp.int32)
    padded = (counts + MOE_ROWS - 1) // MOE_ROWS * MOE_ROWS
    pad_end = jnp.cumsum(padded)
    pad_start = pad_end - padded
    dest = (pad_start[cls] + rank).astype(jnp.int32)
    seg_start = jnp.concatenate([pad_start + counts, pad_end[-1:]])
    seg_len = jnp.concatenate([padded - counts, n_rows - pad_end[-1:]])
    seg_end = jnp.cumsum(seg_len)
    kk = jnp.arange(n_rows - T, dtype=jnp.int32)
    seg = jnp.sum((seg_end[None, :] <= kk[:, None]).astype(jnp.int32), axis=1)
    pad_rows = (seg_start[seg] + kk - (seg_end[seg] - seg_len[seg])).astype(jnp.int32)
    blk_start = jnp.arange(n_blocks, dtype=jnp.int32) * MOE_ROWS
    blk_cls = jnp.minimum(jnp.sum((pad_end[None, :] <= blk_start[:, None]).astype(jnp.int32), axis=1), N_CLASSES - 1)
    nv = jnp.clip(pad_start[blk_cls] + counts[blk_cls] - blk_start, 0, MOE_ROWS).astype(jnp.int32)
    grp = blk_cls // PAIRS_PER_GROUP
    pair = blk_cls % PAIRS_PER_GROUP
    pa = jnp.asarray([0, 0, 0, 1, 1, 2], jnp.int32)[pair]
    pb = jnp.asarray([1, 2, 3, 2, 3, 3], jnp.int32)[pair]
    ea = grp * EXPERTS_PER_GROUP + pa
    eb = grp * EXPERTS_PER_GROUP + pb
    return ea, eb, nv, dest, pad_rows, n_rows


def _moe(x1w, cls_rows, w1, w3, w2, g2, b2, alpha):
    T = x1w.shape[0]
    cls = cls_rows[0].astype(jnp.int32)
    ea, eb, nv, dest, pad_rows, n_rows = _route(cls, T)
    xs = _dispatch(x1w, jnp.concatenate([dest, pad_rows]), n_rows)
    ys = _experts(xs, ea, eb, nv, w1, w3, w2, g2, b2, alpha)
    return _combine(ys, dest, T)


def _block_diag(w):
    n, d, _ = w.shape
    eye = jnp.eye(n, dtype=w.dtype)
    return jnp.einsum('nde,nm->ndme', w, eye).reshape(n * d, n * d)


def _layer(x, pos0, li, P, past, kv_carried):
    B, L, _ = x.shape
    T = B * L
    depth = P['depth']
    alpha = (2 * depth) ** 0.25
    w_in, b_in = P['w_in'][li], P['b_in'][li]

    def cols(lo, hi):
        return w_in[:, lo:hi], b_in[lo:hi].reshape(1, -1)

    lam_init = 0.8 - 0.6 * math.exp(-0.3 * li)
    dl = P['diff_lambda'][li]
    lam = jnp.exp(jnp.sum(dl[0] * dl[1])) - jnp.exp(jnp.sum(dl[2] * dl[3])) + lam_init
    lam_row = jnp.full((1, LANES), lam, F32)
    subln_g = P['diff_subln_g'][li].reshape(1, -1)

    wa, ba = cols(OFF_A, OFF_B)
    q, k_all, kb, v_all, vb, vt = _attn_proj(x, wa, ba, pos0, li, depth, kv_carried)
    if past is None:
        y_a = _flash_prompt(q, kb, vt, lam_row, subln_g.reshape(-1, 1), 1.0 - lam_init)
        conv_buf = jnp.zeros((B, CONV_W - 1, B_WIDTH), F32)
        h0 = jnp.zeros((B, 1, B_WIDTH), F32)
        s0 = jnp.zeros((B, C_HEADS, C_DK, C_DV), F32)
    else:
        kt_cache, v_cache, conv_buf, h0, s0 = past
        y_a = _decode_attention(q, kb, vb, kt_cache, v_cache, li, lam_row, subln_g, 1.0 - lam_init)
        h0 = h0.reshape(B, 1, B_WIDTH)

    wb, bb = cols(OFF_B, OFF_C)
    y_b, conv_new, h_new = _rglru(x, wb, bb, P['conv_w'][li], P['conv_b'][li].reshape(1, -1),
                                  P['lru_gate_w'][li], P['lru_gate_b'][li], P['lru_sp'][li].reshape(1, -1),
                                  conv_buf, h0, pos0)
    wc, bc = cols(OFF_C, OFF_G)
    y_c, s_new = _hgrn(x, wc, bc, P['lower_bounds'][li].reshape(1, -1), P['hgrn_norm_g'][li].reshape(1, -1), s0)

    wg, bg = cols(OFF_G, N_IN)
    x1w, cls_rows = _merge(x.reshape(T, D_MODEL), y_a.reshape(T, A_WIDTH), y_b.reshape(T, B_WIDTH),
                 y_c.reshape(T, C_WIDTH), wg, bg, P['w_out'][li], P['ln1_g'][li].reshape(1, -1),
                 P['ln1_b'][li].reshape(1, -1), P['router_w'], P['router_b'].reshape(1, -1), alpha)
    x2 = _moe(x1w, cls_rows, P['moe_w1'][li], P['moe_w3'][li], P['moe_w2'][li], P['ln2_g'][li].reshape(1, -1),
              P['ln2_b'][li].reshape(1, -1), alpha)
    states = (conv_new, h_new.reshape(B, B_WIDTH), s_new)
    return x2.reshape(B, L, D_MODEL), (k_all, v_all), states


def kernel(x_prompt, x_sample, cache_k, cache_v, state_conv, state_lru, state_hgrn, ln_in_g, ln_in_b, w_in, b_in, diff_lambda, diff_subln_g, conv_w, conv_b, lru_wa, lru_ba, lru_wx, lru_bx, lru_lambda, hgrn_lb, hgrn_norm_g, w_out, ln1_g, ln1_b, router_w, router_b, moe_w1, moe_w3, moe_w2, ln2_g, ln2_b):
    depth = w_in.shape[0]
    p_lb = jax.nn.softmax(hgrn_lb.astype(F32), axis=0)
    P = dict(
        depth=depth,
        w_in=w_in.astype(BF16), b_in=b_in,
        diff_lambda=diff_lambda.astype(F32), diff_subln_g=diff_subln_g,
        conv_w=conv_w, conv_b=conv_b,
        lru_gate_w=jnp.stack([jnp.concatenate([_block_diag(lru_wa[l]), _block_diag(lru_wx[l])], axis=1)
                              for l in range(depth)]).astype(BF16),
        lru_gate_b=jnp.concatenate([lru_ba, lru_bx], axis=1).reshape(depth, 1, -1),
        lru_sp=jax.nn.softplus(-lru_lambda.astype(F32)),
        lower_bounds=jnp.cumsum(p_lb, axis=0) - p_lb[0],
        hgrn_norm_g=hgrn_norm_g,
        w_out=w_out.astype(BF16), ln1_g=ln1_g, ln1_b=ln1_b,
        router_w=router_w.astype(BF16), router_b=router_b,
        moe_w1=moe_w1.astype(BF16), moe_w3=moe_w3.astype(BF16), moe_w2=moe_w2.astype(BF16),
        ln2_g=ln2_g, ln2_b=ln2_b,
    )
    Bp, Lp, _ = x_prompt.shape
    Bs, Ls, _ = x_sample.shape
    past_len = cache_k.shape[2]
    kt_cache = jnp.transpose(cache_k, (0, 1, 3, 4, 5, 2)).reshape(depth, Bs, A_QK_WIDTH, past_len)
    v_cache = cache_v.reshape(depth, Bs, past_len * A_HEADS, A_V_DIM)
    xp = _input_ln(x_prompt.reshape(Bp * Lp, D_MODEL), ln_in_g, ln_in_b).reshape(Bp, Lp, D_MODEL)
    xs = _input_ln(x_sample.reshape(Bs * Ls, D_MODEL), ln_in_g, ln_in_b).reshape(Bs, Ls, D_MODEL)
    prompt_states, sample_states = [], []
    kv_p = kv_s = None
    for li in range(depth):
        xp, kv_p, st_p = _layer(xp, 0, li, P, None, kv_p)
        xs, kv_s, st_s = _layer(xs, past_len, li, P,
                                (kt_cache, v_cache, state_conv[li], state_lru[li], state_hgrn[li]), kv_s)
        prompt_states.append(st_p)
        sample_states.append(st_s)
    conv_p, lru_p, hgrn_p = [jnp.stack(t) for t in zip(*prompt_states)]
    conv_s, lru_s, hgrn_s = [jnp.stack(t) for t in zip(*sample_states)]
    k_p = kv_p[0].reshape(depth, Bp, Lp, A_HEADS, 2, A_HEAD_DIM)
    v_p = kv_p[1].reshape(depth, Bp, Lp, A_HEADS, A_V_DIM)
    k_s = kv_s[0].reshape(depth, Bs, Ls, A_HEADS, 2, A_HEAD_DIM)
    v_s = kv_s[1].reshape(depth, Bs, Ls, A_HEADS, A_V_DIM)
    return (xp, xs, k_p, v_p, conv_p, lru_p, hgrn_p, k_s, v_s, conv_s, lru_s, hgrn_s)
```

```python
import functools
import math

import numpy as np
import jax
import jax.numpy as jnp
from jax import lax
from jax.experimental import pallas as pl
from jax.experimental.pallas import tpu as pltpu

F32 = jnp.float32
BF16 = jnp.bfloat16

D_MODEL = 1024
CHUNK = 64
A_HEADS = 4
A_HEAD_DIM = 64
A_V_DIM = 2 * A_HEAD_DIM
A_QK_WIDTH = A_HEADS * 2 * A_HEAD_DIM
A_WIDTH = A_HEADS * A_V_DIM
ROPE_THETA = 10000.0
NEG_INF = -1e30
Q_SCALE = A_HEAD_DIM ** -0.5 * math.log2(math.e)
B_WIDTH = 512
B_BLOCKS = 8
CONV_W = 4
LRU_C = 8.0
C_HEADS = 4
C_DK = 128
C_DV = 128
C_WIDTH = C_HEADS * C_DV
F_TINY = 1e-30
N_BRANCH = 3
N_EXPERTS = 16
N_GROUPS = 4
EXPERTS_PER_GROUP = N_EXPERTS // N_GROUPS
D_EXPERT = 512
NORM_EPS = 1e-5

LANES = 128
SUBLANES = 8
PAIRS_PER_GROUP = 6
N_CLASSES = N_GROUPS * PAIRS_PER_GROUP
MOE_ROWS = 256
MERGE_COLS = 256
MERGE_ROWS = 512
MOE_DMA_ROWS = 1024
AUX_W = LANES
VMEM_LIMIT = 56 * 1024 * 1024

OFF_A = 0
OFF_B = OFF_A + 2 * A_QK_WIDTH + A_WIDTH
OFF_C = OFF_B + 2 * B_WIDTH
OFF_G = OFF_C + 4 * C_WIDTH
N_IN = OFF_G + N_BRANCH * D_MODEL


def _cparams(*sem):
    return pltpu.CompilerParams(dimension_semantics=sem, vmem_limit_bytes=VMEM_LIMIT)


def _dot(a, b):
    return jnp.dot(a, b, preferred_element_type=F32)


def _dot_nt(a, b):
    return lax.dot_general(a, b, (((1,), (1,)), ((), ())), preferred_element_type=F32)


def _dot_tn(a, b):
    return lax.dot_general(a, b, (((0,), (0,)), ((), ())), preferred_element_type=F32)


def _sigmoid(x):
    return 1.0 / (1.0 + jnp.exp(-x))


def _silu(x):
    return x * _sigmoid(x)


def _layer_norm(x, g, b):
    mu = jnp.mean(x, axis=-1, keepdims=True)
    xc = x - mu
    var = jnp.mean(xc * xc, axis=-1, keepdims=True)
    return xc * lax.rsqrt(var + NORM_EPS) * g + b


def _ln_kernel(x_ref, g_ref, b_ref, o_ref):
    o_ref[...] = _layer_norm(x_ref[...], g_ref[...], b_ref[...])


def _input_ln(x2d, g, b):
    T = x2d.shape[0]
    tm = min(T, 512)
    return pl.pallas_call(
        _ln_kernel,
        grid=(T // tm,),
        in_specs=[pl.BlockSpec((tm, D_MODEL), lambda i: (i, 0)),
                  pl.BlockSpec((1, D_MODEL), lambda i: (0, 0)),
                  pl.BlockSpec((1, D_MODEL), lambda i: (0, 0))],
        out_specs=pl.BlockSpec((tm, D_MODEL), lambda i: (i, 0)),
        out_shape=jax.ShapeDtypeStruct((T, D_MODEL), F32),
        compiler_params=_cparams("parallel"),
        name="input_ln",
    )(x2d, g.reshape(1, -1), b.reshape(1, -1))


def _attn_proj_kernel(x_ref, w_ref, b_ref, cos_ref, sin_ref, *refs):
    q_ref, k_ref, kb_ref, v_ref, vb_ref, vt_ref = refs[-6:]
    x = x_ref[0].astype(BF16)
    y = _dot(x, w_ref[...]) + b_ref[...]
    cos = cos_ref[...]
    sin = sin_ref[...]
    lane = lax.broadcasted_iota(jnp.int32, cos.shape, 1)
    first_half = (lane % A_HEAD_DIM) < (A_HEAD_DIM // 2)
    for s in range(2 * A_QK_WIDTH // LANES):
        blk = y[:, s * LANES:(s + 1) * LANES]
        swapped = jnp.where(first_half,
                            pltpu.roll(blk, LANES - A_HEAD_DIM // 2, 1),
                            pltpu.roll(blk, A_HEAD_DIM // 2, 1))
        rot = blk * cos + swapped * sin
        if s < A_QK_WIDTH // LANES:
            q_ref[0, :, s * LANES:(s + 1) * LANES] = (rot * Q_SCALE).astype(BF16)
        else:
            o = s * LANES - A_QK_WIDTH
            k_ref[0, 0, :, o:o + LANES] = rot
            kb_ref[0, :, o:o + LANES] = rot.astype(BF16)
    v = y[:, 2 * A_QK_WIDTH:]
    for h in range(A_HEADS):
        v_ref[0, 0, pl.ds(h, v.shape[0], stride=A_HEADS), :] = v[:, h * A_V_DIM:(h + 1) * A_V_DIM]
    vb_ref[0] = v.astype(BF16)
    vt_ref[0] = v.T.astype(BF16)
    for later in range(1, k_ref.shape[0]):
        k_ref[later] = jnp.zeros(k_ref.shape[1:], F32)
        v_ref[later] = jnp.zeros(v_ref.shape[1:], F32)


def _rope_tables(pos0, L):
    half = A_HEAD_DIM // 2
    inv_freq = ROPE_THETA ** (-jnp.arange(half, dtype=F32) / half)
    pos = (pos0 + jnp.arange(L)).astype(F32)
    ang = pos[:, None] * inv_freq[None, :]
    cos = jnp.cos(ang)
    sin = jnp.sin(ang)
    reps = LANES // A_HEAD_DIM
    cos_t = jnp.tile(jnp.concatenate([cos, cos], axis=1), (1, reps))
    sin_t = jnp.tile(jnp.concatenate([-sin, sin], axis=1), (1, reps))
    return cos_t, sin_t


def _attn_proj(x, w, b, pos0, li, depth, carried):
    B, L, _ = x.shape
    tl = min(L, 512)
    cos_t, sin_t = _rope_tables(pos0, L)
    wa = A_QK_WIDTH
    n = 2 * A_QK_WIDTH + A_WIDTH
    row = lambda bi, j: (bi, j, 0)
    layer_row = lambda bi, j: (li, bi, j, 0)
    carried = () if carried is None else tuple(carried)
    n_layers = 1 if carried else depth
    assert carried or li == 0
    n_in = 5
    return pl.pallas_call(
        _attn_proj_kernel,
        grid=(B, L // tl),
        in_specs=[pl.BlockSpec((1, tl, D_MODEL), row),
                  pl.BlockSpec((D_MODEL, n), lambda bi, j: (0, 0)),
                  pl.BlockSpec((1, n), lambda bi, j: (0, 0)),
                  pl.BlockSpec((tl, LANES), lambda bi, j: (j, 0)),
                  pl.BlockSpec((tl, LANES), lambda bi, j: (j, 0))]
        + [pl.BlockSpec(memory_space=pl.ANY)] * len(carried),
        out_specs=[pl.BlockSpec((1, tl, wa), row),
                   pl.BlockSpec((n_layers, 1, tl, wa), layer_row),
                   pl.BlockSpec((1, tl, wa), row),
                   pl.BlockSpec((n_layers, 1, tl * A_HEADS, A_V_DIM), layer_row),
                   pl.BlockSpec((1, tl, wa), row),
                   pl.BlockSpec((1, wa, tl), lambda bi, j: (bi, 0, j))],
        out_shape=[jax.ShapeDtypeStruct((B, L, wa), BF16),
                   jax.ShapeDtypeStruct((depth, B, L, wa), F32),
                   jax.ShapeDtypeStruct((B, L, wa), BF16),
                   jax.ShapeDtypeStruct((depth, B, L * A_HEADS, A_V_DIM), F32),
                   jax.ShapeDtypeStruct((B, L, wa), BF16),
                   jax.ShapeDtypeStruct((B, wa, L), BF16)],
        input_output_aliases={n_in: 1, n_in + 1: 3} if carried else {},
        compiler_params=_cparams("parallel", "parallel"),
        name="attn_proj",
    )(x, w, b, cos_t, sin_t, *carried)


def _split_maps(q):
    lane = lax.broadcasted_iota(jnp.int32, q.shape, 1)
    zero = jnp.zeros_like(q)
    return jnp.where(lane < A_HEAD_DIM, q, zero), jnp.where(lane >= A_HEAD_DIM, q, zero)


def _flash_kernel(lam_ref, g_ref, q_ref, k_ref, vt_ref, o_ref, m_s, l_s, acc_s, sa_s, sb_s, *, tq, out_scale):
    i = pl.program_id(2)
    qz = _split_maps(q_ref[0])
    qq = jnp.concatenate(qz, axis=0)
    m_s[...] = jnp.full(m_s.shape, NEG_INF, F32)
    l_s[...] = jnp.zeros(l_s.shape, F32)
    acc_s[...] = jnp.zeros(acc_s.shape, F32)

    def scores_into(dst, blk):
        off = pl.multiple_of(blk * tq, tq)
        dst[...] = _dot_nt(k_ref[0, pl.ds(off, tq), :], qq)

    def update(src, blk, diagonal):
        off = pl.multiple_of(blk * tq, tq)
        vt = vt_ref[0, :, pl.ds(off, tq)]
        s = src[...]
        if diagonal:
            key = lax.broadcasted_iota(jnp.int32, (tq, 2 * tq), 0) // CHUNK
            qry = (lax.broadcasted_iota(jnp.int32, (tq, 2 * tq), 1) % tq) // CHUNK
            s = jnp.where(key <= qry, s, NEG_INF)
        m_old = m_s[...]
        m_new = jnp.maximum(m_old, jnp.max(s, axis=0, keepdims=True))
        alpha = jnp.exp2(m_old - m_new)
        p = jnp.exp2(s - m_new)
        l_s[...] = alpha * l_s[...] + jnp.sum(p, axis=0, keepdims=True)
        acc_s[...] = alpha * acc_s[...] + _dot(vt, p.astype(BF16))
        m_s[...] = m_new

    scores_into(sa_s, 0)

    def pair(p):
        scores_into(sb_s, 2 * p + 1)
        update(sa_s, 2 * p, False)
        scores_into(sa_s, 2 * p + 2)
        update(sb_s, 2 * p + 1, False)

    def two_pairs(t, carry):
        pair(2 * t)
        pair(2 * t + 1)
        return carry

    n_pairs = i // 2
    lax.fori_loop(0, n_pairs // 2, two_pairs, 0)

    @pl.when(n_pairs % 2 == 1)
    def _():
        pair(n_pairs - 1)

    @pl.when(i % 2 == 1)
    def _():
        scores_into(sb_s, i)
        update(sa_s, i - 1, False)
        update(sb_s, i, True)

    @pl.when(i % 2 == 0)
    def _():
        update(sa_s, i, True)

    inv_l = 1.0 / l_s[...]
    o = acc_s[...] * inv_l
    o = o[:, :tq] - lam_ref[0:1, 0:1] * o[:, tq:]
    ms = jnp.mean(o * o, axis=0, keepdims=True)
    y = o * lax.rsqrt(ms + NORM_EPS) * (g_ref[...] * out_scale)
    o_ref[0] = y.T.astype(o_ref.dtype)


def _flash_prompt(q, kb, vt, lam_row, g_col, out_scale):
    B, L, _ = q.shape
    tq = min(L, 512)
    return pl.pallas_call(
        functools.partial(_flash_kernel, tq=tq, out_scale=out_scale),
        grid=(B, A_HEADS, L // tq),
        in_specs=[pl.BlockSpec((1, LANES), lambda b, h, i: (0, 0)),
                  pl.BlockSpec((A_V_DIM, 1), lambda b, h, i: (0, 0)),
                  pl.BlockSpec((1, tq, LANES), lambda b, h, i: (b, i, h)),
                  pl.BlockSpec((1, L, LANES), lambda b, h, i: (b, 0, h)),
                  pl.BlockSpec((1, A_V_DIM, L), lambda b, h, i: (b, h, 0))],
        out_specs=pl.BlockSpec((1, tq, A_V_DIM), lambda b, h, i: (b, i, h)),
        out_shape=jax.ShapeDtypeStruct((B, L, A_WIDTH), BF16),
        scratch_shapes=[pltpu.VMEM((1, 2 * tq), F32), pltpu.VMEM((1, 2 * tq), F32),
                        pltpu.VMEM((A_V_DIM, 2 * tq), F32),
                        pltpu.VMEM((tq, 2 * tq), F32), pltpu.VMEM((tq, 2 * tq), F32)],
        compiler_params=_cparams("parallel", "parallel", "arbitrary"),
        name="flash_prompt",
    )(lam_row, g_col, q, kb, vt)


def _decode_kernel(lam_ref, g_ref, q_ref, kn_ref, vn_ref, kp_ref, vp_ref, o_ref, s_s, acc_s, m_s, r_s,
                   *, nb, tk, L, out_scale):
    j = pl.program_id(1)
    P = nb * tk
    n_new = kn_ref.shape[1]

    def q_rows():
        rep = jnp.concatenate([q_ref[0]] * (LANES // L), axis=0)
        grp = lax.broadcasted_iota(jnp.int32, rep.shape, 0) // L
        cols = lax.broadcasted_iota(jnp.int32, rep.shape, 1) // A_HEAD_DIM
        return jnp.where(grp == cols, rep, jnp.zeros_like(rep))

    @pl.when(j < nb)
    def _():
        off = pl.multiple_of(j * tk, tk)
        s_s[:, pl.ds(off, tk)] = _dot(q_rows(), kp_ref[0, 0].astype(BF16))

    @pl.when(j == nb - 1)
    def _():
        s_new = _dot_nt(q_rows(), kn_ref[0])
        valid = lax.broadcasted_iota(jnp.int32, s_new.shape, 1) < L
        s_s[:, P:P + n_new] = jnp.where(valid, s_new, NEG_INF)
        s = s_s[...]
        m = jnp.max(s, axis=1, keepdims=True)
        m_s[...] = m
        r_s[...] = 1.0 / jnp.sum(jnp.exp2(s - m), axis=1, keepdims=True)
        acc_s[...] = jnp.zeros(acc_s.shape, F32)

    def weights(cols):
        return (jnp.exp2(s_s[:, cols] - m_s[...]) * r_s[...]).astype(BF16)

    @pl.when(j >= nb)
    def _():
        off = pl.multiple_of((j - nb) * tk, tk)
        w = weights(pl.ds(off, tk))
        for h in range(A_HEADS):
            rows = slice(2 * h * L, (2 * h + 2) * L)
            vh = vp_ref[0, 0, pl.ds(h, tk, stride=A_HEADS), :]
            acc_s[rows, :] += _dot(w[rows, :], vh.astype(BF16))

    @pl.when(j == 2 * nb - 1)
    def _():
        w = weights(slice(P, P + n_new))
        for h in range(A_HEADS):
            hs = slice(h * A_V_DIM, (h + 1) * A_V_DIM)
            rows = slice(2 * h * L, (2 * h + 2) * L)
            acc = acc_s[rows, :] + _dot(w[rows, :], vn_ref[0, :, hs])
            o = acc[:L] - lam_ref[...] * acc[L:]
            ms = jnp.mean(o * o, axis=-1, keepdims=True)
            o_ref[0, :, hs] = (o * lax.rsqrt(ms + NORM_EPS) * g_ref[...] * out_scale).astype(o_ref.dtype)


def _decode_attention(q, k_new, v_new, kt_cache, v_cache, li, lam_row, g_row, out_scale):
    B, L, _ = q.shape
    assert 2 * A_HEADS * L == LANES, "decode attention packs (head, map, frame) onto the 128 lanes"
    P = kt_cache.shape[3]
    tk = min(P, 2048)
    nb = P // tk
    pad = ((0, 0), (0, LANES - L), (0, 0))
    k_new = jnp.pad(k_new, pad)
    v_new = jnp.pad(v_new, pad)
    new = lambda b, j: (b, 0, 0)
    return pl.pallas_call(
        functools.partial(_decode_kernel, nb=nb, tk=tk, L=L, out_scale=out_scale),
        grid=(B, 2 * nb),
        in_specs=[pl.BlockSpec((1, LANES), lambda b, j: (0, 0)),
                  pl.BlockSpec((1, A_V_DIM), lambda b, j: (0, 0)),
                  pl.BlockSpec((1, L, A_QK_WIDTH), new),
                  pl.BlockSpec((1, LANES, A_QK_WIDTH), new),
                  pl.BlockSpec((1, LANES, A_WIDTH), new),
                  pl.BlockSpec((1, 1, A_QK_WIDTH, tk), lambda b, j: (li, b, 0, jnp.minimum(j, nb - 1))),
                  pl.BlockSpec((1, 1, tk * A_HEADS, A_V_DIM),
                               lambda b, j: (li, b, jnp.maximum(j - nb, 0), 0))],
        out_specs=pl.BlockSpec((1, L, A_WIDTH), new),
        out_shape=jax.ShapeDtypeStruct((B, L, A_WIDTH), BF16),
        scratch_shapes=[pltpu.VMEM((LANES, P + LANES), F32), pltpu.VMEM((LANES, A_V_DIM), F32),
                        pltpu.VMEM((LANES, 1), F32), pltpu.VMEM((LANES, 1), F32)],
        compiler_params=_cparams("parallel", "arbitrary"),
        name="decode_attention",
    )(lam_row, g_row, q, k_new, v_new, kt_cache, v_cache)


def _gelu_tanh(x):
    return 0.5 * x * (1.0 + jnp.tanh(math.sqrt(2.0 / math.pi) * (x + 0.044715 * (x * x * x))))


def _rglru_kernel(x_ref, w_ref, b_ref, cw_ref, cb_ref, gw_ref, gbias_ref, sp_ref, cbuf_ref, h0_ref,
                  y_ref, conv_ref, h_ref, xs, hc, hs, *, tl, pos0):
    j = pl.program_id(1)
    tail = CONV_W - 1
    base = SUBLANES

    @pl.when(j == 0)
    def _():
        xs[0:base, :] = jnp.zeros((base, B_WIDTH), F32)
        xs[base - tail:base, :] = cbuf_ref[0]
        hc[...] = h0_ref[0]

    x = x_ref[0].astype(BF16)
    y = _dot(x, w_ref[...]) + b_ref[...]
    xb = y[:, :B_WIDTH]
    gb = y[:, B_WIDTH:]
    xs[base:base + tl, :] = xb
    xc = cb_ref[...] + cw_ref[0:1, :] * xs[base - tail:base - tail + tl, :]
    for t in range(1, CONV_W):
        xc = xc + cw_ref[t:t + 1, :] * xs[base - tail + t:base - tail + t + tl, :]
    new_tail = xs[base + tl - tail:base + tl, :]
    xs[base - tail:base, :] = new_tail
    conv_ref[0] = new_tail

    gates = _dot(xc.astype(BF16), gw_ref[...]) + gbias_ref[...]
    r = _sigmoid(gates[:, :B_WIDTH])
    ig = _sigmoid(gates[:, B_WIDTH:])
    log_a = -LRU_C * r * sp_ref[...]
    a = jnp.exp(log_a)
    t = jnp.tanh(log_a)
    mult = jnp.sqrt(-2.0 * t / (1.0 - t))
    row = lax.broadcasted_iota(jnp.int32, (tl, B_WIDTH), 0)
    mult = jnp.where(row + (pos0 + j * tl) == 0, 1.0, mult)
    bb = xc * ig * mult
    a3 = a.reshape(tl // SUBLANES, SUBLANES, B_WIDTH)
    b3 = bb.reshape(tl // SUBLANES, SUBLANES, B_WIDTH)
    sub = lax.broadcasted_iota(jnp.int32, a3.shape, 1)
    d = 1
    while d < SUBLANES:
        keep = sub >= d
        a_sh = pltpu.roll(a3, d, 1)
        b_sh = pltpu.roll(b3, d, 1)
        b3 = jnp.where(keep, a3 * b_sh + b3, b3)
        a3 = jnp.where(keep, a3 * a_sh, a3)
        d *= 2
    carry = hc[...]
    for g in range(tl // SUBLANES):
        hg = a3[g] * carry + b3[g]
        hs[g * SUBLANES:(g + 1) * SUBLANES, :] = hg
        carry = hg[SUBLANES - 1:SUBLANES, :]
    hc[...] = carry
    h_ref[0] = carry
    y_ref[0] = (_gelu_tanh(gb) * hs[...]).astype(y_ref.dtype)


def _rglru(x, w, b, conv_w, conv_b, gate_w, gate_b, sp, conv_buf, h0, pos0):
    B, L, _ = x.shape
    tl = min(L, 512)
    tail = CONV_W - 1
    const = lambda bi, j: (0, 0)
    per_b = lambda bi, j: (bi, 0, 0)
    return pl.pallas_call(
        functools.partial(_rglru_kernel, tl=tl, pos0=pos0),
        grid=(B, L // tl),
        in_specs=[pl.BlockSpec((1, tl, D_MODEL), lambda bi, j: (bi, j, 0)),
                  pl.BlockSpec((D_MODEL, 2 * B_WIDTH), const),
                  pl.BlockSpec((1, 2 * B_WIDTH), const),
                  pl.BlockSpec((CONV_W, B_WIDTH), const),
                  pl.BlockSpec((1, B_WIDTH), const),
                  pl.BlockSpec((B_WIDTH, 2 * B_WIDTH), const),
                  pl.BlockSpec((1, 2 * B_WIDTH), const),
                  pl.BlockSpec((1, B_WIDTH), const),
                  pl.BlockSpec((1, tail, B_WIDTH), per_b),
                  pl.BlockSpec((1, 1, B_WIDTH), per_b)],
        out_specs=[pl.BlockSpec((1, tl, B_WIDTH), lambda bi, j: (bi, j, 0)),
                   pl.BlockSpec((1, tail, B_WIDTH), per_b),
                   pl.BlockSpec((1, 1, B_WIDTH), per_b)],
        out_shape=[jax.ShapeDtypeStruct((B, L, B_WIDTH), BF16),
                   jax.ShapeDtypeStruct((B, tail, B_WIDTH), F32),
                   jax.ShapeDtypeStruct((B, 1, B_WIDTH), F32)],
        scratch_shapes=[pltpu.VMEM((SUBLANES + tl, B_WIDTH), F32), pltpu.VMEM((1, B_WIDTH), F32),
                        pltpu.VMEM((tl, B_WIDTH), F32)],
        compiler_params=_cparams("parallel", "arbitrary"),
        name="rglru",
    )(x, w, b, conv_w, conv_b, gate_w, gate_b, sp, conv_buf, h0)


def _hgrn_levels(C):
    return [C >> i for i in range(int(math.log2(C)) + 1)]


HGRN_SPLIT = 3


def _hgrn_sum_matrices(C):
    t = np.arange(C)

    def prefix(m):
        return (t[None, :] >= ((t // m) * m)[:, None]) & (t[None, :] <= t[:, None])

    def suffix(m):
        return (t[None, :] > t[:, None]) & (t[None, :] <= ((t // m) * m + m - 1)[:, None])

    mats = [prefix(C), suffix(C)]
    for m in _hgrn_levels(C)[1:-1]:
        upper = ((t // m) % 2 == 1)[:, None]
        mats.append(np.where(upper, prefix(m), suffix(m)))
    one = np.concatenate(mats, axis=0).astype(np.float32)
    return np.concatenate([one] * HGRN_SPLIT, axis=1)


def _hgrn_kernel(x_ref, w_ref, b_ref, lb_ref, ng_ref, dm_ref, s0_ref, y_ref, s_ref, st_s, *, tl, C):
    j = pl.program_id(1)
    levels = _hgrn_levels(C)
    nlev = len(levels)

    @pl.when(j == 0)
    def _():
        for h in range(C_HEADS):
            st_s[h] = s0_ref[0, h].T

    x = x_ref[0].astype(BF16)
    y = _dot(x, w_ref[...]) + b_ref[...]
    lb = lb_ref[...]
    z = y[:, C_WIDTH:2 * C_WIDTH]
    f = lb + (1.0 - lb) * _sigmoid(z)
    q_all = _silu(y[:, :C_WIDTH])
    k_all = (1.0 - lb) * _sigmoid(-z)
    g_all = jnp.log(jnp.maximum(f, F_TINY))
    v_all = y[:, 2 * C_WIDTH:3 * C_WIDTH]
    gate_all = _silu(y[:, 3 * C_WIDTH:])
    states = [st_s[h] for h in range(C_HEADS)]

    ti = lax.broadcasted_iota(jnp.int32, (C, C), 0)
    si = lax.broadcasted_iota(jnp.int32, (C, C), 1)
    masks = [ti == si]
    for m in levels[1:]:
        masks.append((ti // (2 * m) == si // (2 * m)) & ((ti // m) % 2 == 1) & ((si // m) % 2 == 0))
    odd_row = lax.broadcasted_iota(jnp.int32, (C, C_WIDTH), 0) % 2 == 1

    for ci in range(tl // C):
        rows = slice(ci * C, (ci + 1) * C)
        g = g_all[rows, :]
        pieces = []
        rest = g
        for _ in range(HGRN_SPLIT):
            piece = rest.astype(BF16)
            pieces.append(piece)
            rest = rest - piece.astype(F32)
        sums = _dot(dm_ref[...], jnp.concatenate(pieces, axis=0))
        e = jnp.exp(jnp.minimum(sums, 0.0))
        e_one = jnp.exp(jnp.where(odd_row, g, 0.0))
        for h in range(C_HEADS):
            hs = slice(h * C_DK, (h + 1) * C_DK)
            q = q_all[rows, hs]
            k = k_all[rows, hs]
            vb = v_all[rows, hs].astype(BF16)
            scores = jnp.where(masks[0], _dot_nt(q.astype(BF16), k.astype(BF16)), 0.0)
            for li in range(1, nlev):
                el = e[(li + 1) * C:(li + 2) * C, hs] if li < nlev - 1 else e_one[:, hs]
                scores = scores + jnp.where(masks[li], _dot_nt((q * el).astype(BF16), (k * el).astype(BF16)), 0.0)
            e_cum = e[0:C, hs]
            qc = (q * e_cum).astype(BF16)
            kc = (k * e[C:2 * C, hs]).astype(BF16)
            st = states[h]
            o = _dot(scores.astype(BF16), vb) + _dot_nt(qc, st.astype(BF16))
            states[h] = st * e_cum[C - 1:C, :] + _dot_tn(vb, kc)
            ms = jnp.mean(o * o, axis=-1, keepdims=True)
            y_ref[0, rows, hs] = (o * lax.rsqrt(ms + NORM_EPS) * ng_ref[...] * gate_all[rows, hs]).astype(y_ref.dtype)
    for h in range(C_HEADS):
        st_s[h] = states[h]
        s_ref[0, h] = states[h].T


def _hgrn(x, w, b, lb, norm_g, s0):
    B, L, _ = x.shape
    C = min(CHUNK, L)
    tl = min(L, 512)
    dm = jnp.asarray(_hgrn_sum_matrices(C), BF16)
    const = lambda bi, j: (0, 0)
    per_b = lambda bi, j: (bi, 0, 0, 0)
    return pl.pallas_call(
        functools.partial(_hgrn_kernel, tl=tl, C=C),
        grid=(B, L // tl),
        in_specs=[pl.BlockSpec((1, tl, D_MODEL), lambda bi, j: (bi, j, 0)),
                  pl.BlockSpec((D_MODEL, 4 * C_WIDTH), const),
                  pl.BlockSpec((1, 4 * C_WIDTH), const),
                  pl.BlockSpec((1, C_WIDTH), const),
                  pl.BlockSpec((1, C_DV), const),
                  pl.BlockSpec(dm.shape, const),
                  pl.BlockSpec((1, C_HEADS, C_DK, C_DV), per_b)],
        out_specs=[pl.BlockSpec((1, tl, C_WIDTH), lambda bi, j: (bi, j, 0)),
                   pl.BlockSpec((1, C_HEADS, C_DK, C_DV), per_b)],
        out_shape=[jax.ShapeDtypeStruct((B, L, C_WIDTH), BF16),
                   jax.ShapeDtypeStruct((B, C_HEADS, C_DK, C_DV), F32)],
        scratch_shapes=[pltpu.VMEM((C_HEADS, C_DV, C_DK), F32)],
        compiler_params=_cparams("parallel", "arbitrary"),
        name="hgrn",
    )(x, w, b, lb, norm_g, dm, s0)


def _merge_kernel(x_ref, ya_ref, yb_ref, yc_ref, wg_ref, bg_ref, wo_ref, g1_ref, b1_ref, rw_ref, rb_ref,
                  o_ref, cls_ref, *, alpha):
    rs = slice(0, x_ref.shape[0])
    xb = x_ref[rs, :].astype(BF16)
    for c in range(D_MODEL // MERGE_COLS):
        cs = slice(c * MERGE_COLS, (c + 1) * MERGE_COLS)
        pre = alpha * x_ref[rs, cs]
        for n, y_ref in enumerate((ya_ref, yb_ref, yc_ref)):
            gs = slice(n * D_MODEL + c * MERGE_COLS, n * D_MODEL + (c + 1) * MERGE_COLS)
            gate = _sigmoid(_dot(xb, wg_ref[:, gs]) + bg_ref[:, gs])
            pre = pre + gate * _dot(y_ref[rs, :], wo_ref[n * A_WIDTH:(n + 1) * A_WIDTH, cs])
        o_ref[rs, cs] = pre
    x1 = _layer_norm(o_ref[rs, :D_MODEL], g1_ref[...], b1_ref[...])
    o_ref[rs, :D_MODEL] = x1

    lg = _dot(x1.astype(BF16), rw_ref[...]) + rb_ref[...]
    col = [lg[:, e:e + 1] for e in range(N_EXPERTS)]
    gmax = [functools.reduce(jnp.maximum, col[g * EXPERTS_PER_GROUP:(g + 1) * EXPERTS_PER_GROUP])
            for g in range(N_GROUPS)]
    best = functools.reduce(jnp.maximum, gmax)
    gs = jnp.where(gmax[0] == best, 0, jnp.where(gmax[1] == best, 1, jnp.where(gmax[2] == best, 2, 3)))
    v = []
    for e in range(EXPERTS_PER_GROUP):
        v.append(jnp.where(gs == 0, col[e],
                           jnp.where(gs == 1, col[EXPERTS_PER_GROUP + e],
                                     jnp.where(gs == 2, col[2 * EXPERTS_PER_GROUP + e],
                                               col[3 * EXPERTS_PER_GROUP + e]))))
    m1 = best
    i1 = jnp.where(v[0] == m1, 0, jnp.where(v[1] == m1, 1, jnp.where(v[2] == m1, 2, 3)))
    v2 = [jnp.where(i1 == e, -jnp.inf, v[e]) for e in range(EXPERTS_PER_GROUP)]
    m2 = functools.reduce(jnp.maximum, v2)
    i2 = jnp.where(v2[0] == m2, 0, jnp.where(v2[1] == m2, 1, jnp.where(v2[2] == m2, 2, 3)))
    e2 = jnp.exp(m2 - m1)
    p1 = 1.0 / (1.0 + e2)
    p2 = e2 / (1.0 + e2)
    ea = jnp.minimum(i1, i2)
    eb = jnp.maximum(i1, i2)
    ga = jnp.where(i1 < i2, p1, p2)
    gb = jnp.where(i1 < i2, p2, p1)
    pair = jnp.where(ea == 0, 0, jnp.where(ea == 1, 3, 5)) + (eb - ea - 1)
    cls = (gs * PAIRS_PER_GROUP + pair).astype(F32)
    lane = lax.broadcasted_iota(jnp.int32, (rs.stop - rs.start, AUX_W), 1)
    aux = jnp.where(lane == 0, ga, jnp.where(lane == 1, gb, jnp.where(lane == 2, cls, 0.0)))
    o_ref[rs, D_MODEL:] = aux
    pick = (lax.broadcasted_iota(jnp.int32, (8, AUX_W), 0) == 0) & (lax.broadcasted_iota(jnp.int32, (8, AUX_W), 1) == 2)
    cls_ref[:, rs] = _dot_nt(jnp.where(pick, 1.0, 0.0).astype(BF16), aux.astype(BF16))


def _merge(x2d, ya, yb, yc, wg, bg, wo, g1, b1, rw, rb, alpha):
    T = x2d.shape[0]
    tm = min(T, MERGE_ROWS)
    row = lambda i: (i, 0)
    const = lambda i: (0, 0)
    return pl.pallas_call(
        functools.partial(_merge_kernel, alpha=alpha),
        grid=(T // tm,),
        in_specs=[pl.BlockSpec((tm, D_MODEL), row),
                  pl.BlockSpec((tm, A_WIDTH), row),
                  pl.BlockSpec((tm, B_WIDTH), row),
                  pl.BlockSpec((tm, C_WIDTH), row),
                  pl.BlockSpec((D_MODEL, N_BRANCH * D_MODEL), const),
                  pl.BlockSpec((1, N_BRANCH * D_MODEL), const),
                  pl.BlockSpec((N_BRANCH * A_WIDTH, D_MODEL), const),
                  pl.BlockSpec((1, D_MODEL), const),
                  pl.BlockSpec((1, D_MODEL), const),
                  pl.BlockSpec((D_MODEL, N_EXPERTS), const),
                  pl.BlockSpec((1, N_EXPERTS), const)],
        out_specs=[pl.BlockSpec((tm, D_MODEL + AUX_W), row), pl.BlockSpec((8, tm), lambda i: (0, i))],
        out_shape=[jax.ShapeDtypeStruct((T, D_MODEL + AUX_W), F32), jax.ShapeDtypeStruct((8, T), F32)],
        compiler_params=_cparams("parallel"),
        name="merge",
    )(x2d, ya, yb, yc, wg, bg, wo, g1, b1, rw, rb)


def _dispatch_kernel(dest_ref, x_ref, out_hbm, zbuf, sem, *, tm, n_tok_steps):
    i = pl.program_id(0)

    def scatter_rows(src):
        for r in range(tm):
            pltpu.make_async_copy(src.at[pl.ds(r, 1), :], out_hbm.at[pl.ds(dest_ref[0, 0, r], 1), :], sem).start()
        pltpu.make_async_copy(src, out_hbm.at[pl.ds(0, tm), :], sem).wait()

    @pl.when(i < n_tok_steps)
    def _():
        scatter_rows(x_ref)

    @pl.when(i >= n_tok_steps)
    def _():
        @pl.when(i == n_tok_steps)
        def _():
            zbuf[...] = jnp.zeros(zbuf.shape, F32)

        scatter_rows(zbuf)


def _dispatch(x1w, dest_all, n_rows):
    T, W = x1w.shape
    tm = min(T, MOE_DMA_ROWS)
    assert T % tm == 0 and dest_all.shape[0] % tm == 0
    n_tok_steps = T // tm
    steps = dest_all.shape[0] // tm
    return pl.pallas_call(
        functools.partial(_dispatch_kernel, tm=tm, n_tok_steps=n_tok_steps),
        grid=(steps,),
        in_specs=[pl.BlockSpec((1, 1, tm), lambda i: (i, 0, 0), memory_space=pltpu.SMEM),
                  pl.BlockSpec((tm, W), lambda i: (jnp.minimum(i, n_tok_steps - 1), 0))],
        out_specs=pl.BlockSpec(memory_space=pl.ANY),
        out_shape=jax.ShapeDtypeStruct((n_rows, W), F32),
        scratch_shapes=[pltpu.VMEM((tm, W), F32), pltpu.SemaphoreType.DMA(())],
        compiler_params=_cparams("arbitrary"),
        name="moe_dispatch",
    )(dest_all.reshape(steps, 1, tm), x1w)


def _experts_kernel(ea_ref, eb_ref, nv_ref, x_ref, w1a, w3a, w2a, w1b, w3b, w2b, g2_ref, b2_ref, o_ref, *, alpha):
    i = pl.program_id(0)

    @pl.when(nv_ref[i] > 0)
    def _():
        xw = x_ref[...]
        x = xw[:, :D_MODEL]
        xb = x.astype(BF16)
        y = None
        for n, (w1, w3, w2) in enumerate(((w1a, w3a, w2a), (w1b, w3b, w2b))):
            hmid = _silu(_dot(xb, w1[0])) * _dot(xb, w3[0])
            ye = xw[:, D_MODEL + n:D_MODEL + n + 1] * _dot(hmid.astype(BF16), w2[0])
            y = ye if y is None else y + ye
        o_ref[...] = _layer_norm(alpha * x + y, g2_ref[...], b2_ref[...])

    @pl.when(nv_ref[i] == 0)
    def _():
        o_ref[...] = jnp.zeros(o_ref.shape, F32)


def _experts(xs, ea, eb, nv, w1, w3, w2, g2, b2, alpha):
    n_rows, W = xs.shape
    wa = lambda i, ea, eb, nv: (ea[i], 0, 0)
    wb = lambda i, ea, eb, nv: (eb[i], 0, 0)
    const = lambda i, ea, eb, nv: (0, 0)
    row = lambda i, ea, eb, nv: (i, 0)
    up = (1, D_MODEL, D_EXPERT)
    down = (1, D_EXPERT, D_MODEL)
    grid_spec = pltpu.PrefetchScalarGridSpec(
        num_scalar_prefetch=3,
        grid=(n_rows // MOE_ROWS,),
        in_specs=[pl.BlockSpec((MOE_ROWS, W), row),
                  pl.BlockSpec(up, wa), pl.BlockSpec(up, wa), pl.BlockSpec(down, wa),
                  pl.BlockSpec(up, wb), pl.BlockSpec(up, wb), pl.BlockSpec(down, wb),
                  pl.BlockSpec((1, D_MODEL), const), pl.BlockSpec((1, D_MODEL), const)],
        out_specs=pl.BlockSpec((MOE_ROWS, D_MODEL), row),
    )
    return pl.pallas_call(
        functools.partial(_experts_kernel, alpha=alpha),
        grid_spec=grid_spec,
        out_shape=jax.ShapeDtypeStruct((n_rows, D_MODEL), F32),
        compiler_params=_cparams("arbitrary"),
        name="moe_experts",
    )(ea, eb, nv, xs, w1, w3, w2, w1, w3, w2, g2, b2)


def _combine_kernel(dest_ref, y_hbm, o_ref, sem, *, tm):
    for r in range(tm):
        pltpu.make_async_copy(y_hbm.at[pl.ds(dest_ref[0, 0, r], 1), :], o_ref.at[pl.ds(r, 1), :], sem).start()
    pltpu.make_async_copy(y_hbm.at[pl.ds(0, tm), :], o_ref, sem).wait()


def _combine(ys, dest, T):
    tm = min(T, MOE_DMA_ROWS)
    return pl.pallas_call(
        functools.partial(_combine_kernel, tm=tm),
        grid=(T // tm,),
        in_specs=[pl.BlockSpec((1, 1, tm), lambda i: (i, 0, 0), memory_space=pltpu.SMEM),
                  pl.BlockSpec(memory_space=pl.ANY)],
        out_specs=pl.BlockSpec((tm, D_MODEL), lambda i: (i, 0)),
        out_shape=jax.ShapeDtypeStruct((T, D_MODEL), F32),
        scratch_shapes=[pltpu.SemaphoreType.DMA(())],
        compiler_params=_cparams("arbitrary"),
        name="moe_combine",
    )(dest.reshape(T // tm, 1, tm), ys)


def _route(cls, T):
    n_blocks = T // MOE_ROWS + N_CLASSES
    n_rows = n_blocks * MOE_ROWS
    rb = math.gcd(T, MOE_ROWS)
    onehot = (cls[:, None] == jnp.arange(N_CLASSES, dtype=jnp.int32)[None, :]).astype(F32).reshape(T // rb, rb, N_CLASSES)
    strict = jnp.asarray(np.tril(np.ones((rb, rb), np.float32), -1))
    rank_in = jnp.einsum('ij,bjc->bic', strict, onehot)
    blk_counts = jnp.sum(onehot, axis=1)
    blk_off = jnp.cumsum(blk_counts, axis=0) - blk_counts
    rank = jnp.sum(onehot * (rank_in + blk_off[:, None, :]), axis=-1).reshape(T).astype(jnp.int32)
    counts = jnp.sum(blk_counts, axis=0).astype(jnp.int32)
    padded = (counts + MOE_ROWS - 1) // MOE_ROWS * MOE_ROWS
    pad_end = jnp.cumsum(padded)
    pad_start = pad_end - padded
    dest = (pad_start[cls] + rank).astype(jnp.int32)
    seg_start = jnp.concatenate([pad_start + counts, pad_end[-1:]])
    seg_len = jnp.concatenate([padded - counts, n_rows - pad_end[-1:]])
    seg_end = jnp.cumsum(seg_len)
    kk = jnp.arange(n_rows - T, dtype=jnp.int32)
    seg = jnp.sum((seg_end[None, :] <= kk[:, None]).astype(jnp.int32), axis=1)
    pad_rows = (seg_start[seg] + kk - (seg_end[seg] - seg_len[seg])).astype(jnp.int32)
    blk_start = jnp.arange(n_blocks, dtype=jnp.int32) * MOE_ROWS
    blk_cls = jnp.minimum(jnp.sum((pad_end[None, :] <= blk_start[:, None]).astype(jnp.int32), axis=1), N_CLASSES - 1)
    nv = jnp.clip(pad_start[blk_cls] + counts[blk_cls] - blk_start, 0, MOE_ROWS).astype(jnp.int32)
    grp = blk_cls // PAIRS_PER_GROUP
    pair = blk_cls % PAIRS_PER_GROUP
    pa = jnp.asarray([0, 0, 0, 1, 1, 2], jnp.int32)[pair]
    pb = jnp.asarray([1, 2, 3, 2, 3, 3], jnp.int32)[pair]
    ea = grp * EXPERTS_PER_GROUP + pa
    eb = grp * EXPERTS_PER_GROUP + pb
    return ea, eb, nv, dest, pad_rows, n_rows


def _moe(x1w, cls_rows, w1, w3, w2, g2, b2, alpha):
    T = x1w.shape[0]
    cls = cls_rows[0].astype(jnp.int32)
    ea, eb, nv, dest, pad_rows, n_rows = _route(cls, T)
    xs = _dispatch(x1w, jnp.concatenate([dest, pad_rows]), n_rows)
    ys = _experts(xs, ea, eb, nv, w1, w3, w2, g2, b2, alpha)
    return _combine(ys, dest, T)


def _block_diag(w):
    n, d, _ = w.shape
    eye = jnp.eye(n, dtype=w.dtype)
    return jnp.einsum('nde,nm->ndme', w, eye).reshape(n * d, n * d)


def _layer(x, pos0, li, P, past, kv_carried):
    B, L, _ = x.shape
    T = B * L
    depth = P['depth']
    alpha = (2 * depth) ** 0.25
    w_in, b_in = P['w_in'][li], P['b_in'][li]

    def cols(lo, hi):
        return w_in[:, lo:hi], b_in[lo:hi].reshape(1, -1)

    lam_init = 0.8 - 0.6 * math.exp(-0.3 * li)
    dl = P['diff_lambda'][li]
    lam = jnp.exp(jnp.sum(dl[0] * dl[1])) - jnp.exp(jnp.sum(dl[2] * dl[3])) + lam_init
    lam_row = jnp.full((1, LANES), lam, F32)
    subln_g = P['diff_subln_g'][li].reshape(1, -1)

    wa, ba = cols(OFF_A, OFF_B)
    q, k_all, kb, v_all, vb, vt = _attn_proj(x, wa, ba, pos0, li, depth, kv_carried)
    if past is None:
        y_a = _flash_prompt(q, kb, vt, lam_row, subln_g.reshape(-1, 1), 1.0 - lam_init)
        conv_buf = jnp.zeros((B, CONV_W - 1, B_WIDTH), F32)
        h0 = jnp.zeros((B, 1, B_WIDTH), F32)
        s0 = jnp.zeros((B, C_HEADS, C_DK, C_DV), F32)
    else:
        kt_cache, v_cache, conv_buf, h0, s0 = past
        y_a = _decode_attention(q, kb, vb, kt_cache, v_cache, li, lam_row, subln_g, 1.0 - lam_init)
        h0 = h0.reshape(B, 1, B_WIDTH)

    wb, bb = cols(OFF_B, OFF_C)
    y_b, conv_new, h_new = _rglru(x, wb, bb, P['conv_w'][li], P['conv_b'][li].reshape(1, -1),
                                  P['lru_gate_w'][li], P['lru_gate_b'][li], P['lru_sp'][li].reshape(1, -1),
                                  conv_buf, h0, pos0)
    wc, bc = cols(OFF_C, OFF_G)
    y_c, s_new = _hgrn(x, wc, bc, P['lower_bounds'][li].reshape(1, -1), P['hgrn_norm_g'][li].reshape(1, -1), s0)

    wg, bg = cols(OFF_G, N_IN)
    x1w, cls_rows = _merge(x.reshape(T, D_MODEL), y_a.reshape(T, A_WIDTH), y_b.reshape(T, B_WIDTH),
                 y_c.reshape(T, C_WIDTH), wg, bg, P['w_out'][li], P['ln1_g'][li].reshape(1, -1),
                 P['ln1_b'][li].reshape(1, -1), P['router_w'], P['router_b'].reshape(1, -1), alpha)
    x2 = _moe(x1w, cls_rows, P['moe_w1'][li], P['moe_w3'][li], P['moe_w2'][li], P['ln2_g'][li].reshape(1, -1),
              P['ln2_b'][li].reshape(1, -1), alpha)
    states = (conv_new, h_new.reshape(B, B_WIDTH), s_new)
    return x2.reshape(B, L, D_MODEL), (k_all, v_all), states


def kernel(x_prompt, x_sample, cache_k, cache_v, state_conv, state_lru, state_hgrn, ln_in_g, ln_in_b, w_in, b_in, diff_lambda, diff_subln_g, conv_w, conv_b, lru_wa, lru_ba, lru_wx, lru_bx, lru_lambda, hgrn_lb, hgrn_norm_g, w_out, ln1_g, ln1_b, router_w, router_b, moe_w1, moe_w3, moe_w2, ln2_g, ln2_b):
    depth = w_in.shape[0]
    p_lb = jax.nn.softmax(hgrn_lb.astype(F32), axis=0)
    P = dict(
        depth=depth,
        w_in=w_in.astype(BF16), b_in=b_in,
        diff_lambda=diff_lambda.astype(F32), diff_subln_g=diff_subln_g,
        conv_w=conv_w, conv_b=conv_b,
        lru_gate_w=jnp.stack([jnp.concatenate([_block_diag(lru_wa[l]), _block_diag(lru_wx[l])], axis=1)
                              for l in range(depth)]).astype(BF16),
        lru_gate_b=jnp.concatenate([lru_ba, lru_bx], axis=1).reshape(depth, 1, -1),
        lru_sp=jax.nn.softplus(-lru_lambda.astype(F32)),
        lower_bounds=jnp.cumsum(p_lb, axis=0) - p_lb[0],
        hgrn_norm_g=hgrn_norm_g,
        w_out=w_out.astype(BF16), ln1_g=ln1_g, ln1_b=ln1_b,
        router_w=router_w.astype(BF16), router_b=router_b,
        moe_w1=moe_w1.astype(BF16), moe_w3=moe_w3.astype(BF16), moe_w2=moe_w2.astype(BF16),
        ln2_g=ln2_g, ln2_b=ln2_b,
    )
    Bp, Lp, _ = x_prompt.shape
    Bs, Ls, _ = x_sample.shape
    past_len = cache_k.shape[2]
    kt_cache = jnp.transpose(cache_k, (0, 1, 3, 4, 5, 2)).reshape(depth, Bs, A_QK_WIDTH, past_len)
    v_cache = cache_v.reshape(depth, Bs, past_len * A_HEADS, A_V_DIM)
    xp = _input_ln(x_prompt.reshape(Bp * Lp, D_MODEL), ln_in_g, ln_in_b).reshape(Bp, Lp, D_MODEL)
    xs = _input_ln(x_sample.reshape(Bs * Ls, D_MODEL), ln_in_g, ln_in_b).reshape(Bs, Ls, D_MODEL)
    prompt_states, sample_states = [], []
    kv_p = kv_s = None
    for li in range(depth):
        xp, kv_p, st_p = _layer(xp, 0, li, P, None, kv_p)
        xs, kv_s, st_s = _layer(xs, past_len, li, P,
                                (kt_cache, v_cache, state_conv[li], state_lru[li], state_hgrn[li]), kv_s)
        prompt_states.append(st_p)
        sample_states.append(st_s)
    conv_p, lru_p, hgrn_p = [jnp.stack(t) for t in zip(*prompt_states)]
    conv_s, lru_s, hgrn_s = [jnp.stack(t) for t in zip(*sample_states)]
    k_p = kv_p[0].reshape(depth, Bp, Lp, A_HEADS, 2, A_HEAD_DIM)
    v_p = kv_p[1].reshape(depth, Bp, Lp, A_HEADS, A_V_DIM)
    k_s = kv_s[0].reshape(depth, Bs, Ls, A_HEADS, 2, A_HEAD_DIM)
    v_s = kv_s[1].reshape(depth, Bs, Ls, A_HEADS, A_V_DIM)
    return (xp, xs, k_p, v_p, conv_p, lru_p, hgrn_p, k_s, v_s, conv_s, lru_s, hgrn_s)
```

```python
import functools
import math

import numpy as np
import jax
import jax.numpy as jnp
from jax import lax
from jax.experimental import pallas as pl
from jax.experimental.pallas import tpu as pltpu

F32 = jnp.float32
BF16 = jnp.bfloat16

D_MODEL = 1024
CHUNK = 64
A_HEADS = 4
A_HEAD_DIM = 64
A_V_DIM = 2 * A_HEAD_DIM
A_QK_WIDTH = A_HEADS * 2 * A_HEAD_DIM
A_WIDTH = A_HEADS * A_V_DIM
ROPE_THETA = 10000.0
NEG_INF = -1e30
Q_SCALE = A_HEAD_DIM ** -0.5 * math.log2(math.e)
B_WIDTH = 512
B_BLOCKS = 8
CONV_W = 4
LRU_C = 8.0
C_HEADS = 4
C_DK = 128
C_DV = 128
C_WIDTH = C_HEADS * C_DV
F_TINY = 1e-30
N_BRANCH = 3
N_EXPERTS = 16
N_GROUPS = 4
EXPERTS_PER_GROUP = N_EXPERTS // N_GROUPS
D_EXPERT = 512
NORM_EPS = 1e-5

LANES = 128
SUBLANES = 8
PAIRS_PER_GROUP = 6
N_CLASSES = N_GROUPS * PAIRS_PER_GROUP
MOE_ROWS = 256
MERGE_COLS = 256
MERGE_ROWS = 512
MOE_DMA_ROWS = 2048
AUX_W = LANES
VMEM_LIMIT = 56 * 1024 * 1024

OFF_A = 0
OFF_B = OFF_A + 2 * A_QK_WIDTH + A_WIDTH
OFF_C = OFF_B + 2 * B_WIDTH
OFF_G = OFF_C + 4 * C_WIDTH
N_IN = OFF_G + N_BRANCH * D_MODEL


def _cparams(*sem):
    return pltpu.CompilerParams(dimension_semantics=sem, vmem_limit_bytes=VMEM_LIMIT)


def _dot(a, b):
    return jnp.dot(a, b, preferred_element_type=F32)


def _dot_nt(a, b):
    return lax.dot_general(a, b, (((1,), (1,)), ((), ())), preferred_element_type=F32)


def _dot_tn(a, b):
    return lax.dot_general(a, b, (((0,), (0,)), ((), ())), preferred_element_type=F32)


def _sigmoid(x):
    return 1.0 / (1.0 + jnp.exp(-x))


def _silu(x):
    return x * _sigmoid(x)


def _layer_norm(x, g, b):
    mu = jnp.mean(x, axis=-1, keepdims=True)
    xc = x - mu
    var = jnp.mean(xc * xc, axis=-1, keepdims=True)
    return xc * lax.rsqrt(var + NORM_EPS) * g + b


def _ln_kernel(x_ref, g_ref, b_ref, o_ref):
    o_ref[...] = _layer_norm(x_ref[...], g_ref[...], b_ref[...])


def _input_ln(x2d, g, b):
    T = x2d.shape[0]
    tm = min(T, 512)
    return pl.pallas_call(
        _ln_kernel,
        grid=(T // tm,),
        in_specs=[pl.BlockSpec((tm, D_MODEL), lambda i: (i, 0)),
                  pl.BlockSpec((1, D_MODEL), lambda i: (0, 0)),
                  pl.BlockSpec((1, D_MODEL), lambda i: (0, 0))],
        out_specs=pl.BlockSpec((tm, D_MODEL), lambda i: (i, 0)),
        out_shape=jax.ShapeDtypeStruct((T, D_MODEL), F32),
        compiler_params=_cparams("parallel"),
        name="input_ln",
    )(x2d, g.reshape(1, -1), b.reshape(1, -1))


def _attn_proj_kernel(x_ref, w_ref, b_ref, cos_ref, sin_ref, *refs):
    q_ref, k_ref, kb_ref, v_ref, vb_ref, vt_ref = refs[-6:]
    x = x_ref[0].astype(BF16)
    y = _dot(x, w_ref[...]) + b_ref[...]
    cos = cos_ref[...]
    sin = sin_ref[...]
    lane = lax.broadcasted_iota(jnp.int32, cos.shape, 1)
    first_half = (lane % A_HEAD_DIM) < (A_HEAD_DIM // 2)
    for s in range(2 * A_QK_WIDTH // LANES):
        blk = y[:, s * LANES:(s + 1) * LANES]
        swapped = jnp.where(first_half,
                            pltpu.roll(blk, LANES - A_HEAD_DIM // 2, 1),
                            pltpu.roll(blk, A_HEAD_DIM // 2, 1))
        rot = blk * cos + swapped * sin
        if s < A_QK_WIDTH // LANES:
            q_ref[0, :, s * LANES:(s + 1) * LANES] = (rot * Q_SCALE).astype(BF16)
        else:
            o = s * LANES - A_QK_WIDTH
            k_ref[0, 0, :, o:o + LANES] = rot
            kb_ref[0, :, o:o + LANES] = rot.astype(BF16)
    v = y[:, 2 * A_QK_WIDTH:]
    for h in range(A_HEADS):
        v_ref[0, 0, pl.ds(h, v.shape[0], stride=A_HEADS), :] = v[:, h * A_V_DIM:(h + 1) * A_V_DIM]
    vb_ref[0] = v.astype(BF16)
    vt_ref[0] = v.T.astype(BF16)
    for later in range(1, k_ref.shape[0]):
        k_ref[later] = jnp.zeros(k_ref.shape[1:], F32)
        v_ref[later] = jnp.zeros(v_ref.shape[1:], F32)


def _rope_tables(pos0, L):
    half = A_HEAD_DIM // 2
    inv_freq = ROPE_THETA ** (-jnp.arange(half, dtype=F32) / half)
    pos = (pos0 + jnp.arange(L)).astype(F32)
    ang = pos[:, None] * inv_freq[None, :]
    cos = jnp.cos(ang)
    sin = jnp.sin(ang)
    reps = LANES // A_HEAD_DIM
    cos_t = jnp.tile(jnp.concatenate([cos, cos], axis=1), (1, reps))
    sin_t = jnp.tile(jnp.concatenate([-sin, sin], axis=1), (1, reps))
    return cos_t, sin_t


def _attn_proj(x, w, b, pos0, li, depth, carried):
    B, L, _ = x.shape
    tl = min(L, 512)
    cos_t, sin_t = _rope_tables(pos0, L)
    wa = A_QK_WIDTH
    n = 2 * A_QK_WIDTH + A_WIDTH
    row = lambda bi, j: (bi, j, 0)
    layer_row = lambda bi, j: (li, bi, j, 0)
    carried = () if carried is None else tuple(carried)
    n_layers = 1 if carried else depth
    assert carried or li == 0
    n_in = 5
    return pl.pallas_call(
        _attn_proj_kernel,
        grid=(B, L // tl),
        in_specs=[pl.BlockSpec((1, tl, D_MODEL), row),
                  pl.BlockSpec((D_MODEL, n), lambda bi, j: (0, 0)),
                  pl.BlockSpec((1, n), lambda bi, j: (0, 0)),
                  pl.BlockSpec((tl, LANES), lambda bi, j: (j, 0)),
                  pl.BlockSpec((tl, LANES), lambda bi, j: (j, 0))]
        + [pl.BlockSpec(memory_space=pl.ANY)] * len(carried),
        out_specs=[pl.BlockSpec((1, tl, wa), row),
                   pl.BlockSpec((n_layers, 1, tl, wa), layer_row),
                   pl.BlockSpec((1, tl, wa), row),
                   pl.BlockSpec((n_layers, 1, tl * A_HEADS, A_V_DIM), layer_row),
                   pl.BlockSpec((1, tl, wa), row),
                   pl.BlockSpec((1, wa, tl), lambda bi, j: (bi, 0, j))],
        out_shape=[jax.ShapeDtypeStruct((B, L, wa), BF16),
                   jax.ShapeDtypeStruct((depth, B, L, wa), F32),
                   jax.ShapeDtypeStruct((B, L, wa), BF16),
                   jax.ShapeDtypeStruct((depth, B, L * A_HEADS, A_V_DIM), F32),
                   jax.ShapeDtypeStruct((B, L, wa), BF16),
                   jax.ShapeDtypeStruct((B, wa, L), BF16)],
        input_output_aliases={n_in: 1, n_in + 1: 3} if carried else {},
        compiler_params=_cparams("parallel", "parallel"),
        name="attn_proj",
    )(x, w, b, cos_t, sin_t, *carried)


def _split_maps(q):
    lane = lax.broadcasted_iota(jnp.int32, q.shape, 1)
    zero = jnp.zeros_like(q)
    return jnp.where(lane < A_HEAD_DIM, q, zero), jnp.where(lane >= A_HEAD_DIM, q, zero)


def _flash_kernel(lam_ref, g_ref, q_ref, k_ref, vt_ref, o_ref, m_s, l_s, acc_s, sa_s, sb_s, *, tq, out_scale):
    i = pl.program_id(2)
    qz = _split_maps(q_ref[0])
    qq = jnp.concatenate(qz, axis=0)
    m_s[...] = jnp.full(m_s.shape, NEG_INF, F32)
    l_s[...] = jnp.zeros(l_s.shape, F32)
    acc_s[...] = jnp.zeros(acc_s.shape, F32)

    def scores_into(dst, blk):
        off = pl.multiple_of(blk * tq, tq)
        dst[...] = _dot_nt(k_ref[0, pl.ds(off, tq), :], qq)

    def update(src, blk, diagonal):
        off = pl.multiple_of(blk * tq, tq)
        vt = vt_ref[0, :, pl.ds(off, tq)]
        s = src[...]
        if diagonal:
            key = lax.broadcasted_iota(jnp.int32, (tq, 2 * tq), 0) // CHUNK
            qry = (lax.broadcasted_iota(jnp.int32, (tq, 2 * tq), 1) % tq) // CHUNK
            s = jnp.where(key <= qry, s, NEG_INF)
        m_old = m_s[...]
        m_new = jnp.maximum(m_old, jnp.max(s, axis=0, keepdims=True))
        alpha = jnp.exp2(m_old - m_new)
        p = jnp.exp2(s - m_new)
        l_s[...] = alpha * l_s[...] + jnp.sum(p, axis=0, keepdims=True)
        acc_s[...] = alpha * acc_s[...] + _dot(vt, p.astype(BF16))
        m_s[...] = m_new

    scores_into(sa_s, 0)

    def pair(p):
        scores_into(sb_s, 2 * p + 1)
        update(sa_s, 2 * p, False)
        scores_into(sa_s, 2 * p + 2)
        update(sb_s, 2 * p + 1, False)

    def two_pairs(t, carry):
        pair(2 * t)
        pair(2 * t + 1)
        return carry

    n_pairs = i // 2
    lax.fori_loop(0, n_pairs // 2, two_pairs, 0)

    @pl.when(n_pairs % 2 == 1)
    def _():
        pair(n_pairs - 1)

    @pl.when(i % 2 == 1)
    def _():
        scores_into(sb_s, i)
        update(sa_s, i - 1, False)
        update(sb_s, i, True)

    @pl.when(i % 2 == 0)
    def _():
        update(sa_s, i, True)

    inv_l = 1.0 / l_s[...]
    o = acc_s[...] * inv_l
    o = o[:, :tq] - lam_ref[0:1, 0:1] * o[:, tq:]
    ms = jnp.mean(o * o, axis=0, keepdims=True)
    y = o * lax.rsqrt(ms + NORM_EPS) * (g_ref[...] * out_scale)
    o_ref[0] = y.T.astype(o_ref.dtype)


def _flash_prompt(q, kb, vt, lam_row, g_col, out_scale):
    B, L, _ = q.shape
    tq = min(L, 512)
    return pl.pallas_call(
        functools.partial(_flash_kernel, tq=tq, out_scale=out_scale),
        grid=(B, A_HEADS, L // tq),
        in_specs=[pl.BlockSpec((1, LANES), lambda b, h, i: (0, 0)),
                  pl.BlockSpec((A_V_DIM, 1), lambda b, h, i: (0, 0)),
                  pl.BlockSpec((1, tq, LANES), lambda b, h, i: (b, i, h)),
                  pl.BlockSpec((1, L, LANES), lambda b, h, i: (b, 0, h)),
                  pl.BlockSpec((1, A_V_DIM, L), lambda b, h, i: (b, h, 0))],
        out_specs=pl.BlockSpec((1, tq, A_V_DIM), lambda b, h, i: (b, i, h)),
        out_shape=jax.ShapeDtypeStruct((B, L, A_WIDTH), BF16),
        scratch_shapes=[pltpu.VMEM((1, 2 * tq), F32), pltpu.VMEM((1, 2 * tq), F32),
                        pltpu.VMEM((A_V_DIM, 2 * tq), F32),
                        pltpu.VMEM((tq, 2 * tq), F32), pltpu.VMEM((tq, 2 * tq), F32)],
        compiler_params=_cparams("parallel", "parallel", "arbitrary"),
        name="flash_prompt",
    )(lam_row, g_col, q, kb, vt)


def _decode_kernel(lam_ref, g_ref, q_ref, kn_ref, vn_ref, kp_ref, vp_ref, o_ref, s_s, acc_s, m_s, r_s,
                   *, nb, tk, L, out_scale):
    j = pl.program_id(1)
    P = nb * tk
    n_new = kn_ref.shape[1]

    def q_rows():
        rep = jnp.concatenate([q_ref[0]] * (LANES // L), axis=0)
        grp = lax.broadcasted_iota(jnp.int32, rep.shape, 0) // L
        cols = lax.broadcasted_iota(jnp.int32, rep.shape, 1) // A_HEAD_DIM
        return jnp.where(grp == cols, rep, jnp.zeros_like(rep))

    @pl.when(j < nb)
    def _():
        off = pl.multiple_of(j * tk, tk)
        s_s[:, pl.ds(off, tk)] = _dot(q_rows(), kp_ref[0, 0].astype(BF16))

    @pl.when(j == nb - 1)
    def _():
        s_new = _dot_nt(q_rows(), kn_ref[0])
        valid = lax.broadcasted_iota(jnp.int32, s_new.shape, 1) < L
        s_s[:, P:P + n_new] = jnp.where(valid, s_new, NEG_INF)
        s = s_s[...]
        m = jnp.max(s, axis=1, keepdims=True)
        m_s[...] = m
        r_s[...] = 1.0 / jnp.sum(jnp.exp2(s - m), axis=1, keepdims=True)
        acc_s[...] = jnp.zeros(acc_s.shape, F32)

    def weights(cols):
        return (jnp.exp2(s_s[:, cols] - m_s[...]) * r_s[...]).astype(BF16)

    @pl.when(j >= nb)
    def _():
        off = pl.multiple_of((j - nb) * tk, tk)
        w = weights(pl.ds(off, tk))
        for h in range(A_HEADS):
            rows = slice(2 * h * L, (2 * h + 2) * L)
            vh = vp_ref[0, 0, pl.ds(h, tk, stride=A_HEADS), :]
            acc_s[rows, :] += _dot(w[rows, :], vh.astype(BF16))

    @pl.when(j == 2 * nb - 1)
    def _():
        w = weights(slice(P, P + n_new))
        for h in range(A_HEADS):
            hs = slice(h * A_V_DIM, (h + 1) * A_V_DIM)
            rows = slice(2 * h * L, (2 * h + 2) * L)
            acc = acc_s[rows, :] + _dot(w[rows, :], vn_ref[0, :, hs])
            o = acc[:L] - lam_ref[...] * acc[L:]
            ms = jnp.mean(o * o, axis=-1, keepdims=True)
            o_ref[0, :, hs] = (o * lax.rsqrt(ms + NORM_EPS) * g_ref[...] * out_scale).astype(o_ref.dtype)


def _decode_attention(q, k_new, v_new, kt_cache, v_cache, li, lam_row, g_row, out_scale):
    B, L, _ = q.shape
    assert 2 * A_HEADS * L == LANES, "decode attention packs (head, map, frame) onto the 128 lanes"
    P = kt_cache.shape[3]
    tk = min(P, 2048)
    nb = P // tk
    pad = ((0, 0), (0, LANES - L), (0, 0))
    k_new = jnp.pad(k_new, pad)
    v_new = jnp.pad(v_new, pad)
    new = lambda b, j: (b, 0, 0)
    return pl.pallas_call(
        functools.partial(_decode_kernel, nb=nb, tk=tk, L=L, out_scale=out_scale),
        grid=(B, 2 * nb),
        in_specs=[pl.BlockSpec((1, LANES), lambda b, j: (0, 0)),
                  pl.BlockSpec((1, A_V_DIM), lambda b, j: (0, 0)),
                  pl.BlockSpec((1, L, A_QK_WIDTH), new),
                  pl.BlockSpec((1, LANES, A_QK_WIDTH), new),
                  pl.BlockSpec((1, LANES, A_WIDTH), new),
                  pl.BlockSpec((1, 1, A_QK_WIDTH, tk), lambda b, j: (li, b, 0, jnp.minimum(j, nb - 1))),
                  pl.BlockSpec((1, 1, tk * A_HEADS, A_V_DIM),
                               lambda b, j: (li, b, jnp.maximum(j - nb, 0), 0))],
        out_specs=pl.BlockSpec((1, L, A_WIDTH), new),
        out_shape=jax.ShapeDtypeStruct((B, L, A_WIDTH), BF16),
        scratch_shapes=[pltpu.VMEM((LANES, P + LANES), F32), pltpu.VMEM((LANES, A_V_DIM), F32),
                        pltpu.VMEM((LANES, 1), F32), pltpu.VMEM((LANES, 1), F32)],
        compiler_params=_cparams("parallel", "arbitrary"),
        name="decode_attention",
    )(lam_row, g_row, q, k_new, v_new, kt_cache, v_cache)


def _gelu_tanh(x):
    return 0.5 * x * (1.0 + jnp.tanh(math.sqrt(2.0 / math.pi) * (x + 0.044715 * (x * x * x))))


def _rglru_kernel(x_ref, w_ref, b_ref, cw_ref, cb_ref, gw_ref, gbias_ref, sp_ref, cbuf_ref, h0_ref,
                  y_ref, conv_ref, h_ref, xs, hc, hs, *, tl, pos0):
    j = pl.program_id(1)
    tail = CONV_W - 1
    base = SUBLANES

    @pl.when(j == 0)
    def _():
        xs[0:base, :] = jnp.zeros((base, B_WIDTH), F32)
        xs[base - tail:base, :] = cbuf_ref[0]
        hc[...] = h0_ref[0]

    x = x_ref[0].astype(BF16)
    y = _dot(x, w_ref[...]) + b_ref[...]
    xb = y[:, :B_WIDTH]
    gb = y[:, B_WIDTH:]
    xs[base:base + tl, :] = xb
    xc = cb_ref[...] + cw_ref[0:1, :] * xs[base - tail:base - tail + tl, :]
    for t in range(1, CONV_W):
        xc = xc + cw_ref[t:t + 1, :] * xs[base - tail + t:base - tail + t + tl, :]
    new_tail = xs[base + tl - tail:base + tl, :]
    xs[base - tail:base, :] = new_tail
    conv_ref[0] = new_tail

    gates = _dot(xc.astype(BF16), gw_ref[...]) + gbias_ref[...]
    r = _sigmoid(gates[:, :B_WIDTH])
    ig = _sigmoid(gates[:, B_WIDTH:])
    log_a = -LRU_C * r * sp_ref[...]
    a = jnp.exp(log_a)
    t = jnp.tanh(log_a)
    mult = jnp.sqrt(-2.0 * t / (1.0 - t))
    row = lax.broadcasted_iota(jnp.int32, (tl, B_WIDTH), 0)
    mult = jnp.where(row + (pos0 + j * tl) == 0, 1.0, mult)
    bb = xc * ig * mult
    a3 = a.reshape(tl // SUBLANES, SUBLANES, B_WIDTH)
    b3 = bb.reshape(tl // SUBLANES, SUBLANES, B_WIDTH)
    sub = lax.broadcasted_iota(jnp.int32, a3.shape, 1)
    d = 1
    while d < SUBLANES:
        keep = sub >= d
        a_sh = pltpu.roll(a3, d, 1)
        b_sh = pltpu.roll(b3, d, 1)
        b3 = jnp.where(keep, a3 * b_sh + b3, b3)
        a3 = jnp.where(keep, a3 * a_sh, a3)
        d *= 2
    carry = hc[...]
    for g in range(tl // SUBLANES):
        hg = a3[g] * carry + b3[g]
        hs[g * SUBLANES:(g + 1) * SUBLANES, :] = hg
        carry = hg[SUBLANES - 1:SUBLANES, :]
    hc[...] = carry
    h_ref[0] = carry
    y_ref[0] = (_gelu_tanh(gb) * hs[...]).astype(y_ref.dtype)


def _rglru(x, w, b, conv_w, conv_b, gate_w, gate_b, sp, conv_buf, h0, pos0):
    B, L, _ = x.shape
    tl = min(L, 512)
    tail = CONV_W - 1
    const = lambda bi, j: (0, 0)
    per_b = lambda bi, j: (bi, 0, 0)
    return pl.pallas_call(
        functools.partial(_rglru_kernel, tl=tl, pos0=pos0),
        grid=(B, L // tl),
        in_specs=[pl.BlockSpec((1, tl, D_MODEL), lambda bi, j: (bi, j, 0)),
                  pl.BlockSpec((D_MODEL, 2 * B_WIDTH), const),
                  pl.BlockSpec((1, 2 * B_WIDTH), const),
                  pl.BlockSpec((CONV_W, B_WIDTH), const),
                  pl.BlockSpec((1, B_WIDTH), const),
                  pl.BlockSpec((B_WIDTH, 2 * B_WIDTH), const),
                  pl.BlockSpec((1, 2 * B_WIDTH), const),
                  pl.BlockSpec((1, B_WIDTH), const),
                  pl.BlockSpec((1, tail, B_WIDTH), per_b),
                  pl.BlockSpec((1, 1, B_WIDTH), per_b)],
        out_specs=[pl.BlockSpec((1, tl, B_WIDTH), lambda bi, j: (bi, j, 0)),
                   pl.BlockSpec((1, tail, B_WIDTH), per_b),
                   pl.BlockSpec((1, 1, B_WIDTH), per_b)],
        out_shape=[jax.ShapeDtypeStruct((B, L, B_WIDTH), BF16),
                   jax.ShapeDtypeStruct((B, tail, B_WIDTH), F32),
                   jax.ShapeDtypeStruct((B, 1, B_WIDTH), F32)],
        scratch_shapes=[pltpu.VMEM((SUBLANES + tl, B_WIDTH), F32), pltpu.VMEM((1, B_WIDTH), F32),
                        pltpu.VMEM((tl, B_WIDTH), F32)],
        compiler_params=_cparams("parallel", "arbitrary"),
        name="rglru",
    )(x, w, b, conv_w, conv_b, gate_w, gate_b, sp, conv_buf, h0)


def _hgrn_levels(C):
    return [C >> i for i in range(int(math.log2(C)) + 1)]


HGRN_SPLIT = 3


def _hgrn_sum_matrices(C):
    t = np.arange(C)

    def prefix(m):
        return (t[None, :] >= ((t // m) * m)[:, None]) & (t[None, :] <= t[:, None])

    def suffix(m):
        return (t[None, :] > t[:, None]) & (t[None, :] <= ((t // m) * m + m - 1)[:, None])

    mats = [prefix(C), suffix(C)]
    for m in _hgrn_levels(C)[1:-1]:
        upper = ((t // m) % 2 == 1)[:, None]
        mats.append(np.where(upper, prefix(m), suffix(m)))
    one = np.concatenate(mats, axis=0).astype(np.float32)
    return np.concatenate([one] * HGRN_SPLIT, axis=1)


def _hgrn_kernel(x_ref, w_ref, b_ref, lb_ref, ng_ref, dm_ref, s0_ref, y_ref, s_ref, st_s, *, tl, C):
    j = pl.program_id(1)
    levels = _hgrn_levels(C)
    nlev = len(levels)

    @pl.when(j == 0)
    def _():
        for h in range(C_HEADS):
            st_s[h] = s0_ref[0, h].T

    x = x_ref[0].astype(BF16)
    y = _dot(x, w_ref[...]) + b_ref[...]
    lb = lb_ref[...]
    z = y[:, C_WIDTH:2 * C_WIDTH]
    f = lb + (1.0 - lb) * _sigmoid(z)
    q_all = _silu(y[:, :C_WIDTH])
    k_all = (1.0 - lb) * _sigmoid(-z)
    g_all = jnp.log(jnp.maximum(f, F_TINY))
    v_all = y[:, 2 * C_WIDTH:3 * C_WIDTH]
    gate_all = _silu(y[:, 3 * C_WIDTH:])
    states = [st_s[h] for h in range(C_HEADS)]

    ti = lax.broadcasted_iota(jnp.int32, (C, C), 0)
    si = lax.broadcasted_iota(jnp.int32, (C, C), 1)
    masks = [ti == si]
    for m in levels[1:]:
        masks.append((ti // (2 * m) == si // (2 * m)) & ((ti // m) % 2 == 1) & ((si // m) % 2 == 0))
    odd_row = lax.broadcasted_iota(jnp.int32, (C, C_WIDTH), 0) % 2 == 1

    for ci in range(tl // C):
        rows = slice(ci * C, (ci + 1) * C)
        g = g_all[rows, :]
        pieces = []
        rest = g
        for _ in range(HGRN_SPLIT):
            piece = rest.astype(BF16)
            pieces.append(piece)
            rest = rest - piece.astype(F32)
        sums = _dot(dm_ref[...], jnp.concatenate(pieces, axis=0))
        e = jnp.exp(jnp.minimum(sums, 0.0))
        e_one = jnp.exp(jnp.where(odd_row, g, 0.0))
        for h in range(C_HEADS):
            hs = slice(h * C_DK, (h + 1) * C_DK)
            q = q_all[rows, hs]
            k = k_all[rows, hs]
            vb = v_all[rows, hs].astype(BF16)
            scores = jnp.where(masks[0], _dot_nt(q.astype(BF16), k.astype(BF16)), 0.0)
            for li in range(1, nlev):
                el = e[(li + 1) * C:(li + 2) * C, hs] if li < nlev - 1 else e_one[:, hs]
                scores = scores + jnp.where(masks[li], _dot_nt((q * el).astype(BF16), (k * el).astype(BF16)), 0.0)
            e_cum = e[0:C, hs]
            qc = (q * e_cum).astype(BF16)
            kc = (k * e[C:2 * C, hs]).astype(BF16)
            st = states[h]
            o = _dot(scores.astype(BF16), vb) + _dot_nt(qc, st.astype(BF16))
            states[h] = st * e_cum[C - 1:C, :] + _dot_tn(vb, kc)
            ms = jnp.mean(o * o, axis=-1, keepdims=True)
            y_ref[0, rows, hs] = (o * lax.rsqrt(ms + NORM_EPS) * ng_ref[...] * gate_all[rows, hs]).astype(y_ref.dtype)
    for h in range(C_HEADS):
        st_s[h] = states[h]
        s_ref[0, h] = states[h].T


def _hgrn(x, w, b, lb, norm_g, s0):
    B, L, _ = x.shape
    C = min(CHUNK, L)
    tl = min(L, 512)
    dm = jnp.asarray(_hgrn_sum_matrices(C), BF16)
    const = lambda bi, j: (0, 0)
    per_b = lambda bi, j: (bi, 0, 0, 0)
    return pl.pallas_call(
        functools.partial(_hgrn_kernel, tl=tl, C=C),
        grid=(B, L // tl),
        in_specs=[pl.BlockSpec((1, tl, D_MODEL), lambda bi, j: (bi, j, 0)),
                  pl.BlockSpec((D_MODEL, 4 * C_WIDTH), const),
                  pl.BlockSpec((1, 4 * C_WIDTH), const),
                  pl.BlockSpec((1, C_WIDTH), const),
                  pl.BlockSpec((1, C_DV), const),
                  pl.BlockSpec(dm.shape, const),
                  pl.BlockSpec((1, C_HEADS, C_DK, C_DV), per_b)],
        out_specs=[pl.BlockSpec((1, tl, C_WIDTH), lambda bi, j: (bi, j, 0)),
                   pl.BlockSpec((1, C_HEADS, C_DK, C_DV), per_b)],
        out_shape=[jax.ShapeDtypeStruct((B, L, C_WIDTH), BF16),
                   jax.ShapeDtypeStruct((B, C_HEADS, C_DK, C_DV), F32)],
        scratch_shapes=[pltpu.VMEM((C_HEADS, C_DV, C_DK), F32)],
        compiler_params=_cparams("parallel", "arbitrary"),
        name="hgrn",
    )(x, w, b, lb, norm_g, dm, s0)


def _merge_kernel(x_ref, ya_ref, yb_ref, yc_ref, wg_ref, bg_ref, wo_ref, g1_ref, b1_ref, rw_ref, rb_ref,
                  o_ref, cls_ref, *, alpha):
    rs = slice(0, x_ref.shape[0])
    xb = x_ref[rs, :].astype(BF16)
    for c in range(D_MODEL // MERGE_COLS):
        cs = slice(c * MERGE_COLS, (c + 1) * MERGE_COLS)
        pre = alpha * x_ref[rs, cs]
        for n, y_ref in enumerate((ya_ref, yb_ref, yc_ref)):
            gs = slice(n * D_MODEL + c * MERGE_COLS, n * D_MODEL + (c + 1) * MERGE_COLS)
            gate = _sigmoid(_dot(xb, wg_ref[:, gs]) + bg_ref[:, gs])
            pre = pre + gate * _dot(y_ref[rs, :], wo_ref[n * A_WIDTH:(n + 1) * A_WIDTH, cs])
        o_ref[rs, cs] = pre
    x1 = _layer_norm(o_ref[rs, :D_MODEL], g1_ref[...], b1_ref[...])
    o_ref[rs, :D_MODEL] = x1

    lg = _dot(x1.astype(BF16), rw_ref[...]) + rb_ref[...]
    col = [lg[:, e:e + 1] for e in range(N_EXPERTS)]
    gmax = [functools.reduce(jnp.maximum, col[g * EXPERTS_PER_GROUP:(g + 1) * EXPERTS_PER_GROUP])
            for g in range(N_GROUPS)]
    best = functools.reduce(jnp.maximum, gmax)
    gs = jnp.where(gmax[0] == best, 0, jnp.where(gmax[1] == best, 1, jnp.where(gmax[2] == best, 2, 3)))
    v = []
    for e in range(EXPERTS_PER_GROUP):
        v.append(jnp.where(gs == 0, col[e],
                           jnp.where(gs == 1, col[EXPERTS_PER_GROUP + e],
                                     jnp.where(gs == 2, col[2 * EXPERTS_PER_GROUP + e],
                                               col[3 * EXPERTS_PER_GROUP + e]))))
    m1 = best
    i1 = jnp.where(v[0] == m1, 0, jnp.where(v[1] == m1, 1, jnp.where(v[2] == m1, 2, 3)))
    v2 = [jnp.where(i1 == e, -jnp.inf, v[e]) for e in range(EXPERTS_PER_GROUP)]
    m2 = functools.reduce(jnp.maximum, v2)
    i2 = jnp.where(v2[0] == m2, 0, jnp.where(v2[1] == m2, 1, jnp.where(v2[2] == m2, 2, 3)))
    e2 = jnp.exp(m2 - m1)
    p1 = 1.0 / (1.0 + e2)
    p2 = e2 / (1.0 + e2)
    ea = jnp.minimum(i1, i2)
    eb = jnp.maximum(i1, i2)
    ga = jnp.where(i1 < i2, p1, p2)
    gb = jnp.where(i1 < i2, p2, p1)
    pair = jnp.where(ea == 0, 0, jnp.where(ea == 1, 3, 5)) + (eb - ea - 1)
    cls = (gs * PAIRS_PER_GROUP + pair).astype(F32)
    lane = lax.broadcasted_iota(jnp.int32, (rs.stop - rs.start, AUX_W), 1)
    aux = jnp.where(lane == 0, ga, jnp.where(lane == 1, gb, jnp.where(lane == 2, cls, 0.0)))
    o_ref[rs, D_MODEL:] = aux
    pick = (lax.broadcasted_iota(jnp.int32, (8, AUX_W), 0) == 0) & (lax.broadcasted_iota(jnp.int32, (8, AUX_W), 1) == 2)
    cls_ref[:, rs] = _dot_nt(jnp.where(pick, 1.0, 0.0).astype(BF16), aux.astype(BF16))


def _merge(x2d, ya, yb, yc, wg, bg, wo, g1, b1, rw, rb, alpha):
    T = x2d.shape[0]
    tm = min(T, MERGE_ROWS)
    row = lambda i: (i, 0)
    const = lambda i: (0, 0)
    return pl.pallas_call(
        functools.partial(_merge_kernel, alpha=alpha),
        grid=(T // tm,),
        in_specs=[pl.BlockSpec((tm, D_MODEL), row),
                  pl.BlockSpec((tm, A_WIDTH), row),
                  pl.BlockSpec((tm, B_WIDTH), row),
                  pl.BlockSpec((tm, C_WIDTH), row),
                  pl.BlockSpec((D_MODEL, N_BRANCH * D_MODEL), const),
                  pl.BlockSpec((1, N_BRANCH * D_MODEL), const),
                  pl.BlockSpec((N_BRANCH * A_WIDTH, D_MODEL), const),
                  pl.BlockSpec((1, D_MODEL), const),
                  pl.BlockSpec((1, D_MODEL), const),
                  pl.BlockSpec((D_MODEL, N_EXPERTS), const),
                  pl.BlockSpec((1, N_EXPERTS), const)],
        out_specs=[pl.BlockSpec((tm, D_MODEL + AUX_W), row), pl.BlockSpec((8, tm), lambda i: (0, i))],
        out_shape=[jax.ShapeDtypeStruct((T, D_MODEL + AUX_W), F32), jax.ShapeDtypeStruct((8, T), F32)],
        compiler_params=_cparams("parallel"),
        name="merge",
    )(x2d, ya, yb, yc, wg, bg, wo, g1, b1, rw, rb)


def _dispatch_kernel(dest_ref, x_ref, out_hbm, zbuf, sem, *, tm, n_tok_steps):
    i = pl.program_id(0)

    def scatter_rows(src):
        for r in range(tm):
            pltpu.make_async_copy(src.at[pl.ds(r, 1), :], out_hbm.at[pl.ds(dest_ref[0, 0, r], 1), :], sem).start()
        pltpu.make_async_copy(src, out_hbm.at[pl.ds(0, tm), :], sem).wait()

    @pl.when(i < n_tok_steps)
    def _():
        scatter_rows(x_ref)

    @pl.when(i >= n_tok_steps)
    def _():
        @pl.when(i == n_tok_steps)
        def _():
            zbuf[...] = jnp.zeros(zbuf.shape, F32)

        scatter_rows(zbuf)


def _dispatch(x1w, dest_all, n_rows):
    T, W = x1w.shape
    tm = min(T, MOE_DMA_ROWS)
    assert T % tm == 0 and dest_all.shape[0] % tm == 0
    n_tok_steps = T // tm
    steps = dest_all.shape[0] // tm
    return pl.pallas_call(
        functools.partial(_dispatch_kernel, tm=tm, n_tok_steps=n_tok_steps),
        grid=(steps,),
        in_specs=[pl.BlockSpec((1, 1, tm), lambda i: (i, 0, 0), memory_space=pltpu.SMEM),
                  pl.BlockSpec((tm, W), lambda i: (jnp.minimum(i, n_tok_steps - 1), 0))],
        out_specs=pl.BlockSpec(memory_space=pl.ANY),
        out_shape=jax.ShapeDtypeStruct((n_rows, W), F32),
        scratch_shapes=[pltpu.VMEM((tm, W), F32), pltpu.SemaphoreType.DMA(())],
        compiler_params=_cparams("arbitrary"),
        name="moe_dispatch",
    )(dest_all.reshape(steps, 1, tm), x1w)


def _experts_kernel(ea_ref, eb_ref, nv_ref, x_ref, w1a, w3a, w2a, w1b, w3b, w2b, g2_ref, b2_ref, o_ref, *, alpha):
    i = pl.program_id(0)

    @pl.when(nv_ref[i] > 0)
    def _():
        xw = x_ref[...]
        x = xw[:, :D_MODEL]
        xb = x.astype(BF16)
        y = None
        for n, (w1, w3, w2) in enumerate(((w1a, w3a, w2a), (w1b, w3b, w2b))):
            hmid = _silu(_dot(xb, w1[0])) * _dot(xb, w3[0])
            ye = xw[:, D_MODEL + n:D_MODEL + n + 1] * _dot(hmid.astype(BF16), w2[0])
            y = ye if y is None else y + ye
        o_ref[...] = _layer_norm(alpha * x + y, g2_ref[...], b2_ref[...])

    @pl.when(nv_ref[i] == 0)
    def _():
        o_ref[...] = jnp.zeros(o_ref.shape, F32)


def _experts(xs, ea, eb, nv, w1, w3, w2, g2, b2, alpha):
    n_rows, W = xs.shape
    wa = lambda i, ea, eb, nv: (ea[i], 0, 0)
    wb = lambda i, ea, eb, nv: (eb[i], 0, 0)
    const = lambda i, ea, eb, nv: (0, 0)
    row = lambda i, ea, eb, nv: (i, 0)
    up = (1, D_MODEL, D_EXPERT)
    down = (1, D_EXPERT, D_MODEL)
    grid_spec = pltpu.PrefetchScalarGridSpec(
        num_scalar_prefetch=3,
        grid=(n_rows // MOE_ROWS,),
        in_specs=[pl.BlockSpec((MOE_ROWS, W), row),
                  pl.BlockSpec(up, wa), pl.BlockSpec(up, wa), pl.BlockSpec(down, wa),
                  pl.BlockSpec(up, wb), pl.BlockSpec(up, wb), pl.BlockSpec(down, wb),
                  pl.BlockSpec((1, D_MODEL), const), pl.BlockSpec((1, D_MODEL), const)],
        out_specs=pl.BlockSpec((MOE_ROWS, D_MODEL), row),
    )
    return pl.pallas_call(
        functools.partial(_experts_kernel, alpha=alpha),
        grid_spec=grid_spec,
        out_shape=jax.ShapeDtypeStruct((n_rows, D_MODEL), F32),
        compiler_params=_cparams("arbitrary"),
        name="moe_experts",
    )(ea, eb, nv, xs, w1, w3, w2, w1, w3, w2, g2, b2)


def _combine_kernel(dest_ref, y_hbm, o_ref, sem, *, tm):
    for r in range(tm):
        pltpu.make_async_copy(y_hbm.at[pl.ds(dest_ref[0, 0, r], 1), :], o_ref.at[pl.ds(r, 1), :], sem).start()
    pltpu.make_async_copy(y_hbm.at[pl.ds(0, tm), :], o_ref, sem).wait()


def _combine(ys, dest, T):
    tm = min(T, MOE_DMA_ROWS)
    return pl.pallas_call(
        functools.partial(_combine_kernel, tm=tm),
        grid=(T // tm,),
        in_specs=[pl.BlockSpec((1, 1, tm), lambda i: (i, 0, 0), memory_space=pltpu.SMEM),
                  pl.BlockSpec(memory_space=pl.ANY)],
        out_specs=pl.BlockSpec((tm, D_MODEL), lambda i: (i, 0)),
        out_shape=jax.ShapeDtypeStruct((T, D_MODEL), F32),
        scratch_shapes=[pltpu.SemaphoreType.DMA(())],
        compiler_params=_cparams("arbitrary"),
        name="moe_combine",
    )(dest.reshape(T // tm, 1, tm), ys)


def _route(cls, T):
    n_blocks = T // MOE_ROWS + N_CLASSES
    n_rows = n_blocks * MOE_ROWS
    rb = math.gcd(T, MOE_ROWS)
    onehot = (cls[:, None] == jnp.arange(N_CLASSES, dtype=jnp.int32)[None, :]).astype(F32).reshape(T // rb, rb, N_CLASSES)
    strict = jnp.asarray(np.tril(np.ones((rb, rb), np.float32), -1))
    rank_in = jnp.einsum('ij,bjc->bic', strict, onehot)
    blk_counts = jnp.sum(onehot, axis=1)
    blk_off = jnp.cumsum(blk_counts, axis=0) - blk_counts
    rank = jnp.sum(onehot * (rank_in + blk_off[:, None, :]), axis=-1).reshape(T).astype(jnp.int32)
    counts = jnp.sum(blk_counts, axis=0).astype(jnp.int32)
    padded = (counts + MOE_ROWS - 1) // MOE_ROWS * MOE_ROWS
    pad_end = jnp.cumsum(padded)
    pad_start = pad_end - padded
    dest = (pad_start[cls] + rank).astype(jnp.int32)
    seg_start = jnp.concatenate([pad_start + counts, pad_end[-1:]])
    seg_len = jnp.concatenate([padded - counts, n_rows - pad_end[-1:]])
    seg_end = jnp.cumsum(seg_len)
    kk = jnp.arange(n_rows - T, dtype=jnp.int32)
    seg = jnp.sum((seg_end[None, :] <= kk[:, None]).astype(jnp.int32), axis=1)
    pad_rows = (seg_start[seg] + kk - (seg_end[seg] - seg_len[seg])).astype(jnp.int32)
    blk_start = jnp.arange(n_blocks, dtype=jnp.int32) * MOE_ROWS
    blk_cls = jnp.minimum(jnp.sum((pad_end[None, :] <= blk_start[:, None]).astype(jnp.int32), axis=1), N_CLASSES - 1)
    nv = jnp.clip(pad_start[blk_cls] + counts[blk_cls] - blk_start, 0, MOE_ROWS).astype(jnp.int32)
    grp = blk_cls // PAIRS_PER_GROUP
    pair = blk_cls % PAIRS_PER_GROUP
    pa = jnp.asarray([0, 0, 0, 1, 1, 2], jnp.int32)[pair]
    pb = jnp.asarray([1, 2, 3, 2, 3, 3], jnp.int32)[pair]
    ea = grp * EXPERTS_PER_GROUP + pa
    eb = grp * EXPERTS_PER_GROUP + pb
    return ea, eb, nv, dest, pad_rows, n_rows


def _moe(x1w, cls_rows, w1, w3, w2, g2, b2, alpha):
    T = x1w.shape[0]
    cls = cls_rows[0].astype(jnp.int32)
    ea, eb, nv, dest, pad_rows, n_rows = _route(cls, T)
    xs = _dispatch(x1w, jnp.concatenate([dest, pad_rows]), n_rows)
    ys = _experts(xs, ea, eb, nv, w1, w3, w2, g2, b2, alpha)
    return _combine(ys, dest, T)


def _block_diag(w):
    n, d, _ = w.shape
    eye = jnp.eye(n, dtype=w.dtype)
    return jnp.einsum('nde,nm->ndme', w, eye).reshape(n * d, n * d)


def _layer(x, pos0, li, P, past, kv_carried):
    B, L, _ = x.shape
    T = B * L
    depth = P['depth']
    alpha = (2 * depth) ** 0.25
    w_in, b_in = P['w_in'][li], P['b_in'][li]

    def cols(lo, hi):
        return w_in[:, lo:hi], b_in[lo:hi].reshape(1, -1)

    lam_init = 0.8 - 0.6 * math.exp(-0.3 * li)
    dl = P['diff_lambda'][li]
    lam = jnp.exp(jnp.sum(dl[0] * dl[1])) - jnp.exp(jnp.sum(dl[2] * dl[3])) + lam_init
    lam_row = jnp.full((1, LANES), lam, F32)
    subln_g = P['diff_subln_g'][li].reshape(1, -1)

    wa, ba = cols(OFF_A, OFF_B)
    q, k_all, kb, v_all, vb, vt = _attn_proj(x, wa, ba, pos0, li, depth, kv_carried)
    if past is None:
        y_a = _flash_prompt(q, kb, vt, lam_row, subln_g.reshape(-1, 1), 1.0 - lam_init)
        conv_buf = jnp.zeros((B, CONV_W - 1, B_WIDTH), F32)
        h0 = jnp.zeros((B, 1, B_WIDTH), F32)
        s0 = jnp.zeros((B, C_HEADS, C_DK, C_DV), F32)
    else:
        kt_cache, v_cache, conv_buf, h0, s0 = past
        y_a = _decode_attention(q, kb, vb, kt_cache, v_cache, li, lam_row, subln_g, 1.0 - lam_init)
        h0 = h0.reshape(B, 1, B_WIDTH)

    wb, bb = cols(OFF_B, OFF_C)
    y_b, conv_new, h_new = _rglru(x, wb, bb, P['conv_w'][li], P['conv_b'][li].reshape(1, -1),
                                  P['lru_gate_w'][li], P['lru_gate_b'][li], P['lru_sp'][li].reshape(1, -1),
                                  conv_buf, h0, pos0)
    wc, bc = cols(OFF_C, OFF_G)
    y_c, s_new = _hgrn(x, wc, bc, P['lower_bounds'][li].reshape(1, -1), P['hgrn_norm_g'][li].reshape(1, -1), s0)

    wg, bg = cols(OFF_G, N_IN)
    x1w, cls_rows = _merge(x.reshape(T, D_MODEL), y_a.reshape(T, A_WIDTH), y_b.reshape(T, B_WIDTH),
                 y_c.reshape(T, C_WIDTH), wg, bg, P['w_out'][li], P['ln1_g'][li].reshape(1, -1),
                 P['ln1_b'][li].reshape(1, -1), P['router_w'], P['router_b'].reshape(1, -1), alpha)
    x2 = _moe(x1w, cls_rows, P['moe_w1'][li], P['moe_w3'][li], P['moe_w2'][li], P['ln2_g'][li].reshape(1, -1),
              P['ln2_b'][li].reshape(1, -1), alpha)
    states = (conv_new, h_new.reshape(B, B_WIDTH), s_new)
    return x2.reshape(B, L, D_MODEL), (k_all, v_all), states


def kernel(x_prompt, x_sample, cache_k, cache_v, state_conv, state_lru, state_hgrn, ln_in_g, ln_in_b, w_in, b_in, diff_lambda, diff_subln_g, conv_w, conv_b, lru_wa, lru_ba, lru_wx, lru_bx, lru_lambda, hgrn_lb, hgrn_norm_g, w_out, ln1_g, ln1_b, router_w, router_b, moe_w1, moe_w3, moe_w2, ln2_g, ln2_b):
    depth = w_in.shape[0]
    p_lb = jax.nn.softmax(hgrn_lb.astype(F32), axis=0)
    P = dict(
        depth=depth,
        w_in=w_in.astype(BF16), b_in=b_in,
        diff_lambda=diff_lambda.astype(F32), diff_subln_g=diff_subln_g,
        conv_w=conv_w, conv_b=conv_b,
        lru_gate_w=jnp.stack([jnp.concatenate([_block_diag(lru_wa[l]), _block_diag(lru_wx[l])], axis=1)
                              for l in range(depth)]).astype(BF16),
        lru_gate_b=jnp.concatenate([lru_ba, lru_bx], axis=1).reshape(depth, 1, -1),
        lru_sp=jax.nn.softplus(-lru_lambda.astype(F32)),
        lower_bounds=jnp.cumsum(p_lb, axis=0) - p_lb[0],
        hgrn_norm_g=hgrn_norm_g,
        w_out=w_out.astype(BF16), ln1_g=ln1_g, ln1_b=ln1_b,
        router_w=router_w.astype(BF16), router_b=router_b,
        moe_w1=moe_w1.astype(BF16), moe_w3=moe_w3.astype(BF16), moe_w2=moe_w2.astype(BF16),
        ln2_g=ln2_g, ln2_b=ln2_b,
    )
    Bp, Lp, _ = x_prompt.shape
    Bs, Ls, _ = x_sample.shape
    past_len = cache_k.shape[2]
    kt_cache = jnp.transpose(cache_k, (0, 1, 3, 4, 5, 2)).reshape(depth, Bs, A_QK_WIDTH, past_len)
    v_cache = cache_v.reshape(depth, Bs, past_len * A_HEADS, A_V_DIM)
    xp = _input_ln(x_prompt.reshape(Bp * Lp, D_MODEL), ln_in_g, ln_in_b).reshape(Bp, Lp, D_MODEL)
    xs = _input_ln(x_sample.reshape(Bs * Ls, D_MODEL), ln_in_g, ln_in_b).reshape(Bs, Ls, D_MODEL)
    prompt_states, sample_states = [], []
    kv_p = kv_s = None
    for li in range(depth):
        xp, kv_p, st_p = _layer(xp, 0, li, P, None, kv_p)
        xs, kv_s, st_s = _layer(xs, past_len, li, P,
                                (kt_cache, v_cache, state_conv[li], state_lru[li], state_hgrn[li]), kv_s)
        prompt_states.append(st_p)
        sample_states.append(st_s)
    conv_p, lru_p, hgrn_p = [jnp.stack(t) for t in zip(*prompt_states)]
    conv_s, lru_s, hgrn_s = [jnp.stack(t) for t in zip(*sample_states)]
    k_p = kv_p[0].reshape(depth, Bp, Lp, A_HEADS, 2, A_HEAD_DIM)
    v_p = kv_p[1].reshape(depth, Bp, Lp, A_HEADS, A_V_DIM)
    k_s = kv_s[0].reshape(depth, Bs, Ls, A_HEADS, 2, A_HEAD_DIM)
    v_s = kv_s[1].reshape(depth, Bs, Ls, A_HEADS, A_V_DIM)
    return (xp, xs, k_p, v_p, conv_p, lru_p, hgrn_p, k_s, v_s, conv_s, lru_s, hgrn_s)
```
